```python
import math
import jax, jax.numpy as jnp
from jax import lax
import numpy as np

D_MODEL = 1024
BATCH = 32
SEQ = 256
DEPTH = 1
DEC_BATCH = 8
DEC_SEQ = 2048
PAST_LEN = 256

GRID_W = 64
D_MIX = D_MODEL
D_SSM = D_MIX // 2
D_POOL = D_MIX - D_SSM
SSM_H = 16
SSM_G = D_SSM // SSM_H
SSM_P = 64
POOL_WINDOWS = (2, 4, 8, 16)
POOL_G = len(POOL_WINDOWS)
POOL_C = D_POOL // POOL_G
PEER_HEADS = 8
PEER_NKEYS = 128
PEER_N = PEER_NKEYS * PEER_NKEYS
PEER_DKEY = 256
PEER_DHALF = PEER_DKEY // 2
PEER_TOPK = 16
PEER_BLOCK = 128
N_MOD = 6
EPS = 1e-6

kernel_name = 'hymba_s5_pool_peer_diffusion_step'

F32 = jnp.float32


def rmsnorm(x, g):
    x32 = x.astype(F32)
    y = x32 * lax.rsqrt(jnp.mean(x32 * x32, axis=-1, keepdims=True) + EPS)
    return (y * g.astype(F32)).astype(x.dtype)


def adaln(cond, w_mod, b_mod):
    m = jax.nn.silu(cond) @ w_mod + b_mod
    return jnp.split(m, N_MOD, axis=-1)


def modulate(h, shift, scale):
    return h * (1 + scale[..., None, :]) + shift[..., None, :]


def s5_discretise(lam_re, lam_im, log_dt, b_re, b_im):
    lr, li = lam_re.astype(F32), lam_im.astype(F32)
    dt = jnp.exp(log_dt.astype(F32))[:, None]
    mag = jnp.exp(lr * dt)
    ar, ai = mag * jnp.cos(li * dt), mag * jnp.sin(li * dt)
    den = lr * lr + li * li
    nr, ni = ar - 1.0, ai
    fr = (nr * lr + ni * li) / den
    fi = (ni * lr - nr * li) / den
    br, bi = b_re.astype(F32), b_im.astype(F32)
    bbr = fr[..., None] * br - fi[..., None] * bi
    bbi = fr[..., None] * bi + fi[..., None] * br
    return ar, ai, bbr, bbi


def _linrec_op(e1, e2):
    a1r, a1i, b1r, b1i = e1
    a2r, a2i, b2r, b2i = e2
    return (a2r * a1r - a2i * a1i,
            a2r * a1i + a2i * a1r,
            a2r * b1r - a2i * b1i + b2r,
            a2r * b1i + a2i * b1r + b2i)


def s5_direction(u, h0r, h0i, lam_re, lam_im, log_dt, b_re, b_im, c_re, c_im, reverse):
    ar, ai, bbr, bbi = s5_discretise(lam_re, lam_im, log_dt, b_re, b_im)
    if reverse:
        u = jnp.flip(u, axis=1)
    br = jnp.einsum('blgh,gph->blgp', u, bbr)
    bi = jnp.einsum('blgh,gph->blgp', u, bbi)
    h0r, h0i = h0r.astype(F32), h0i.astype(F32)
    br = br.at[:, 0].add(ar * h0r - ai * h0i)
    bi = bi.at[:, 0].add(ar * h0i + ai * h0r)
    shape = br.shape
    scanned = lax.associative_scan(
        _linrec_op, (jnp.broadcast_to(ar, shape), jnp.broadcast_to(ai, shape), br, bi), axis=1)
    hr, hi = scanned[2], scanned[3]
    y = (jnp.einsum('blgp,ghp->blgh', hr, c_re.astype(F32))
         - jnp.einsum('blgp,ghp->blgh', hi, c_im.astype(F32)))
    if reverse:
        y = jnp.flip(y, axis=1)
    return y, hr[:, -1], hi[:, -1]


def s5_mixer(xs, h0r, h0i, p):
    B, L, _ = xs.shape
    u = xs.astype(F32).reshape(B, L, SSM_G, SSM_H)
    y = p['ssm_d'].astype(F32).reshape(SSM_G, SSM_H) * u
    srs, sis = [], []
    for d in range(2):
        yd, sr, si = s5_direction(u, h0r[:, d], h0i[:, d], p['ssm_lambda_re'][d], p['ssm_lambda_im'][d],
                                  p['ssm_log_dt'][d], p['ssm_b_re'][d], p['ssm_b_im'][d],
                                  p['ssm_c_re'][d], p['ssm_c_im'][d], reverse=(d == 1))
        y = y + yd
        srs.append(sr)
        sis.append(si)
    g = jax.nn.gelu(y.reshape(B, L, D_SSM))
    out = g * jax.nn.sigmoid(g @ p['w_glu'].astype(F32))
    return out.astype(xs.dtype), jnp.stack(srs, axis=1), jnp.stack(sis, axis=1)


def window_mean(x, w, axis):
    L = x.shape[axis]
    cs = jnp.cumsum(x.astype(F32), axis=axis)
    cs = jnp.concatenate([jnp.zeros_like(lax.slice_in_dim(cs, 0, 1, axis=axis)), cs], axis=axis)
    t = jnp.arange(L)
    lo = jnp.clip(t - w // 2, 0, L)
    hi = jnp.clip(t + (w - w // 2), 0, L)
    s = jnp.take(cs, hi, axis=axis) - jnp.take(cs, lo, axis=axis)
    shape = [1] * x.ndim
    shape[axis] = L
    return s / (hi - lo).astype(F32).reshape(shape)


def pool_mixer(xp, w_pool, pool_scale, grid):
    B, L, _ = xp.shape
    xg = xp.reshape(B, L, POOL_G, POOL_C)
    if grid:
        rows = L // GRID_W
        xw, axis = xg.reshape(B, rows, GRID_W, POOL_G, POOL_C), 2
    else:
        xw, axis = xg, 1
    pooled = jnp.stack([window_mean(xw[..., gi, :], w, axis) for gi, w in enumerate(POOL_WINDOWS)], axis=-2)
    pooled = pooled.reshape(B, L, POOL_G, POOL_C) - xg.astype(F32)
    out = jnp.einsum('blgc,gcd->blgd', pooled, w_pool.astype(F32)).reshape(B, L, D_POOL)
    return (out * pool_scale.astype(F32)).astype(xp.dtype)


def peer_ffn(h, w_q, sub_keys, u_tab, v_tab):
    B, L, D = h.shape
    T = B * L
    hf = h.reshape(T, D)
    q = (hf @ w_q).astype(F32).reshape(T, PEER_HEADS, 2, PEER_DHALF)
    s = jnp.einsum('thsk,hsnk->thsn', q, sub_keys.astype(F32))
    sv, si = lax.top_k(s, PEER_TOPK)
    cand_s = (sv[:, :, 0, :, None] + sv[:, :, 1, None, :]).reshape(T, PEER_HEADS, PEER_TOPK * PEER_TOPK)
    cand_i = (si[:, :, 0, :, None] * PEER_NKEYS + si[:, :, 1, None, :]).reshape(T, PEER_HEADS, PEER_TOPK * PEER_TOPK)
    top_s, pos = lax.top_k(cand_s, PEER_TOPK)
    experts = jnp.take_along_axis(cand_i, pos, axis=-1)
    gates = jax.nn.softmax(top_s, axis=-1)
    nb = T // PEER_BLOCK

    def block(args):
        xb, eb, gb = args
        u = u_tab[eb]
        act = jnp.einsum('td,thkd->thk', xb, u).astype(F32)
        wgt = (gb * jax.nn.gelu(act)).astype(xb.dtype)
        return jnp.einsum('thk,thkd->td', wgt, v_tab[eb])

    out = lax.map(block, (hf.reshape(nb, PEER_BLOCK, D),
                          experts.reshape(nb, PEER_BLOCK, PEER_HEADS, PEER_TOPK),
                          gates.reshape(nb, PEER_BLOCK, PEER_HEADS, PEER_TOPK)))
    return out.reshape(B, L, D).astype(h.dtype)


def trunk_layer(x, cond, h0r, h0i, grid, p):
    sh1, sc1, g1, sh2, sc2, g2 = adaln(cond, p['w_mod'], p['b_mod'])
    h = modulate(rmsnorm(x, p['norm1_g']), sh1, sc1)
    z = h @ p['w_in']
    ys, sr, si = s5_mixer(z[..., :D_SSM], h0r, h0i, p)
    yp = pool_mixer(z[..., D_SSM:], p['w_pool'], p['pool_scale'], grid)
    mix = jnp.concatenate([ys.astype(x.dtype), yp.astype(x.dtype)], axis=-1) @ p['w_out']
    x = x + g1[..., None, :] * mix
    h = modulate(rmsnorm(x, p['norm2_g']), sh2, sc2)
    x = x + g2[..., None, :] * peer_ffn(h, p['peer_wq'], p['peer_subkeys'], p['peer_u'], p['peer_v'])
    return x, sr, si


def setup_inputs(seed: int = 0) -> dict:
    key = jax.random.key(seed)
    ks = jax.random.split(key, 32)
    nrm = lambda k, shape, s: jax.random.normal(k, shape, F32) * s
    n_idx = jnp.arange(SSM_P, dtype=F32)
    lam_re = -0.5 + nrm(ks[7], (DEPTH, 2, SSM_G, SSM_P), 0.01)
    lam_im = math.pi * n_idx + nrm(ks[8], (DEPTH, 2, SSM_G, SSM_P), 0.01)
    log_dt = jax.random.uniform(ks[9], (DEPTH, 2, SSM_G), F32, math.log(1e-3), math.log(1e-1))
    return {
        'x_prompt': nrm(ks[0], (BATCH, SEQ, D_MODEL), 1.0),
        'x_sample': nrm(ks[1], (DEC_BATCH, DEC_SEQ, D_MODEL), 1.0),
        'state_ssm_re': nrm(ks[2], (DEC_BATCH, DEPTH, 2, SSM_G, SSM_P), 0.5),
        'state_ssm_im': nrm(ks[3], (DEC_BATCH, DEPTH, 2, SSM_G, SSM_P), 0.5),
        'c': nrm(ks[4], (DEC_BATCH, D_MODEL), 1.0),
        'c_ctx': nrm(ks[5], (D_MODEL,), 1.0),
        'norm1_g': 1.0 + nrm(ks[6], (DEPTH, D_MODEL), 0.02),
        'w_mod': nrm(ks[10], (DEPTH, D_MODEL, N_MOD * D_MODEL), 0.5 * D_MODEL ** -0.5),
        'b_mod': nrm(ks[11], (DEPTH, N_MOD * D_MODEL), 0.02),
        'w_in': nrm(ks[12], (DEPTH, D_MODEL, D_MIX), D_MODEL ** -0.5),
        'ssm_lambda_re': lam_re,
        'ssm_lambda_im': lam_im,
        'ssm_log_dt': log_dt,
        'ssm_b_re': nrm(ks[13], (DEPTH, 2, SSM_G, SSM_P, SSM_H), (2 * SSM_H) ** -0.5),
        'ssm_b_im': nrm(ks[14], (DEPTH, 2, SSM_G, SSM_P, SSM_H), (2 * SSM_H) ** -0.5),
        'ssm_c_re': nrm(ks[15], (DEPTH, 2, SSM_G, SSM_H, SSM_P), (2 * SSM_P) ** -0.5),
        'ssm_c_im': nrm(ks[16], (DEPTH, 2, SSM_G, SSM_H, SSM_P), (2 * SSM_P) ** -0.5),
        'ssm_d': nrm(ks[17], (DEPTH, D_SSM), 1.0),
        'w_glu': nrm(ks[18], (DEPTH, D_SSM, D_SSM), D_SSM ** -0.5),
        'w_pool': nrm(ks[19], (DEPTH, POOL_G, POOL_C, POOL_C), POOL_C ** -0.5),
        'pool_scale': 1.0 + nrm(ks[20], (DEPTH, D_POOL), 0.1),
        'w_out': nrm(ks[21], (DEPTH, D_MIX, D_MODEL), D_MIX ** -0.5),
        'norm2_g': 1.0 + nrm(ks[22], (DEPTH, D_MODEL), 0.02),
        'peer_wq': nrm(ks[23], (DEPTH, D_MODEL, PEER_HEADS * PEER_DKEY), D_MODEL ** -0.5),
        'peer_subkeys': nrm(ks[24], (DEPTH, PEER_HEADS, 2, PEER_NKEYS, PEER_DHALF), PEER_DHALF ** -0.5),
        'peer_u': nrm(ks[25], (DEPTH, PEER_N, D_MODEL), D_MODEL ** -0.5),
        'peer_v': nrm(ks[26], (DEPTH, PEER_N, D_MODEL), 1.0),
        'final_g': 1.0 + nrm(ks[27], (D_MODEL,), 0.02),
    }


def reference(x_prompt, x_sample, state_ssm_re, state_ssm_im, c, c_ctx, norm1_g, w_mod, b_mod, w_in,
              ssm_lambda_re, ssm_lambda_im, ssm_log_dt, ssm_b_re, ssm_b_im, ssm_c_re, ssm_c_im, ssm_d,
              w_glu, w_pool, pool_scale, w_out, norm2_g, peer_wq, peer_subkeys, peer_u, peer_v, final_g):
    xp = x_prompt
    xs = x_sample
    zeros = jnp.zeros((x_prompt.shape[0], 2, SSM_G, SSM_P), F32)
    new_re, new_im = [], []
    for l in range(DEPTH):
        p = {
            'norm1_g': norm1_g[l], 'w_mod': w_mod[l], 'b_mod': b_mod[l], 'w_in': w_in[l],
            'ssm_lambda_re': ssm_lambda_re[l], 'ssm_lambda_im': ssm_lambda_im[l], 'ssm_log_dt': ssm_log_dt[l],
            'ssm_b_re': ssm_b_re[l], 'ssm_b_im': ssm_b_im[l], 'ssm_c_re': ssm_c_re[l], 'ssm_c_im': ssm_c_im[l],
            'ssm_d': ssm_d[l], 'w_glu': w_glu[l], 'w_pool': w_pool[l], 'pool_scale': pool_scale[l],
            'w_out': w_out[l], 'norm2_g': norm2_g[l], 'peer_wq': peer_wq[l], 'peer_subkeys': peer_subkeys[l],
            'peer_u': peer_u[l], 'peer_v': peer_v[l],
        }
        xp, sr, si = trunk_layer(xp, c_ctx, zeros, zeros, False, p)
        new_re.append(sr)
        new_im.append(si)
        xs, _, _ = trunk_layer(xs, c, state_ssm_re[:, l], state_ssm_im[:, l], True, p)
    y_prompt = rmsnorm(xp, final_g)
    y_sample = rmsnorm(xs, final_g)
    new_state_ssm_re = jnp.stack(new_re, axis=1)
    new_state_ssm_im = jnp.stack(new_im, axis=1)
    return (y_prompt, y_sample, new_state_ssm_re, new_state_ssm_im)
```

```python
import functools
import math

import numpy as np
import jax
import jax.numpy as jnp
from jax import lax
from jax.experimental import pallas as pl
from jax.experimental.pallas import tpu as pltpu

F32 = jnp.float32
BF16 = jnp.bfloat16

D_MODEL = 1024
D_SSM = 512
D_POOL = 512
SSM_H = 16
SSM_G = 32
SSM_P = 64
SSM_N = SSM_G * SSM_P
POOL_WINDOWS = (2, 4, 8, 16)
POOL_G = 4
POOL_C = 128
GRID_W = 64
PEER_HEADS = 8
PEER_NKEYS = 128
PEER_N = PEER_NKEYS * PEER_NKEYS
PEER_DHALF = 128
PEER_TOPK = 16
N_MOD = 6
EPS = 1e-6

SUBLANES = 8
LANES = 128
VMEM_LIMIT = 56 * 1024 * 1024

COND_ROWS = 16
CTX_ROW = 8
SEQ_TILE = 256
S5_CHUNK = 64
S5_LANES = 512
PREP_TILE = 256
PEER_TM = 512
PEER_TE = 1024


def _gelu(x):
    return 0.5 * x * (1.0 + jnp.tanh(0.7978845608028654 * (x + 0.044715 * (x * x * x))))


def _split_bf16(x):
    hi = x.astype(BF16)
    lo = (x - hi.astype(F32)).astype(BF16)
    return hi, lo


def _dot(a, b):
    return jnp.dot(a, b, preferred_element_type=F32)


def _params(sem):
    return pltpu.CompilerParams(dimension_semantics=sem, vmem_limit_bytes=VMEM_LIMIT)


def _mod_kernel(cond_ref, whi_ref, wlo_ref, b_ref, o_ref):
    c = cond_ref[...]
    s = c * jax.nn.sigmoid(c)
    shi, slo = _split_bf16(s)
    whi = whi_ref[...]
    o_ref[...] = _dot(shi, whi) + _dot(slo, whi) + _dot(shi, wlo_ref[...]) + b_ref[...]


def _mod_vectors(cond, w_mod, b_mod):
    whi, wlo = _split_bf16(w_mod)
    n = w_mod.shape[1]
    bn = D_MODEL
    return pl.pallas_call(
        _mod_kernel,
        grid=(n // bn,),
        in_specs=[
            pl.BlockSpec((COND_ROWS, D_MODEL), lambda k: (0, 0)),
            pl.BlockSpec((D_MODEL, bn), lambda k: (0, k)),
            pl.BlockSpec((D_MODEL, bn), lambda k: (0, k)),
            pl.BlockSpec((1, bn), lambda k: (0, k)),
        ],
        out_specs=pl.BlockSpec((COND_ROWS, bn), lambda k: (0, k)),
        out_shape=jax.ShapeDtypeStruct((COND_ROWS, n), F32),
        compiler_params=_params(("arbitrary",)),
        name="mod",
    )(cond, whi, wlo, b_mod.reshape(1, n))


def _disc_kernel(lr_ref, li_ref, ldt_ref, br_ref, bi_ref, ar_ref, ai_ref, bbr_ref, bbi_ref):
    lr = lr_ref[...]
    li = li_ref[...]
    dt = jnp.exp(ldt_ref[...])
    mag = jnp.exp(lr * dt)
    ar = mag * jnp.cos(li * dt)
    ai = mag * jnp.sin(li * dt)
    den = lr * lr + li * li
    nr = ar - 1.0
    ni = ai
    fr = (nr * lr + ni * li) / den
    fi = (ni * lr - nr * li) / den
    ar_ref[...] = ar
    ai_ref[...] = ai
    br = br_ref[...]
    bi = bi_ref[...]
    frb = fr[:, None, :]
    fib = fi[:, None, :]
    bbr_ref[...] = frb * br - fib * bi
    bbi_ref[...] = frb * bi + fib * br


def _discretise(lam_re, lam_im, log_dt, b_re, b_im):
    rows = 2 * SSM_G
    lr = lam_re.reshape(rows, SSM_P)
    li = lam_im.reshape(rows, SSM_P)
    ldt = jnp.broadcast_to(log_dt.reshape(rows, 1), (rows, SSM_P))
    br = jnp.swapaxes(b_re, -1, -2).reshape(rows, SSM_H, SSM_P)
    bi = jnp.swapaxes(b_im, -1, -2).reshape(rows, SSM_H, SSM_P)
    small = jax.ShapeDtypeStruct((rows, SSM_P), F32)
    big = jax.ShapeDtypeStruct((rows, SSM_H, SSM_P), F32)
    return pl.pallas_call(_disc_kernel, out_shape=(small, small, big, big), name="disc")(lr, li, ldt, br, bi)


def _s5_weights(ar, ai, bbr, bbi, c_re, c_im):
    half_g = SSM_G // 2
    eye = jnp.eye(half_g, dtype=F32)
    a_re = ar.reshape(2, 1, SSM_N)
    a_im = ai.reshape(2, 1, SSM_N)

    def bmat(t):
        t = t.reshape(2, 2, half_g, SSM_H, SSM_P)
        m = jnp.einsum("djghp,gk->djghkp", t, eye)
        return m.reshape(2, 2, half_g * SSM_H, half_g * SSM_P)

    def cmat(t):
        t = t.reshape(2, 2, half_g, SSM_H, SSM_P)
        m = jnp.einsum("djghp,gk->djgpkh", t, eye)
        return m.reshape(2, 2, half_g * SSM_P, half_g * SSM_H)

    wb = jnp.concatenate([bmat(bbr), bmat(bbi)], axis=-1).astype(BF16)
    wc_re = cmat(c_re).astype(BF16)
    wc_im = cmat(c_im).astype(BF16)
    return a_re, a_im, wb, wc_re, wc_im


def _inproj_kernel(x_ref, sh_ref, sc_ref, g_ref, w_ref, zs_ref, zp_ref):
    x = x_ref[0]
    ms = jnp.mean(x * x, axis=-1, keepdims=True)
    y = x * lax.rsqrt(ms + EPS) * g_ref[...]
    h = y * (1.0 + sc_ref[0, 0]) + sh_ref[0, 0]
    z = _dot(h.astype(BF16), w_ref[...])
    zs_ref[...] = z[:, :D_SSM]
    zp_ref[0] = z[:, D_SSM:]


def _mod_spec(k, per_batch, tokens_per_row=None):
    if per_batch:
        return pl.BlockSpec((1, 1, 1, D_MODEL), lambda b, i: (b, k, 0, 0))
    return pl.BlockSpec((1, 1, 1, D_MODEL), lambda b, i: (CTX_ROW, k, 0, 0))


def _inproj(x, mods, g, w_in, per_batch):
    B, L, _ = x.shape
    tm = SEQ_TILE
    return pl.pallas_call(
        _inproj_kernel,
        grid=(B, L // tm),
        in_specs=[
            pl.BlockSpec((1, tm, D_MODEL), lambda b, i: (b, i, 0)),
            _mod_spec(0, per_batch),
            _mod_spec(1, per_batch),
            pl.BlockSpec((1, D_MODEL), lambda b, i: (0, 0)),
            pl.BlockSpec((D_MODEL, D_MODEL), lambda b, i: (0, 0)),
        ],
        out_specs=[
            pl.BlockSpec((tm, D_SSM), lambda b, i: (i, b)),
            pl.BlockSpec((1, tm, D_POOL), lambda b, i: (b, i, 0)),
        ],
        out_shape=[
            jax.ShapeDtypeStruct((L, B * D_SSM), F32),
            jax.ShapeDtypeStruct((B, L, D_POOL), F32),
        ],
        compiler_params=_params(("parallel", "parallel")),
        name="inproj",
    )(x, mods, mods, g.reshape(1, D_MODEL), w_in)


def _s5_kernel(u_ref, are_ref, aim_ref, wb_ref, wcr_ref, wci_ref, h0r_ref, h0i_ref,
               y_ref, sr_ref, si_ref, bur_ref, bui_ref, cr_ref, ci_ref, *, reverse, n_chunks):
    c = pl.program_id(1)
    rows = S5_CHUNK * SUBLANES
    half = SSM_N // 2
    half_in = D_SSM // 2

    @pl.when(c == 0)
    def _():
        cr_ref[...] = h0r_ref[...]
        ci_ref[...] = h0i_ref[...]

    u = u_ref[...].reshape(rows, D_SSM).astype(BF16)
    for j in range(2):
        r = _dot(u[:, j * half_in:(j + 1) * half_in], wb_ref[0, j])
        bur_ref[:, j * half:(j + 1) * half] = r[:, :half]
        bui_ref[:, j * half:(j + 1) * half] = r[:, half:]

    for lc in range(SSM_N // S5_LANES):
        ls = slice(lc * S5_LANES, (lc + 1) * S5_LANES)
        ar = jnp.broadcast_to(are_ref[0, :, ls], (SUBLANES, S5_LANES))
        ai = jnp.broadcast_to(aim_ref[0, :, ls], (SUBLANES, S5_LANES))

        def step(i, carry, ls=ls, ar=ar, ai=ai):
            hr, hi = carry
            t = (S5_CHUNK - 1 - i) if reverse else i
            row = pl.multiple_of(t * SUBLANES, SUBLANES)
            nhr = ar * hr - ai * hi + bur_ref[pl.ds(row, SUBLANES), ls]
            nhi = ar * hi + ai * hr + bui_ref[pl.ds(row, SUBLANES), ls]
            bur_ref[pl.ds(row, SUBLANES), ls] = nhr
            bui_ref[pl.ds(row, SUBLANES), ls] = nhi
            return nhr, nhi

        hr, hi = lax.fori_loop(0, S5_CHUNK, step, (cr_ref[:, ls], ci_ref[:, ls]), unroll=4)
        cr_ref[:, ls] = hr
        ci_ref[:, ls] = hi

    ys = []
    for j in range(2):
        hs = slice(j * half, (j + 1) * half)
        yr = _dot(bur_ref[:, hs].astype(BF16), wcr_ref[0, j])
        yi = _dot(bui_ref[:, hs].astype(BF16), wci_ref[0, j])
        ys.append(yr - yi)
    y_ref[...] = jnp.concatenate(ys, axis=-1).reshape(S5_CHUNK, SUBLANES, D_SSM)

    @pl.when(c == n_chunks - 1)
    def _():
        sr_ref[...] = cr_ref[...]
        si_ref[...] = ci_ref[...]


def _s5_direction(zs, a_re, a_im, wb, wc_re, wc_im, h0r, h0i, d):
    L, B, _ = zs.shape
    n_chunks = L // S5_CHUNK
    reverse = d == 1
    cidx = (lambda g, c: n_chunks - 1 - c) if reverse else (lambda g, c: c)
    rows = S5_CHUNK * SUBLANES
    kern = functools.partial(_s5_kernel, reverse=reverse, n_chunks=n_chunks)
    state = jax.ShapeDtypeStruct((B, SSM_N), F32)
    return pl.pallas_call(
        kern,
        grid=(B // SUBLANES, n_chunks),
        in_specs=[
            pl.BlockSpec((S5_CHUNK, SUBLANES, D_SSM), lambda g, c: (cidx(g, c), g, 0)),
            pl.BlockSpec((1, 1, SSM_N), lambda g, c: (d, 0, 0)),
            pl.BlockSpec((1, 1, SSM_N), lambda g, c: (d, 0, 0)),
            pl.BlockSpec((1, 2, D_SSM // 2, SSM_N), lambda g, c: (d, 0, 0, 0)),
            pl.BlockSpec((1, 2, SSM_N // 2, D_SSM // 2), lambda g, c: (d, 0, 0, 0)),
            pl.BlockSpec((1, 2, SSM_N // 2, D_SSM // 2), lambda g, c: (d, 0, 0, 0)),
            pl.BlockSpec((SUBLANES, SSM_N), lambda g, c: (g, 0)),
            pl.BlockSpec((SUBLANES, SSM_N), lambda g, c: (g, 0)),
        ],
        out_specs=[
            pl.BlockSpec((S5_CHUNK, SUBLANES, D_SSM), lambda g, c: (cidx(g, c), g, 0)),
            pl.BlockSpec((SUBLANES, SSM_N), lambda g, c: (g, 0)),
            pl.BlockSpec((SUBLANES, SSM_N), lambda g, c: (g, 0)),
        ],
        out_shape=[jax.ShapeDtypeStruct((L, B, D_SSM), F32), state, state],
        scratch_shapes=[
            pltpu.VMEM((rows, SSM_N), F32),
            pltpu.VMEM((rows, SSM_N), F32),
            pltpu.VMEM((SUBLANES, SSM_N), F32),
            pltpu.VMEM((SUBLANES, SSM_N), F32),
        ],
        compiler_params=_params(("parallel", "arbitrary")),
        name="s5_bwd" if reverse else "s5_fwd",
    )(zs, a_re, a_im, wb, wc_re, wc_im, h0r, h0i)


def _pool_tables(grid):
    seg = GRID_W if grid else SEQ_TILE
    t = np.arange(SEQ_TILE)
    pos = t % seg
    base = t - pos
    mats, invs = [], []
    for w in POOL_WINDOWS:
        lo = np.clip(pos - w // 2, 0, seg)
        hi = np.clip(pos + (w - w // 2), 0, seg)
        s = t[None, :]
        m = (s >= (base + lo)[:, None]) & (s < (base + hi)[:, None])
        mats.append(m.astype(np.float32))
        invs.append(np.broadcast_to((1.0 / (hi - lo).astype(np.float32))[:, None], (SEQ_TILE, LANES)))
    return jnp.asarray(np.stack(mats), dtype=BF16), jnp.asarray(np.stack(invs), dtype=F32)


def _mix_kernel(x_ref, u_ref, zp_ref, yf_ref, yb_ref, g1_ref, sh2_ref, sc2_ref, d_ref, wglu_ref,
                pm_ref, pinv_ref, wpool_ref, pscale_ref, wout_ref, n2_ref, x1_ref, h2t_ref):
    x = x_ref[0]
    y = d_ref[...] * u_ref[...] + yf_ref[...] + yb_ref[...]
    g = _gelu(y)
    ys = g * jax.nn.sigmoid(_dot(g.astype(BF16), wglu_ref[...]))

    zp = zp_ref[0]
    parts = [ys]
    for gi in range(POOL_G):
        zg = zp[:, gi * POOL_C:(gi + 1) * POOL_C]
        hi, lo = _split_bf16(zg)
        pm = pm_ref[gi]
        win = _dot(pm, hi) + _dot(pm, lo)
        pooled = win * pinv_ref[gi] - zg
        og = _dot(pooled.astype(BF16), wpool_ref[gi])
        parts.append(og * pscale_ref[:, gi * POOL_C:(gi + 1) * POOL_C])
    mix = _dot(jnp.concatenate(parts, axis=-1).astype(BF16), wout_ref[...])

    x1 = x + g1_ref[0, 0] * mix
    x1_ref[...] = x1
    ms = jnp.mean(x1 * x1, axis=-1, keepdims=True)
    h2 = x1 * lax.rsqrt(ms + EPS) * n2_ref[...]
    h2 = h2 * (1.0 + sc2_ref[0, 0]) + sh2_ref[0, 0]
    h2t_ref[...] = h2.T.astype(BF16)


def _mix(x, zs2d, zp, yf2d, yb2d, mods, ssm_d, w_glu, w_pool, pool_scale, w_out, norm2_g, per_batch, grid):
    B, L, _ = x.shape
    tm = SEQ_TILE
    nt = L // tm
    pm, pinv = _pool_tables(grid)
    tm_spec = pl.BlockSpec((tm, D_SSM), lambda b, i: (i, b))
    const2 = lambda b, i: (0, 0)
    const3 = lambda b, i: (0, 0, 0)
    return pl.pallas_call(
        _mix_kernel,
        grid=(B, nt),
        in_specs=[
            pl.BlockSpec((1, tm, D_MODEL), lambda b, i: (b, i, 0)),
            tm_spec,
            pl.BlockSpec((1, tm, D_POOL), lambda b, i: (b, i, 0)),
            tm_spec,
            tm_spec,
            _mod_spec(2, per_batch),
            _mod_spec(3, per_batch),
            _mod_spec(4, per_batch),
            pl.BlockSpec((1, D_SSM), const2),
            pl.BlockSpec((D_SSM, D_SSM), const2),
            pl.BlockSpec((POOL_G, tm, tm), const3),
            pl.BlockSpec((POOL_G, tm, LANES), const3),
            pl.BlockSpec((POOL_G, POOL_C, POOL_C), const3),
            pl.BlockSpec((1, D_POOL), const2),
            pl.BlockSpec((D_MODEL, D_MODEL), const2),
            pl.BlockSpec((1, D_MODEL), const2),
        ],
        out_specs=[
            pl.BlockSpec((tm, D_MODEL), lambda b, i: (b * nt + i, 0)),
            pl.BlockSpec((D_MODEL, tm), lambda b, i: (0, b * nt + i)),
        ],
        out_shape=[
            jax.ShapeDtypeStruct((B * L, D_MODEL), F32),
            jax.ShapeDtypeStruct((D_MODEL, B * L), BF16),
        ],
        compiler_params=_params(("parallel", "parallel")),
        name="mix",
    )(x, zs2d, zp, yf2d, yb2d, mods, mods, mods, ssm_d.reshape(1, D_SSM), w_glu.astype(BF16),
      pm, pinv, w_pool.astype(BF16), pool_scale.reshape(1, D_POOL), w_out.astype(BF16),
      norm2_g.reshape(1, D_MODEL))


def _sort_pairs(n):
    pairs = []

    def merge(lo, hi, r):
        step = r * 2
        if step < hi - lo:
            merge(lo, hi, step)
            merge(lo + r, hi, step)
            for i in range(lo + r, hi - r, step):
                pairs.append((i, i + r))
        else:
            pairs.append((lo, lo + r))

    def sort(lo, hi):
        if hi - lo >= 1:
            mid = lo + (hi - lo) // 2
            sort(lo, mid)
            sort(mid + 1, hi)
            merge(lo, hi, 1)

    sort(0, n - 1)
    return pairs


_SORT16 = _sort_pairs(PEER_TOPK)


def _vmax(a, b):
    if a is None:
        return b
    if b is None:
        return a
    return jnp.maximum(a, b)


def _vmin(a, b):
    if a is None or b is None:
        return None
    return jnp.minimum(a, b)


def _sort16_desc(w):
    w = list(w)
    for i, j in _SORT16:
        w[i], w[j] = _vmax(w[i], w[j]), _vmin(w[i], w[j])
    return w


def _merge_top16(a, b):
    k = PEER_TOPK
    w = [_vmax(a[i], b[k - 1 - i]) for i in range(k)]
    d = k // 2
    while d >= 1:
        for i in range(k):
            if i & d == 0:
                w[i], w[i + d] = _vmax(w[i], w[i + d]), _vmin(w[i], w[i + d])
        d //= 2
    return w


def _top16_of_128(load):
    cur = None
    for grp in range(PEER_NKEYS // PEER_TOPK):
        s = _sort16_desc([load(grp * PEER_TOPK + i) for i in range(PEER_TOPK)])
        cur = s if cur is None else _merge_top16(cur, s)
    return cur


def _top16_pair_sums(sv0, sv1):
    k = PEER_TOPK
    pad = lambda row: row + [None] * (k - len(row))
    cur = [sv0[0] + sv1[j] for j in range(k)]
    for i in range(1, k // 2):
        cur = _merge_top16(cur, pad([sv0[i] + sv1[j] for j in range(k // (i + 1))]))
    return _merge_top16(cur, pad([sv0[i] + sv1[0] for i in range(k // 2, k)]))


def _peer_prep_kernel(ht_ref, wq_ref, wka_ref, kb_ref, a_ref, c_ref, b_ref, s1_ref, q_ref, sa_ref, st_ref):
    tm = ht_ref.shape[1]
    half = PEER_HEADS * PEER_DHALF
    q_ref[...] = _dot(wq_ref[...], ht_ref[...]).astype(BF16)
    for side in range(2):
        s = _dot(wka_ref[side], q_ref[side * half:(side + 1) * half, :])
        sa_ref[side] = s.reshape(PEER_NKEYS, PEER_HEADS, tm)

    for lc in range(tm // LANES):
        ls = slice(lc * LANES, (lc + 1) * LANES)
        sv0 = _top16_of_128(lambda n: sa_ref[0, n, :, ls])
        sv1 = _top16_of_128(lambda n: sa_ref[1, n, :, ls])
        top = _top16_pair_sums(sv0, sv1)
        tau = top[PEER_TOPK - 1]
        best = sv0[0] + sv1[0]
        z = jnp.exp(top[0] - best)
        for k in range(1, PEER_TOPK):
            z = z + jnp.exp(top[k] - best)
        slack = (jnp.abs(tau) + jnp.abs(sv0[0]) + jnp.abs(sv0[PEER_TOPK - 1])
                 + jnp.abs(sv1[0]) + jnp.abs(sv1[PEER_TOPK - 1])) * (2.0 ** -21) + 1e-30
        st_ref[0, :, ls] = sv0[0]
        st_ref[1, :, ls] = sv1[0]
        st_ref[2, :, ls] = tau - slack
        st_ref[3, :, ls] = 1.0 / z

    for h in range(PEER_HEADS):
        hs = slice(h, h + 1)
        s0 = _dot(kb_ref[h], q_ref[h * PEER_DHALF:(h + 1) * PEER_DHALF, :])
        s1 = _dot(kb_ref[PEER_HEADS + h], q_ref[half + h * PEER_DHALF:half + (h + 1) * PEER_DHALF, :])
        a = jnp.exp(s0 - st_ref[0, hs, :]) * st_ref[3, hs, :]
        c = st_ref[2, hs, :] - s0
        b = jnp.exp(s1 - st_ref[1, hs, :])
        for lc in range(tm // LANES):
            ls = slice(lc * LANES, (lc + 1) * LANES)
            a_ref[lc, h] = a[:, ls]
            c_ref[lc, h] = c[:, ls]
            b_ref[lc, h] = b[:, ls]
            s1_ref[lc, h] = s1[:, ls]


def _peer_tables(peer_wq, sub_keys):
    wq = peer_wq.reshape(D_MODEL, PEER_HEADS, 2, PEER_DHALF)
    wq_t = jnp.transpose(wq, (2, 1, 3, 0)).reshape(2 * PEER_HEADS * PEER_DHALF, D_MODEL).astype(BF16)
    eye = jnp.eye(PEER_HEADS, dtype=F32)
    wka = jnp.einsum("hsnk,hg->snhgk", sub_keys, eye)
    wka = wka.reshape(2, PEER_NKEYS * PEER_HEADS, PEER_HEADS * PEER_DHALF).astype(BF16)
    kb = jnp.transpose(sub_keys, (1, 0, 2, 3)).reshape(2 * PEER_HEADS, PEER_NKEYS, PEER_DHALF).astype(BF16)
    return wq_t, wka, kb


def _peer_prep(h2t, wq_t, wka, kb):
    T = h2t.shape[1]
    tm = PREP_TILE
    half = PEER_HEADS * PEER_DHALF
    out = jax.ShapeDtypeStruct((T // LANES, PEER_HEADS, PEER_NKEYS, LANES), F32)
    ospec = pl.BlockSpec((tm // LANES, PEER_HEADS, PEER_NKEYS, LANES), lambda i: (i, 0, 0, 0))
    return pl.pallas_call(
        _peer_prep_kernel,
        grid=(T // tm,),
        in_specs=[
            pl.BlockSpec((D_MODEL, tm), lambda i: (0, i)),
            pl.BlockSpec((2 * half, D_MODEL), lambda i: (0, 0)),
            pl.BlockSpec((2, PEER_NKEYS * PEER_HEADS, half), lambda i: (0, 0, 0)),
            pl.BlockSpec((2 * PEER_HEADS, PEER_NKEYS, PEER_DHALF), lambda i: (0, 0, 0)),
        ],
        out_specs=[ospec, ospec, ospec, ospec],
        out_shape=[out, out, out, out],
        scratch_shapes=[
            pltpu.VMEM((2 * half, tm), BF16),
            pltpu.VMEM((2, PEER_NKEYS, PEER_HEADS, tm), F32),
            pltpu.VMEM((4, PEER_HEADS, tm), F32),
        ],
        compiler_params=_params(("parallel",)),
        name="peer_prep",
    )(h2t, wq_t, wka, kb)


def _peer_dense_kernel(ht_ref, u_ref, vt_ref, a_ref, c_ref, b_ref, s1_ref, x1_ref, g2_ref, fg_ref,
                       y_ref, acc_ref, act_ref, w_ref):
    j = pl.program_id(1)
    tm = ht_ref.shape[1]
    te = u_ref.shape[0]
    rows_per_step = te // PEER_NKEYS
    n_chunks = tm // LANES

    @pl.when(j == 0)
    def _():
        acc_ref[...] = jnp.zeros_like(acc_ref)

    act = _dot(u_ref[...], ht_ref[...])
    for tc in range(n_chunks):
        act_ref[tc] = act[:, tc * LANES:(tc + 1) * LANES]

    first_keys = pl.ds(pl.multiple_of(j * rows_per_step, SUBLANES), rows_per_step)

    def chunk_body(tc, carry):
        a_rows = [a_ref[tc, h, first_keys, :] for h in range(PEER_HEADS)]
        c_rows = [c_ref[tc, h, first_keys, :] for h in range(PEER_HEADS)]
        for il in range(rows_per_step):
            gate = jnp.zeros((PEER_NKEYS, LANES), F32)
            for h in range(PEER_HEADS):
                keep = s1_ref[tc, h] >= c_rows[h][il:il + 1, :]
                gate = gate + jnp.where(keep, b_ref[tc, h], 0.0) * a_rows[h][il:il + 1, :]
            rows = slice(il * PEER_NKEYS, (il + 1) * PEER_NKEYS)
            w_ref[tc, rows, :] = (_gelu(act_ref[tc, rows, :]) * gate).astype(BF16)
        return carry

    lax.fori_loop(0, n_chunks, chunk_body, 0)
    w = jnp.concatenate([w_ref[tc] for tc in range(n_chunks)], axis=1)
    acc_ref[...] += _dot(vt_ref[...], w)

    @pl.when(j == pl.num_programs(1) - 1)
    def _():
        out = x1_ref[...] + g2_ref[0, 0] * acc_ref[...].T
        ms = jnp.mean(out * out, axis=-1, keepdims=True)
        y_ref[...] = out * lax.rsqrt(ms + EPS) * fg_ref[...]


def _peer_dense(h2t, u_bf, vt_bf, a, c, b, s1, x1, mods, final_g, tokens_per_row):
    T = h2t.shape[1]
    tm, te = PEER_TM, PEER_TE
    if tokens_per_row is None:
        g2_map = lambda i, j: (CTX_ROW, 5, 0, 0)
    else:
        g2_map = lambda i, j: ((i * tm) // tokens_per_row, 5, 0, 0)
    stat = pl.BlockSpec((tm // LANES, PEER_HEADS, PEER_NKEYS, LANES), lambda i, j: (i, 0, 0, 0))
    return pl.pallas_call(
        _peer_dense_kernel,
        grid=(T // tm, PEER_N // te),
        in_specs=[
            pl.BlockSpec((D_MODEL, tm), lambda i, j: (0, i)),
            pl.BlockSpec((te, D_MODEL), lambda i, j: (j, 0)),
            pl.BlockSpec((D_MODEL, te), lambda i, j: (0, j)),
            stat, stat, stat, stat,
            pl.BlockSpec((tm, D_MODEL), lambda i, j: (i, 0)),
            pl.BlockSpec((1, 1, 1, D_MODEL), g2_map),
            pl.BlockSpec((1, D_MODEL), lambda i, j: (0, 0)),
        ],
        out_specs=pl.BlockSpec((tm, D_MODEL), lambda i, j: (i, 0)),
        out_shape=jax.ShapeDtypeStruct((T, D_MODEL), F32),
        scratch_shapes=[
            pltpu.VMEM((D_MODEL, tm), F32),
            pltpu.VMEM((tm // LANES, te, LANES), F32),
            pltpu.VMEM((tm // LANES, te, LANES), BF16),
        ],
        compiler_params=_params(("parallel", "arbitrary")),
        name="peer_dense",
    )(h2t, u_bf, vt_bf, a, c, b, s1, x1, mods, final_g.reshape(1, D_MODEL))


def _trunk(x, mods, h0r, h0i, s5w, p, tables, final_g, per_batch, grid):
    B, L, _ = x.shape
    a_re, a_im, wb, wc_re, wc_im = s5w
    zs2d, zp = _inproj(x, mods, p["norm1_g"], p["w_in"], per_batch)
    zs = zs2d.reshape(L, B, D_SSM)
    yf, sfr, sfi = _s5_direction(zs, a_re, a_im, wb, wc_re, wc_im, h0r[:, 0], h0i[:, 0], 0)
    yb, sbr, sbi = _s5_direction(zs, a_re, a_im, wb, wc_re, wc_im, h0r[:, 1], h0i[:, 1], 1)
    x1, h2t = _mix(x, zs2d, zp, yf.reshape(L, B * D_SSM), yb.reshape(L, B * D_SSM), mods, p["ssm_d"],
                   p["w_glu"], p["w_pool"], p["pool_scale"], p["w_out"], p["norm2_g"], per_batch, grid)
    wq_t, wka, kb, u_bf, vt_bf = tables
    a, c, b, s1 = _peer_prep(h2t, wq_t, wka, kb)
    y = _peer_dense(h2t, u_bf, vt_bf, a, c, b, s1, x1, mods, final_g, L if per_batch else None)
    new_re = jnp.stack([sfr, sbr], axis=1).reshape(B, 2, SSM_G, SSM_P)
    new_im = jnp.stack([sfi, sbi], axis=1).reshape(B, 2, SSM_G, SSM_P)
    return y.reshape(B, L, D_MODEL), new_re, new_im


def kernel(x_prompt, x_sample, state_ssm_re, state_ssm_im, c, c_ctx, norm1_g, w_mod, b_mod, w_in,
           ssm_lambda_re, ssm_lambda_im, ssm_log_dt, ssm_b_re, ssm_b_im, ssm_c_re, ssm_c_im, ssm_d,
           w_glu, w_pool, pool_scale, w_out, norm2_g, peer_wq, peer_subkeys, peer_u, peer_v, final_g):
    depth = w_mod.shape[0]
    assert depth == 1, "single trunk layer"
    l = 0
    n_dec = c.shape[0]
    cond = jnp.zeros((COND_ROWS, D_MODEL), F32).at[:n_dec].set(c).at[CTX_ROW].set(c_ctx)
    mods = _mod_vectors(cond, w_mod[l], b_mod[l]).reshape(COND_ROWS, N_MOD, 1, D_MODEL)

    ar, ai, bbr, bbi = _discretise(ssm_lambda_re[l], ssm_lambda_im[l], ssm_log_dt[l], ssm_b_re[l], ssm_b_im[l])
    s5w = _s5_weights(ar, ai, bbr, bbi, ssm_c_re[l], ssm_c_im[l])

    wq_t, wka, kb = _peer_tables(peer_wq[l], peer_subkeys[l])
    tables = (wq_t, wka, kb, peer_u[l].astype(BF16), peer_v[l].T.astype(BF16))

    p = {"norm1_g": norm1_g[l], "w_in": w_in[l].astype(BF16), "ssm_d": ssm_d[l], "w_glu": w_glu[l],
         "w_pool": w_pool[l], "pool_scale": pool_scale[l], "w_out": w_out[l], "norm2_g": norm2_g[l]}

    bp = x_prompt.shape[0]
    zeros = jnp.zeros((bp, 2, SSM_N), F32)
    y_prompt, new_re, new_im = _trunk(x_prompt, mods, zeros, zeros, s5w, p, tables, final_g, False, False)
    h0r = state_ssm_re[:, l].reshape(n_dec, 2, SSM_N)
    h0i = state_ssm_im[:, l].reshape(n_dec, 2, SSM_N)
    y_sample, _, _ = _trunk(x_sample, mods, h0r, h0i, s5w, p, tables, final_g, True, True)
    return (y_prompt, y_sample, new_re[:, None], new_im[:, None])
```

```python
import functools
import math

import numpy as np
import jax
import jax.numpy as jnp
from jax import lax
from jax.experimental import pallas as pl
from jax.experimental.pallas import tpu as pltpu

F32 = jnp.float32
BF16 = jnp.bfloat16

D_MODEL = 1024
D_SSM = 512
D_POOL = 512
SSM_H = 16
SSM_G = 32
SSM_P = 64
SSM_N = SSM_G * SSM_P
POOL_WINDOWS = (2, 4, 8, 16)
POOL_G = 4
POOL_C = 128
GRID_W = 64
PEER_HEADS = 8
PEER_NKEYS = 128
PEER_N = PEER_NKEYS * PEER_NKEYS
PEER_DHALF = 128
PEER_TOPK = 16
N_MOD = 6
EPS = 1e-6

SUBLANES = 8
LANES = 128
VMEM_LIMIT = 56 * 1024 * 1024

COND_ROWS = 16
CTX_ROW = 8
SEQ_TILE = 256
S5_CHUNK = 64
S5_LANES = 512
PREP_TILE = 256
PEER_TM = 512
PEER_TE = 1024
PEER_SB = 256
GATE_ROWS = 64


def _gelu(x):
    return 0.5 * x * (1.0 + jnp.tanh(0.7978845608028654 * (x + 0.044715 * (x * x * x))))


def _split_bf16(x):
    hi = x.astype(BF16)
    lo = (x - hi.astype(F32)).astype(BF16)
    return hi, lo


def _dot(a, b):
    return jnp.dot(a, b, preferred_element_type=F32)


def _params(sem):
    return pltpu.CompilerParams(dimension_semantics=sem, vmem_limit_bytes=VMEM_LIMIT)


def _mod_kernel(cond_ref, whi_ref, wlo_ref, b_ref, o_ref):
    c = cond_ref[...]
    s = c * jax.nn.sigmoid(c)
    shi, slo = _split_bf16(s)
    whi = whi_ref[...]
    o_ref[...] = _dot(shi, whi) + _dot(slo, whi) + _dot(shi, wlo_ref[...]) + b_ref[...]


def _mod_vectors(cond, w_mod, b_mod):
    whi, wlo = _split_bf16(w_mod)
    n = w_mod.shape[1]
    bn = D_MODEL
    return pl.pallas_call(
        _mod_kernel,
        grid=(n // bn,),
        in_specs=[
            pl.BlockSpec((COND_ROWS, D_MODEL), lambda k: (0, 0)),
            pl.BlockSpec((D_MODEL, bn), lambda k: (0, k)),
            pl.BlockSpec((D_MODEL, bn), lambda k: (0, k)),
            pl.BlockSpec((1, bn), lambda k: (0, k)),
        ],
        out_specs=pl.BlockSpec((COND_ROWS, bn), lambda k: (0, k)),
        out_shape=jax.ShapeDtypeStruct((COND_ROWS, n), F32),
        compiler_params=_params(("arbitrary",)),
        name="mod",
    )(cond, whi, wlo, b_mod.reshape(1, n))


def _disc_kernel(lr_ref, li_ref, ldt_ref, br_ref, bi_ref, ar_ref, ai_ref, bbr_ref, bbi_ref):
    lr = lr_ref[...]
    li = li_ref[...]
    dt = jnp.exp(ldt_ref[...])
    mag = jnp.exp(lr * dt)
    ar = mag * jnp.cos(li * dt)
    ai = mag * jnp.sin(li * dt)
    den = lr * lr + li * li
    nr = ar - 1.0
    ni = ai
    fr = (nr * lr + ni * li) / den
    fi = (ni * lr - nr * li) / den
    ar_ref[...] = ar
    ai_ref[...] = ai
    br = br_ref[...]
    bi = bi_ref[...]
    frb = fr[:, None, :]
    fib = fi[:, None, :]
    bbr_ref[...] = frb * br - fib * bi
    bbi_ref[...] = frb * bi + fib * br


def _discretise(lam_re, lam_im, log_dt, b_re, b_im):
    rows = 2 * SSM_G
    lr = lam_re.reshape(rows, SSM_P)
    li = lam_im.reshape(rows, SSM_P)
    ldt = jnp.broadcast_to(log_dt.reshape(rows, 1), (rows, SSM_P))
    br = jnp.swapaxes(b_re, -1, -2).reshape(rows, SSM_H, SSM_P)
    bi = jnp.swapaxes(b_im, -1, -2).reshape(rows, SSM_H, SSM_P)
    small = jax.ShapeDtypeStruct((rows, SSM_P), F32)
    big = jax.ShapeDtypeStruct((rows, SSM_H, SSM_P), F32)
    return pl.pallas_call(_disc_kernel, out_shape=(small, small, big, big), name="disc")(lr, li, ldt, br, bi)


def _s5_weights(ar, ai, bbr, bbi, c_re, c_im):
    half_g = SSM_G // 2
    eye = jnp.eye(half_g, dtype=F32)
    a_re = ar.reshape(2, 1, SSM_N)
    a_im = ai.reshape(2, 1, SSM_N)

    def bmat(t):
        t = t.reshape(2, 2, half_g, SSM_H, SSM_P)
        m = jnp.einsum("djghp,gk->djghkp", t, eye)
        return m.reshape(2, 2, half_g * SSM_H, half_g * SSM_P)

    def cmat(t):
        t = t.reshape(2, 2, half_g, SSM_H, SSM_P)
        m = jnp.einsum("djghp,gk->djgpkh", t, eye)
        return m.reshape(2, 2, half_g * SSM_P, half_g * SSM_H)

    wb = jnp.concatenate([bmat(bbr), bmat(bbi)], axis=-1).astype(BF16)
    wc_re = cmat(c_re).astype(BF16)
    wc_im = cmat(c_im).astype(BF16)
    return a_re, a_im, wb, wc_re, wc_im


def _inproj_kernel(x_ref, sh_ref, sc_ref, g_ref, w_ref, zs_ref, zp_ref):
    x = x_ref[0]
    ms = jnp.mean(x * x, axis=-1, keepdims=True)
    y = x * lax.rsqrt(ms + EPS) * g_ref[...]
    h = y * (1.0 + sc_ref[0, 0]) + sh_ref[0, 0]
    z = _dot(h.astype(BF16), w_ref[...])
    zs_ref[...] = z[:, :D_SSM]
    zp_ref[0] = z[:, D_SSM:]


def _mod_spec(k, per_batch, tokens_per_row=None):
    if per_batch:
        return pl.BlockSpec((1, 1, 1, D_MODEL), lambda b, i: (b, k, 0, 0))
    return pl.BlockSpec((1, 1, 1, D_MODEL), lambda b, i: (CTX_ROW, k, 0, 0))


def _inproj(x, mods, g, w_in, per_batch):
    B, L, _ = x.shape
    tm = SEQ_TILE
    return pl.pallas_call(
        _inproj_kernel,
        grid=(B, L // tm),
        in_specs=[
            pl.BlockSpec((1, tm, D_MODEL), lambda b, i: (b, i, 0)),
            _mod_spec(0, per_batch),
            _mod_spec(1, per_batch),
            pl.BlockSpec((1, D_MODEL), lambda b, i: (0, 0)),
            pl.BlockSpec((D_MODEL, D_MODEL), lambda b, i: (0, 0)),
        ],
        out_specs=[
            pl.BlockSpec((tm, D_SSM), lambda b, i: (i, b)),
            pl.BlockSpec((1, tm, D_POOL), lambda b, i: (b, i, 0)),
        ],
        out_shape=[
            jax.ShapeDtypeStruct((L, B * D_SSM), F32),
            jax.ShapeDtypeStruct((B, L, D_POOL), F32),
        ],
        compiler_params=_params(("parallel", "parallel")),
        name="inproj",
    )(x, mods, mods, g.reshape(1, D_MODEL), w_in)


def _s5_kernel(u_ref, are_ref, aim_ref, wb_ref, wcr_ref, wci_ref, h0r_ref, h0i_ref,
               y_ref, sr_ref, si_ref, bur_ref, bui_ref, cr_ref, ci_ref, *, reverse, n_chunks):
    c = pl.program_id(1)
    rows = S5_CHUNK * SUBLANES
    half = SSM_N // 2
    half_in = D_SSM // 2

    @pl.when(c == 0)
    def _():
        cr_ref[...] = h0r_ref[...]
        ci_ref[...] = h0i_ref[...]

    u = u_ref[...].reshape(rows, D_SSM).astype(BF16)
    for j in range(2):
        r = _dot(u[:, j * half_in:(j + 1) * half_in], wb_ref[0, j])
        bur_ref[:, j * half:(j + 1) * half] = r[:, :half]
        bui_ref[:, j * half:(j + 1) * half] = r[:, half:]

    for lc in range(SSM_N // S5_LANES):
        ls = slice(lc * S5_LANES, (lc + 1) * S5_LANES)
        ar = jnp.broadcast_to(are_ref[0, :, ls], (SUBLANES, S5_LANES))
        ai = jnp.broadcast_to(aim_ref[0, :, ls], (SUBLANES, S5_LANES))

        def step(i, carry, ls=ls, ar=ar, ai=ai):
            hr, hi = carry
            t = (S5_CHUNK - 1 - i) if reverse else i
            row = pl.multiple_of(t * SUBLANES, SUBLANES)
            nhr = ar * hr - ai * hi + bur_ref[pl.ds(row, SUBLANES), ls]
            nhi = ar * hi + ai * hr + bui_ref[pl.ds(row, SUBLANES), ls]
            bur_ref[pl.ds(row, SUBLANES), ls] = nhr
            bui_ref[pl.ds(row, SUBLANES), ls] = nhi
            return nhr, nhi

        hr, hi = lax.fori_loop(0, S5_CHUNK, step, (cr_ref[:, ls], ci_ref[:, ls]), unroll=4)
        cr_ref[:, ls] = hr
        ci_ref[:, ls] = hi

    ys = []
    for j in range(2):
        hs = slice(j * half, (j + 1) * half)
        yr = _dot(bur_ref[:, hs].astype(BF16), wcr_ref[0, j])
        yi = _dot(bui_ref[:, hs].astype(BF16), wci_ref[0, j])
        ys.append(yr - yi)
    y_ref[...] = jnp.concatenate(ys, axis=-1).reshape(S5_CHUNK, SUBLANES, D_SSM)

    @pl.when(c == n_chunks - 1)
    def _():
        sr_ref[...] = cr_ref[...]
        si_ref[...] = ci_ref[...]


def _s5_direction(zs, a_re, a_im, wb, wc_re, wc_im, h0r, h0i, d):
    L, B, _ = zs.shape
    n_chunks = L // S5_CHUNK
    reverse = d == 1
    cidx = (lambda g, c: n_chunks - 1 - c) if reverse else (lambda g, c: c)
    rows = S5_CHUNK * SUBLANES
    kern = functools.partial(_s5_kernel, reverse=reverse, n_chunks=n_chunks)
    state = jax.ShapeDtypeStruct((B, SSM_N), F32)
    return pl.pallas_call(
        kern,
        grid=(B // SUBLANES, n_chunks),
        in_specs=[
            pl.BlockSpec((S5_CHUNK, SUBLANES, D_SSM), lambda g, c: (cidx(g, c), g, 0)),
            pl.BlockSpec((1, 1, SSM_N), lambda g, c: (d, 0, 0)),
            pl.BlockSpec((1, 1, SSM_N), lambda g, c: (d, 0, 0)),
            pl.BlockSpec((1, 2, D_SSM // 2, SSM_N), lambda g, c: (d, 0, 0, 0)),
            pl.BlockSpec((1, 2, SSM_N // 2, D_SSM // 2), lambda g, c: (d, 0, 0, 0)),
            pl.BlockSpec((1, 2, SSM_N // 2, D_SSM // 2), lambda g, c: (d, 0, 0, 0)),
            pl.BlockSpec((SUBLANES, SSM_N), lambda g, c: (g, 0)),
            pl.BlockSpec((SUBLANES, SSM_N), lambda g, c: (g, 0)),
        ],
        out_specs=[
            pl.BlockSpec((S5_CHUNK, SUBLANES, D_SSM), lambda g, c: (cidx(g, c), g, 0)),
            pl.BlockSpec((SUBLANES, SSM_N), lambda g, c: (g, 0)),
            pl.BlockSpec((SUBLANES, SSM_N), lambda g, c: (g, 0)),
        ],
        out_shape=[jax.ShapeDtypeStruct((L, B, D_SSM), F32), state, state],
        scratch_shapes=[
            pltpu.VMEM((rows, SSM_N), F32),
            pltpu.VMEM((rows, SSM_N), F32),
            pltpu.VMEM((SUBLANES, SSM_N), F32),
            pltpu.VMEM((SUBLANES, SSM_N), F32),
        ],
        compiler_params=_params(("parallel", "arbitrary")),
        name="s5_bwd" if reverse else "s5_fwd",
    )(zs, a_re, a_im, wb, wc_re, wc_im, h0r, h0i)


def _pool_tables(grid):
    seg = GRID_W if grid else SEQ_TILE
    t = np.arange(SEQ_TILE)
    pos = t % seg
    base = t - pos
    mats, invs = [], []
    for w in POOL_WINDOWS:
        lo = np.clip(pos - w // 2, 0, seg)
        hi = np.clip(pos + (w - w // 2), 0, seg)
        s = t[None, :]
        m = (s >= (base + lo)[:, None]) & (s < (base + hi)[:, None])
        mats.append(m.astype(np.float32))
        invs.append(np.broadcast_to((1.0 / (hi - lo).astype(np.float32))[:, None], (SEQ_TILE, LANES)))
    return jnp.asarray(np.stack(mats), dtype=BF16), jnp.asarray(np.stack(invs), dtype=F32)


def _mix_kernel(x_ref, u_ref, zp_ref, yf_ref, yb_ref, g1_ref, sh2_ref, sc2_ref, d_ref, wglu_ref,
                pm_ref, pinv_ref, wpool_ref, pscale_ref, wout_ref, n2_ref, x1_ref, h2t_ref):
    x = x_ref[0]
    y = d_ref[...] * u_ref[...] + yf_ref[...] + yb_ref[...]
    g = _gelu(y)
    ys = g * jax.nn.sigmoid(_dot(g.astype(BF16), wglu_ref[...]))

    zp = zp_ref[0]
    parts = [ys]
    for gi in range(POOL_G):
        zg = zp[:, gi * POOL_C:(gi + 1) * POOL_C]
        hi, lo = _split_bf16(zg)
        pm = pm_ref[gi]
        win = _dot(pm, hi) + _dot(pm, lo)
        pooled = win * pinv_ref[gi] - zg
        og = _dot(pooled.astype(BF16), wpool_ref[gi])
        parts.append(og * pscale_ref[:, gi * POOL_C:(gi + 1) * POOL_C])
    mix = _dot(jnp.concatenate(parts, axis=-1).astype(BF16), wout_ref[...])

    x1 = x + g1_ref[0, 0] * mix
    x1_ref[...] = x1
    ms = jnp.mean(x1 * x1, axis=-1, keepdims=True)
    h2 = x1 * lax.rsqrt(ms + EPS) * n2_ref[...]
    h2 = h2 * (1.0 + sc2_ref[0, 0]) + sh2_ref[0, 0]
    h2t_ref[...] = h2.T.astype(BF16)


def _mix(x, zs2d, zp, yf2d, yb2d, mods, ssm_d, w_glu, w_pool, pool_scale, w_out, norm2_g, per_batch, grid):
    B, L, _ = x.shape
    tm = SEQ_TILE
    nt = L // tm
    pm, pinv = _pool_tables(grid)
    tm_spec = pl.BlockSpec((tm, D_SSM), lambda b, i: (i, b))
    const2 = lambda b, i: (0, 0)
    const3 = lambda b, i: (0, 0, 0)
    return pl.pallas_call(
        _mix_kernel,
        grid=(B, nt),
        in_specs=[
            pl.BlockSpec((1, tm, D_MODEL), lambda b, i: (b, i, 0)),
            tm_spec,
            pl.BlockSpec((1, tm, D_POOL), lambda b, i: (b, i, 0)),
            tm_spec,
            tm_spec,
            _mod_spec(2, per_batch),
            _mod_spec(3, per_batch),
            _mod_spec(4, per_batch),
            pl.BlockSpec((1, D_SSM), const2),
            pl.BlockSpec((D_SSM, D_SSM), const2),
            pl.BlockSpec((POOL_G, tm, tm), const3),
            pl.BlockSpec((POOL_G, tm, LANES), const3),
            pl.BlockSpec((POOL_G, POOL_C, POOL_C), const3),
            pl.BlockSpec((1, D_POOL), const2),
            pl.BlockSpec((D_MODEL, D_MODEL), const2),
            pl.BlockSpec((1, D_MODEL), const2),
        ],
        out_specs=[
            pl.BlockSpec((tm, D_MODEL), lambda b, i: (b * nt + i, 0)),
            pl.BlockSpec((D_MODEL, tm), lambda b, i: (0, b * nt + i)),
        ],
        out_shape=[
            jax.ShapeDtypeStruct((B * L, D_MODEL), F32),
            jax.ShapeDtypeStruct((D_MODEL, B * L), BF16),
        ],
        compiler_params=_params(("parallel", "parallel")),
        name="mix",
    )(x, zs2d, zp, yf2d, yb2d, mods, mods, mods, ssm_d.reshape(1, D_SSM), w_glu.astype(BF16),
      pm, pinv, w_pool.astype(BF16), pool_scale.reshape(1, D_POOL), w_out.astype(BF16),
      norm2_g.reshape(1, D_MODEL))


def _sort_pairs(n):
    pairs = []

    def merge(lo, hi, r):
        step = r * 2
        if step < hi - lo:
            merge(lo, hi, step)
            merge(lo + r, hi, step)
            for i in range(lo + r, hi - r, step):
                pairs.append((i, i + r))
        else:
            pairs.append((lo, lo + r))

    def sort(lo, hi):
        if hi - lo >= 1:
            mid = lo + (hi - lo) // 2
            sort(lo, mid)
            sort(mid + 1, hi)
            merge(lo, hi, 1)

    sort(0, n - 1)
    return pairs


_SORT16 = _sort_pairs(PEER_TOPK)


def _vmax(a, b):
    if a is None:
        return b
    if b is None:
        return a
    return jnp.maximum(a, b)


def _vmin(a, b):
    if a is None or b is None:
        return None
    return jnp.minimum(a, b)


def _sort16_desc(w):
    w = list(w)
    for i, j in _SORT16:
        w[i], w[j] = _vmax(w[i], w[j]), _vmin(w[i], w[j])
    return w


def _merge_top16(a, b):
    k = PEER_TOPK
    w = [_vmax(a[i], b[k - 1 - i]) for i in range(k)]
    d = k // 2
    while d >= 1:
        for i in range(k):
            if i & d == 0:
                w[i], w[i + d] = _vmax(w[i], w[i + d]), _vmin(w[i], w[i + d])
        d //= 2
    return w


def _top16_of_128(load):
    cur = None
    for grp in range(PEER_NKEYS // PEER_TOPK):
        s = _sort16_desc([load(grp * PEER_TOPK + i) for i in range(PEER_TOPK)])
        cur = s if cur is None else _merge_top16(cur, s)
    return cur


def _top16_pair_sums(sv0, sv1):
    k = PEER_TOPK
    pad = lambda row: row + [None] * (k - len(row))
    cur = [sv0[0] + sv1[j] for j in range(k)]
    for i in range(1, k // 2):
        cur = _merge_top16(cur, pad([sv0[i] + sv1[j] for j in range(k // (i + 1))]))
    return _merge_top16(cur, pad([sv0[i] + sv1[0] for i in range(k // 2, k)]))


def _peer_prep_kernel(ht_ref, wq_ref, wka_ref, kb_ref, a_ref, c_ref, b_ref, s1_ref, q_ref, sa_ref, st_ref):
    tm = ht_ref.shape[1]
    half = PEER_HEADS * PEER_DHALF
    q_ref[...] = _dot(wq_ref[...], ht_ref[...]).astype(BF16)
    for side in range(2):
        s = _dot(wka_ref[side], q_ref[side * half:(side + 1) * half, :])
        sa_ref[side] = s.reshape(PEER_NKEYS, PEER_HEADS, tm)

    for lc in range(tm // LANES):
        ls = slice(lc * LANES, (lc + 1) * LANES)
        sv0 = _top16_of_128(lambda n: sa_ref[0, n, :, ls])
        sv1 = _top16_of_128(lambda n: sa_ref[1, n, :, ls])
        top = _top16_pair_sums(sv0, sv1)
        tau = top[PEER_TOPK - 1]
        best = sv0[0] + sv1[0]
        z = jnp.exp(top[0] - best)
        for k in range(1, PEER_TOPK):
            z = z + jnp.exp(top[k] - best)
        slack = (jnp.abs(tau) + jnp.abs(sv0[0]) + jnp.abs(sv0[PEER_TOPK - 1])
                 + jnp.abs(sv1[0]) + jnp.abs(sv1[PEER_TOPK - 1])) * (2.0 ** -21) + 1e-30
        st_ref[0, :, ls] = sv0[0]
        st_ref[1, :, ls] = sv1[0]
        st_ref[2, :, ls] = tau - slack
        st_ref[3, :, ls] = 1.0 / z

    for h in range(PEER_HEADS):
        hs = slice(h, h + 1)
        s0 = _dot(kb_ref[h], q_ref[h * PEER_DHALF:(h + 1) * PEER_DHALF, :])
        s1 = _dot(kb_ref[PEER_HEADS + h], q_ref[half + h * PEER_DHALF:half + (h + 1) * PEER_DHALF, :])
        a = jnp.exp(s0 - st_ref[0, hs, :]) * st_ref[3, hs, :]
        c = st_ref[2, hs, :] - s0
        b = jnp.exp(s1 - st_ref[1, hs, :])
        for lc in range(tm // LANES):
            ls = slice(lc * LANES, (lc + 1) * LANES)
            a_ref[lc, h] = a[:, ls]
            c_ref[lc, h] = c[:, ls]
            b_ref[lc, h] = b[:, ls]
            s1_ref[lc, h] = s1[:, ls]


def _peer_tables(peer_wq, sub_keys):
    wq = peer_wq.reshape(D_MODEL, PEER_HEADS, 2, PEER_DHALF)
    wq_t = jnp.transpose(wq, (2, 1, 3, 0)).reshape(2 * PEER_HEADS * PEER_DHALF, D_MODEL).astype(BF16)
    eye = jnp.eye(PEER_HEADS, dtype=F32)
    wka = jnp.einsum("hsnk,hg->snhgk", sub_keys, eye)
    wka = wka.reshape(2, PEER_NKEYS * PEER_HEADS, PEER_HEADS * PEER_DHALF).astype(BF16)
    kb = jnp.transpose(sub_keys, (1, 0, 2, 3)).reshape(2 * PEER_HEADS, PEER_NKEYS, PEER_DHALF).astype(BF16)
    return wq_t, wka, kb


def _peer_prep(h2t, wq_t, wka, kb):
    T = h2t.shape[1]
    tm = PREP_TILE
    half = PEER_HEADS * PEER_DHALF
    out = jax.ShapeDtypeStruct((T // LANES, PEER_HEADS, PEER_NKEYS, LANES), F32)
    ospec = pl.BlockSpec((tm // LANES, PEER_HEADS, PEER_NKEYS, LANES), lambda i: (i, 0, 0, 0))
    return pl.pallas_call(
        _peer_prep_kernel,
        grid=(T // tm,),
        in_specs=[
            pl.BlockSpec((D_MODEL, tm), lambda i: (0, i)),
            pl.BlockSpec((2 * half, D_MODEL), lambda i: (0, 0)),
            pl.BlockSpec((2, PEER_NKEYS * PEER_HEADS, half), lambda i: (0, 0, 0)),
            pl.BlockSpec((2 * PEER_HEADS, PEER_NKEYS, PEER_DHALF), lambda i: (0, 0, 0)),
        ],
        out_specs=[ospec, ospec, ospec, ospec],
        out_shape=[out, out, out, out],
        scratch_shapes=[
            pltpu.VMEM((2 * half, tm), BF16),
            pltpu.VMEM((2, PEER_NKEYS, PEER_HEADS, tm), F32),
            pltpu.VMEM((4, PEER_HEADS, tm), F32),
        ],
        compiler_params=_params(("parallel",)),
        name="peer_prep",
    )(h2t, wq_t, wka, kb)


def _peer_dense_kernel(ht_ref, u_ref, vt_ref, a_ref, c_ref, b_ref, s1_ref, x1_ref, g2_ref, fg_ref,
                       y_ref, acc_ref, act_ref, w_ref):
    j = pl.program_id(1)
    tm = ht_ref.shape[1]
    te = u_ref.shape[0]
    rows_per_step = te // PEER_NKEYS
    n_chunks = tm // LANES

    @pl.when(j == 0)
    def _():
        acc_ref[...] = jnp.zeros_like(acc_ref)

    first_keys = pl.ds(pl.multiple_of(j * rows_per_step, SUBLANES), rows_per_step)
    n_sub = te // PEER_SB
    keys_per_sub = PEER_SB // PEER_NKEYS

    def activations(s):
        act = _dot(u_ref[s * PEER_SB:(s + 1) * PEER_SB, :], ht_ref[...])
        for tc in range(n_chunks):
            act_ref[s % 2, tc] = act[:, tc * LANES:(tc + 1) * LANES]

    def gated(s):
        for tc in range(n_chunks):
            a_rows = [a_ref[tc, h, first_keys, :] for h in range(PEER_HEADS)]
            c_rows = [c_ref[tc, h, first_keys, :] for h in range(PEER_HEADS)]
            for part in range(PEER_NKEYS // GATE_ROWS):
                krows = slice(part * GATE_ROWS, (part + 1) * GATE_ROWS)
                gates = [jnp.zeros((GATE_ROWS, LANES), F32) for _ in range(keys_per_sub)]
                for h in range(PEER_HEADS):
                    s1 = s1_ref[tc, h, krows, :]
                    b = b_ref[tc, h, krows, :]
                    for k in range(keys_per_sub):
                        il = s * keys_per_sub + k
                        keep = s1 >= c_rows[h][il:il + 1, :]
                        gates[k] = gates[k] + jnp.where(keep, b, 0.0) * a_rows[h][il:il + 1, :]
                for k in range(keys_per_sub):
                    rows = slice(k * PEER_NKEYS + part * GATE_ROWS, k * PEER_NKEYS + (part + 1) * GATE_ROWS)
                    w_ref[s % 2, tc, rows, :] = (_gelu(act_ref[s % 2, tc, rows, :]) * gates[k]).astype(BF16)

    def accumulate(s):
        w = jnp.concatenate([w_ref[s % 2, tc] for tc in range(n_chunks)], axis=1)
        acc_ref[...] += _dot(vt_ref[:, s * PEER_SB:(s + 1) * PEER_SB], w)

    activations(0)
    for s in range(n_sub):
        if s + 1 < n_sub:
            activations(s + 1)
        gated(s)
        if s >= 1:
            accumulate(s - 1)
    accumulate(n_sub - 1)

    @pl.when(j == pl.num_programs(1) - 1)
    def _():
        out = x1_ref[...] + g2_ref[0, 0] * acc_ref[...].T
        ms = jnp.mean(out * out, axis=-1, keepdims=True)
        y_ref[...] = out * lax.rsqrt(ms + EPS) * fg_ref[...]


def _peer_dense(h2t, u_bf, vt_bf, a, c, b, s1, x1, mods, final_g, tokens_per_row):
    T = h2t.shape[1]
    tm, te = PEER_TM, PEER_TE
    if tokens_per_row is None:
        g2_map = lambda i, j: (CTX_ROW, 5, 0, 0)
    else:
        g2_map = lambda i, j: ((i * tm) // tokens_per_row, 5, 0, 0)
    stat = pl.BlockSpec((tm // LANES, PEER_HEADS, PEER_NKEYS, LANES), lambda i, j: (i, 0, 0, 0))
    return pl.pallas_call(
        _peer_dense_kernel,
        grid=(T // tm, PEER_N // te),
        in_specs=[
            pl.BlockSpec((D_MODEL, tm), lambda i, j: (0, i)),
            pl.BlockSpec((te, D_MODEL), lambda i, j: (j, 0)),
            pl.BlockSpec((D_MODEL, te), lambda i, j: (0, j)),
            stat, stat, stat, stat,
            pl.BlockSpec((tm, D_MODEL), lambda i, j: (i, 0)),
            pl.BlockSpec((1, 1, 1, D_MODEL), g2_map),
            pl.BlockSpec((1, D_MODEL), lambda i, j: (0, 0)),
        ],
        out_specs=pl.BlockSpec((tm, D_MODEL), lambda i, j: (i, 0)),
        out_shape=jax.ShapeDtypeStruct((T, D_MODEL), F32),
        scratch_shapes=[
            pltpu.VMEM((D_MODEL, tm), F32),
            pltpu.VMEM((2, tm // LANES, PEER_SB, LANES), F32),
            pltpu.VMEM((2, tm // LANES, PEER_SB, LANES), BF16),
        ],
        compiler_params=_params(("parallel", "arbitrary")),
        name="peer_dense",
    )(h2t, u_bf, vt_bf, a, c, b, s1, x1, mods, final_g.reshape(1, D_MODEL))


def _trunk(x, mods, h0r, h0i, s5w, p, tables, final_g, per_batch, grid):
    B, L, _ = x.shape
    a_re, a_im, wb, wc_re, wc_im = s5w
    zs2d, zp = _inproj(x, mods, p["norm1_g"], p["w_in"], per_batch)
    zs = zs2d.reshape(L, B, D_SSM)
    yf, sfr, sfi = _s5_direction(zs, a_re, a_im, wb, wc_re, wc_im, h0r[:, 0], h0i[:, 0], 0)
    yb, sbr, sbi = _s5_direction(zs, a_re, a_im, wb, wc_re, wc_im, h0r[:, 1], h0i[:, 1], 1)
    x1, h2t = _mix(x, zs2d, zp, yf.reshape(L, B * D_SSM), yb.reshape(L, B * D_SSM), mods, p["ssm_d"],
                   p["w_glu"], p["w_pool"], p["pool_scale"], p["w_out"], p["norm2_g"], per_batch, grid)
    wq_t, wka, kb, u_bf, vt_bf = tables
    a, c, b, s1 = _peer_prep(h2t, wq_t, wka, kb)
    y = _peer_dense(h2t, u_bf, vt_bf, a, c, b, s1, x1, mods, final_g, L if per_batch else None)
    new_re = jnp.stack([sfr, sbr], axis=1).reshape(B, 2, SSM_G, SSM_P)
    new_im = jnp.stack([sfi, sbi], axis=1).reshape(B, 2, SSM_G, SSM_P)
    return y.reshape(B, L, D_MODEL), new_re, new_im


def kernel(x_prompt, x_sample, state_ssm_re, state_ssm_im, c, c_ctx, norm1_g, w_mod, b_mod, w_in,
           ssm_lambda_re, ssm_lambda_im, ssm_log_dt, ssm_b_re, ssm_b_im, ssm_c_re, ssm_c_im, ssm_d,
           w_glu, w_pool, pool_scale, w_out, norm2_g, peer_wq, peer_subkeys, peer_u, peer_v, final_g):
    depth = w_mod.shape[0]
    assert depth == 1, "single trunk layer"
    l = 0
    n_dec = c.shape[0]
    cond = jnp.zeros((COND_ROWS, D_MODEL), F32).at[:n_dec].set(c).at[CTX_ROW].set(c_ctx)
    mods = _mod_vectors(cond, w_mod[l], b_mod[l]).reshape(COND_ROWS, N_MOD, 1, D_MODEL)

    ar, ai, bbr, bbi = _discretise(ssm_lambda_re[l], ssm_lambda_im[l], ssm_log_dt[l], ssm_b_re[l], ssm_b_im[l])
    s5w = _s5_weights(ar, ai, bbr, bbi, ssm_c_re[l], ssm_c_im[l])

    wq_t, wka, kb = _peer_tables(peer_wq[l], peer_subkeys[l])
    tables = (wq_t, wka, kb, peer_u[l].astype(BF16), peer_v[l].T.astype(BF16))

    p = {"norm1_g": norm1_g[l], "w_in": w_in[l].astype(BF16), "ssm_d": ssm_d[l], "w_glu": w_glu[l],
         "w_pool": w_pool[l], "pool_scale": pool_scale[l], "w_out": w_out[l], "norm2_g": norm2_g[l]}

    bp = x_prompt.shape[0]
    zeros = jnp.zeros((bp, 2, SSM_N), F32)
    y_prompt, new_re, new_im = _trunk(x_prompt, mods, zeros, zeros, s5w, p, tables, final_g, False, False)
    h0r = state_ssm_re[:, l].reshape(n_dec, 2, SSM_N)
    h0i = state_ssm_im[:, l].reshape(n_dec, 2, SSM_N)
    y_sample, _, _ = _trunk(x_sample, mods, h0r, h0i, s5w, p, tables, final_g, True, True)
    return (y_prompt, y_sample, new_re[:, None], new_im[:, None])
```

```python
import functools
import math

import numpy as np
import jax
import jax.numpy as jnp
from jax import lax
from jax.experimental import pallas as pl
from jax.experimental.pallas import tpu as pltpu

F32 = jnp.float32
BF16 = jnp.bfloat16

D_MODEL = 1024
D_SSM = 512
D_POOL = 512
SSM_H = 16
SSM_G = 32
SSM_P = 64
SSM_N = SSM_G * SSM_P
POOL_WINDOWS = (2, 4, 8, 16)
POOL_G = 4
POOL_C = 128
GRID_W = 64
PEER_HEADS = 8
PEER_NKEYS = 128
PEER_N = PEER_NKEYS * PEER_NKEYS
PEER_DHALF = 128
PEER_TOPK = 16
N_MOD = 6
EPS = 1e-6

SUBLANES = 8
LANES = 128
VMEM_LIMIT = 56 * 1024 * 1024

COND_ROWS = 16
CTX_ROW = 8
SEQ_TILE = 256
S5_CHUNK = 64
S5_LANES = 512
PREP_TILE = 256
PEER_TM = 512
PEER_SB = 256
DMA_SLOTS = 4
RANK_ROWS = 64


def _gelu(x):
    return 0.5 * x * (1.0 + jnp.tanh(0.7978845608028654 * (x + 0.044715 * (x * x * x))))


def _split_bf16(x):
    hi = x.astype(BF16)
    lo = (x - hi.astype(F32)).astype(BF16)
    return hi, lo


def _dot(a, b):
    return jnp.dot(a, b, preferred_element_type=F32)


def _params(sem):
    return pltpu.CompilerParams(dimension_semantics=sem, vmem_limit_bytes=VMEM_LIMIT)


def _mod_kernel(cond_ref, whi_ref, wlo_ref, b_ref, o_ref):
    c = cond_ref[...]
    s = c * jax.nn.sigmoid(c)
    shi, slo = _split_bf16(s)
    whi = whi_ref[...]
    o_ref[...] = _dot(shi, whi) + _dot(slo, whi) + _dot(shi, wlo_ref[...]) + b_ref[...]


def _mod_vectors(cond, w_mod, b_mod):
    whi, wlo = _split_bf16(w_mod)
    n = w_mod.shape[1]
    bn = D_MODEL
    return pl.pallas_call(
        _mod_kernel,
        grid=(n // bn,),
        in_specs=[
            pl.BlockSpec((COND_ROWS, D_MODEL), lambda k: (0, 0)),
            pl.BlockSpec((D_MODEL, bn), lambda k: (0, k)),
            pl.BlockSpec((D_MODEL, bn), lambda k: (0, k)),
            pl.BlockSpec((1, bn), lambda k: (0, k)),
        ],
        out_specs=pl.BlockSpec((COND_ROWS, bn), lambda k: (0, k)),
        out_shape=jax.ShapeDtypeStruct((COND_ROWS, n), F32),
        compiler_params=_params(("arbitrary",)),
        name="mod",
    )(cond, whi, wlo, b_mod.reshape(1, n))


def _disc_kernel(lr_ref, li_ref, ldt_ref, br_ref, bi_ref, ar_ref, ai_ref, bbr_ref, bbi_ref):
    lr = lr_ref[...]
    li = li_ref[...]
    dt = jnp.exp(ldt_ref[...])
    mag = jnp.exp(lr * dt)
    ar = mag * jnp.cos(li * dt)
    ai = mag * jnp.sin(li * dt)
    den = lr * lr + li * li
    nr = ar - 1.0
    ni = ai
    fr = (nr * lr + ni * li) / den
    fi = (ni * lr - nr * li) / den
    ar_ref[...] = ar
    ai_ref[...] = ai
    br = br_ref[...]
    bi = bi_ref[...]
    frb = fr[:, None, :]
    fib = fi[:, None, :]
    bbr_ref[...] = frb * br - fib * bi
    bbi_ref[...] = frb * bi + fib * br


def _discretise(lam_re, lam_im, log_dt, b_re, b_im):
    rows = 2 * SSM_G
    lr = lam_re.reshape(rows, SSM_P)
    li = lam_im.reshape(rows, SSM_P)
    ldt = jnp.broadcast_to(log_dt.reshape(rows, 1), (rows, SSM_P))
    br = jnp.swapaxes(b_re, -1, -2).reshape(rows, SSM_H, SSM_P)
    bi = jnp.swapaxes(b_im, -1, -2).reshape(rows, SSM_H, SSM_P)
    small = jax.ShapeDtypeStruct((rows, SSM_P), F32)
    big = jax.ShapeDtypeStruct((rows, SSM_H, SSM_P), F32)
    return pl.pallas_call(_disc_kernel, out_shape=(small, small, big, big), name="disc")(lr, li, ldt, br, bi)


def _s5_weights(ar, ai, bbr, bbi, c_re, c_im):
    half_g = SSM_G // 2
    eye = jnp.eye(half_g, dtype=F32)
    a_re = ar.reshape(2, 1, SSM_N)
    a_im = ai.reshape(2, 1, SSM_N)

    def bmat(t):
        t = t.reshape(2, 2, half_g, SSM_H, SSM_P)
        m = jnp.einsum("djghp,gk->djghkp", t, eye)
        return m.reshape(2, 2, half_g * SSM_H, half_g * SSM_P)

    def cmat(t):
        t = t.reshape(2, 2, half_g, SSM_H, SSM_P)
        m = jnp.einsum("djghp,gk->djgpkh", t, eye)
        return m.reshape(2, 2, half_g * SSM_P, half_g * SSM_H)

    wb = jnp.concatenate([bmat(bbr), bmat(bbi)], axis=-1).astype(BF16)
    wc_re = cmat(c_re).astype(BF16)
    wc_im = cmat(c_im).astype(BF16)
    return a_re, a_im, wb, wc_re, wc_im


def _inproj_kernel(x_ref, sh_ref, sc_ref, g_ref, w_ref, zs_ref, zp_ref):
    x = x_ref[0]
    ms = jnp.mean(x * x, axis=-1, keepdims=True)
    y = x * lax.rsqrt(ms + EPS) * g_ref[...]
    h = y * (1.0 + sc_ref[0, 0]) + sh_ref[0, 0]
    z = _dot(h.astype(BF16), w_ref[...])
    zs_ref[...] = z[:, :D_SSM]
    zp_ref[0] = z[:, D_SSM:]


def _mod_spec(k, per_batch, tokens_per_row=None):
    if per_batch:
        return pl.BlockSpec((1, 1, 1, D_MODEL), lambda b, i: (b, k, 0, 0))
    return pl.BlockSpec((1, 1, 1, D_MODEL), lambda b, i: (CTX_ROW, k, 0, 0))


def _inproj(x, mods, g, w_in, per_batch):
    B, L, _ = x.shape
    tm = SEQ_TILE
    return pl.pallas_call(
        _inproj_kernel,
        grid=(B, L // tm),
        in_specs=[
            pl.BlockSpec((1, tm, D_MODEL), lambda b, i: (b, i, 0)),
            _mod_spec(0, per_batch),
            _mod_spec(1, per_batch),
            pl.BlockSpec((1, D_MODEL), lambda b, i: (0, 0)),
            pl.BlockSpec((D_MODEL, D_MODEL), lambda b, i: (0, 0)),
        ],
        out_specs=[
            pl.BlockSpec((tm, D_SSM), lambda b, i: (i, b)),
            pl.BlockSpec((1, tm, D_POOL), lambda b, i: (b, i, 0)),
        ],
        out_shape=[
            jax.ShapeDtypeStruct((L, B * D_SSM), F32),
            jax.ShapeDtypeStruct((B, L, D_POOL), F32),
        ],
        compiler_params=_params(("parallel", "parallel")),
        name="inproj",
    )(x, mods, mods, g.reshape(1, D_MODEL), w_in)


def _s5_kernel(u_ref, are_ref, aim_ref, wb_ref, wcr_ref, wci_ref, h0r_ref, h0i_ref,
               y_ref, sr_ref, si_ref, bur_ref, bui_ref, cr_ref, ci_ref, *, reverse, n_chunks):
    c = pl.program_id(1)
    rows = S5_CHUNK * SUBLANES
    half = SSM_N // 2
    half_in = D_SSM // 2

    @pl.when(c == 0)
    def _():
        cr_ref[...] = h0r_ref[...]
        ci_ref[...] = h0i_ref[...]

    u = u_ref[...].reshape(rows, D_SSM).astype(BF16)
    for j in range(2):
        r = _dot(u[:, j * half_in:(j + 1) * half_in], wb_ref[0, j])
        bur_ref[:, j * half:(j + 1) * half] = r[:, :half]
        bui_ref[:, j * half:(j + 1) * half] = r[:, half:]

    for lc in range(SSM_N // S5_LANES):
        ls = slice(lc * S5_LANES, (lc + 1) * S5_LANES)
        ar = jnp.broadcast_to(are_ref[0, :, ls], (SUBLANES, S5_LANES))
        ai = jnp.broadcast_to(aim_ref[0, :, ls], (SUBLANES, S5_LANES))

        def step(i, carry, ls=ls, ar=ar, ai=ai):
            hr, hi = carry
            t = (S5_CHUNK - 1 - i) if reverse else i
            row = pl.multiple_of(t * SUBLANES, SUBLANES)
            nhr = ar * hr - ai * hi + bur_ref[pl.ds(row, SUBLANES), ls]
            nhi = ar * hi + ai * hr + bui_ref[pl.ds(row, SUBLANES), ls]
            bur_ref[pl.ds(row, SUBLANES), ls] = nhr
            bui_ref[pl.ds(row, SUBLANES), ls] = nhi
            return nhr, nhi

        hr, hi = lax.fori_loop(0, S5_CHUNK, step, (cr_ref[:, ls], ci_ref[:, ls]), unroll=4)
        cr_ref[:, ls] = hr
        ci_ref[:, ls] = hi

    ys = []
    for j in range(2):
        hs = slice(j * half, (j + 1) * half)
        yr = _dot(bur_ref[:, hs].astype(BF16), wcr_ref[0, j])
        yi = _dot(bui_ref[:, hs].astype(BF16), wci_ref[0, j])
        ys.append(yr - yi)
    y_ref[...] = jnp.concatenate(ys, axis=-1).reshape(S5_CHUNK, SUBLANES, D_SSM)

    @pl.when(c == n_chunks - 1)
    def _():
        sr_ref[...] = cr_ref[...]
        si_ref[...] = ci_ref[...]


def _s5_direction(zs, a_re, a_im, wb, wc_re, wc_im, h0r, h0i, d):
    L, B, _ = zs.shape
    n_chunks = L // S5_CHUNK
    reverse = d == 1
    cidx = (lambda g, c: n_chunks - 1 - c) if reverse else (lambda g, c: c)
    rows = S5_CHUNK * SUBLANES
    kern = functools.partial(_s5_kernel, reverse=reverse, n_chunks=n_chunks)
    state = jax.ShapeDtypeStruct((B, SSM_N), F32)
    return pl.pallas_call(
        kern,
        grid=(B // SUBLANES, n_chunks),
        in_specs=[
            pl.BlockSpec((S5_CHUNK, SUBLANES, D_SSM), lambda g, c: (cidx(g, c), g, 0)),
            pl.BlockSpec((1, 1, SSM_N), lambda g, c: (d, 0, 0)),
            pl.BlockSpec((1, 1, SSM_N), lambda g, c: (d, 0, 0)),
            pl.BlockSpec((1, 2, D_SSM // 2, SSM_N), lambda g, c: (d, 0, 0, 0)),
            pl.BlockSpec((1, 2, SSM_N // 2, D_SSM // 2), lambda g, c: (d, 0, 0, 0)),
            pl.BlockSpec((1, 2, SSM_N // 2, D_SSM // 2), lambda g, c: (d, 0, 0, 0)),
            pl.BlockSpec((SUBLANES, SSM_N), lambda g, c: (g, 0)),
            pl.BlockSpec((SUBLANES, SSM_N), lambda g, c: (g, 0)),
        ],
        out_specs=[
            pl.BlockSpec((S5_CHUNK, SUBLANES, D_SSM), lambda g, c: (cidx(g, c), g, 0)),
            pl.BlockSpec((SUBLANES, SSM_N), lambda g, c: (g, 0)),
            pl.BlockSpec((SUBLANES, SSM_N), lambda g, c: (g, 0)),
        ],
        out_shape=[jax.ShapeDtypeStruct((L, B, D_SSM), F32), state, state],
        scratch_shapes=[
            pltpu.VMEM((rows, SSM_N), F32),
            pltpu.VMEM((rows, SSM_N), F32),
            pltpu.VMEM((SUBLANES, SSM_N), F32),
            pltpu.VMEM((SUBLANES, SSM_N), F32),
        ],
        compiler_params=_params(("parallel", "arbitrary")),
        name="s5_bwd" if reverse else "s5_fwd",
    )(zs, a_re, a_im, wb, wc_re, wc_im, h0r, h0i)


def _pool_tables(grid):
    seg = GRID_W if grid else SEQ_TILE
    t = np.arange(SEQ_TILE)
    pos = t % seg
    base = t - pos
    mats, invs = [], []
    for w in POOL_WINDOWS:
        lo = np.clip(pos - w // 2, 0, seg)
        hi = np.clip(pos + (w - w // 2), 0, seg)
        s = t[None, :]
        m = (s >= (base + lo)[:, None]) & (s < (base + hi)[:, None])
        mats.append(m.astype(np.float32))
        invs.append(np.broadcast_to((1.0 / (hi - lo).astype(np.float32))[:, None], (SEQ_TILE, LANES)))
    return jnp.asarray(np.stack(mats), dtype=BF16), jnp.asarray(np.stack(invs), dtype=F32)


def _mix_kernel(x_ref, u_ref, zp_ref, yf_ref, yb_ref, g1_ref, sh2_ref, sc2_ref, d_ref, wglu_ref,
                pm_ref, pinv_ref, wpool_ref, pscale_ref, wout_ref, n2_ref, x1_ref, h2t_ref):
    x = x_ref[0]
    y = d_ref[...] * u_ref[...] + yf_ref[...] + yb_ref[...]
    g = _gelu(y)
    ys = g * jax.nn.sigmoid(_dot(g.astype(BF16), wglu_ref[...]))

    zp = zp_ref[0]
    parts = [ys]
    for gi in range(POOL_G):
        zg = zp[:, gi * POOL_C:(gi + 1) * POOL_C]
        hi, lo = _split_bf16(zg)
        pm = pm_ref[gi]
        win = _dot(pm, hi) + _dot(pm, lo)
        pooled = win * pinv_ref[gi] - zg
        og = _dot(pooled.astype(BF16), wpool_ref[gi])
        parts.append(og * pscale_ref[:, gi * POOL_C:(gi + 1) * POOL_C])
    mix = _dot(jnp.concatenate(parts, axis=-1).astype(BF16), wout_ref[...])

    x1 = x + g1_ref[0, 0] * mix
    x1_ref[...] = x1
    ms = jnp.mean(x1 * x1, axis=-1, keepdims=True)
    h2 = x1 * lax.rsqrt(ms + EPS) * n2_ref[...]
    h2 = h2 * (1.0 + sc2_ref[0, 0]) + sh2_ref[0, 0]
    h2t_ref[...] = h2.T.astype(BF16)


def _mix(x, zs2d, zp, yf2d, yb2d, mods, ssm_d, w_glu, w_pool, pool_scale, w_out, norm2_g, per_batch, grid):
    B, L, _ = x.shape
    tm = SEQ_TILE
    nt = L // tm
    pm, pinv = _pool_tables(grid)
    tm_spec = pl.BlockSpec((tm, D_SSM), lambda b, i: (i, b))
    const2 = lambda b, i: (0, 0)
    const3 = lambda b, i: (0, 0, 0)
    return pl.pallas_call(
        _mix_kernel,
        grid=(B, nt),
        in_specs=[
            pl.BlockSpec((1, tm, D_MODEL), lambda b, i: (b, i, 0)),
            tm_spec,
            pl.BlockSpec((1, tm, D_POOL), lambda b, i: (b, i, 0)),
            tm_spec,
            tm_spec,
            _mod_spec(2, per_batch),
            _mod_spec(3, per_batch),
            _mod_spec(4, per_batch),
            pl.BlockSpec((1, D_SSM), const2),
            pl.BlockSpec((D_SSM, D_SSM), const2),
            pl.BlockSpec((POOL_G, tm, tm), const3),
            pl.BlockSpec((POOL_G, tm, LANES), const3),
            pl.BlockSpec((POOL_G, POOL_C, POOL_C), const3),
            pl.BlockSpec((1, D_POOL), const2),
            pl.BlockSpec((D_MODEL, D_MODEL), const2),
            pl.BlockSpec((1, D_MODEL), const2),
        ],
        out_specs=[
            pl.BlockSpec((tm, D_MODEL), lambda b, i: (b * nt + i, 0)),
            pl.BlockSpec((D_MODEL, tm), lambda b, i: (0, b * nt + i)),
        ],
        out_shape=[
            jax.ShapeDtypeStruct((B * L, D_MODEL), F32),
            jax.ShapeDtypeStruct((D_MODEL, B * L), BF16),
        ],
        compiler_params=_params(("parallel", "parallel")),
        name="mix",
    )(x, zs2d, zp, yf2d, yb2d, mods, mods, mods, ssm_d.reshape(1, D_SSM), w_glu.astype(BF16),
      pm, pinv, w_pool.astype(BF16), pool_scale.reshape(1, D_POOL), w_out.astype(BF16),
      norm2_g.reshape(1, D_MODEL))


def _sort_pairs(n):
    pairs = []

    def merge(lo, hi, r):
        step = r * 2
        if step < hi - lo:
            merge(lo, hi, step)
            merge(lo + r, hi, step)
            for i in range(lo + r, hi - r, step):
                pairs.append((i, i + r))
        else:
            pairs.append((lo, lo + r))

    def sort(lo, hi):
        if hi - lo >= 1:
            mid = lo + (hi - lo) // 2
            sort(lo, mid)
            sort(mid + 1, hi)
            merge(lo, hi, 1)

    sort(0, n - 1)
    return pairs


_SORT16 = _sort_pairs(PEER_TOPK)


def _vmax(a, b):
    if a is None:
        return b
    if b is None:
        return a
    return jnp.maximum(a, b)


def _vmin(a, b):
    if a is None or b is None:
        return None
    return jnp.minimum(a, b)


def _sort16_desc(w):
    w = list(w)
    for i, j in _SORT16:
        w[i], w[j] = _vmax(w[i], w[j]), _vmin(w[i], w[j])
    return w


def _merge_top16(a, b):
    k = PEER_TOPK
    w = [_vmax(a[i], b[k - 1 - i]) for i in range(k)]
    d = k // 2
    while d >= 1:
        for i in range(k):
            if i & d == 0:
                w[i], w[i + d] = _vmax(w[i], w[i + d]), _vmin(w[i], w[i + d])
        d //= 2
    return w


def _top16_of_128(load):
    cur = None
    for grp in range(PEER_NKEYS // PEER_TOPK):
        s = _sort16_desc([load(grp * PEER_TOPK + i) for i in range(PEER_TOPK)])
        cur = s if cur is None else _merge_top16(cur, s)
    return cur


def _top16_pair_sums(sv0, sv1):
    k = PEER_TOPK
    pad = lambda row: row + [None] * (k - len(row))
    cur = [sv0[0] + sv1[j] for j in range(k)]
    for i in range(1, k // 2):
        cur = _merge_top16(cur, pad([sv0[i] + sv1[j] for j in range(k // (i + 1))]))
    return _merge_top16(cur, pad([sv0[i] + sv1[0] for i in range(k // 2, k)]))


def _count_sorted(v, x, strict):
    above = (lambda p: p > x) if strict else (lambda p: p >= x)
    sel = jnp.where
    g8 = above(v[7])
    g4 = above(sel(g8, v[11], v[3]))
    g2 = above(sel(g8, sel(g4, v[13], v[9]), sel(g4, v[5], v[1])))
    g1 = above(sel(g8, sel(g4, sel(g2, v[14], v[12]), sel(g2, v[10], v[8])),
                   sel(g4, sel(g2, v[6], v[4]), sel(g2, v[2], v[0]))))
    low = sel(g8, 8.0, 0.0) + sel(g4, 4.0, 0.0) + sel(g2, 2.0, 0.0) + sel(g1, 1.0, 0.0)
    return sel(above(v[15]), 16.0, low)


def _as_words(x):
    return pltpu.bitcast(x.astype(BF16), jnp.uint32)


def _as_bf16(words):
    return pltpu.bitcast(words, BF16)


def _peer_prep_kernel(ht_ref, wq_ref, wka_ref, kb_ref, a_ref, m_ref, b_ref, r_ref, q_ref, sa_ref, st_ref):
    tm = ht_ref.shape[1]
    half = PEER_HEADS * PEER_DHALF
    q_ref[...] = _dot(wq_ref[...], ht_ref[...]).astype(BF16)
    for side in range(2):
        s = _dot(wka_ref[side], q_ref[side * half:(side + 1) * half, :])
        sa_ref[side] = s.reshape(PEER_NKEYS, PEER_HEADS, tm)

    for lc in range(tm // LANES):
        ls = slice(lc * LANES, (lc + 1) * LANES)
        sv0 = _top16_of_128(lambda n: sa_ref[0, n, :, ls])
        sv1 = _top16_of_128(lambda n: sa_ref[1, n, :, ls])
        top = _top16_pair_sums(sv0, sv1)
        tau = top[PEER_TOPK - 1]
        best = sv0[0] + sv1[0]
        z = jnp.exp(top[0] - best)
        for k in range(1, PEER_TOPK):
            z = z + jnp.exp(top[k] - best)
        inv_z = 1.0 / z
        slack = (jnp.abs(tau) + jnp.abs(sv0[0]) + jnp.abs(sv0[PEER_TOPK - 1])
                 + jnp.abs(sv1[0]) + jnp.abs(sv1[PEER_TOPK - 1])) * (2.0 ** -21) + 1e-30
        cut = tau - slack
        best0 = sv0[0]
        for k in range(PEER_TOPK):
            st_ref[k, :, ls] = sv1[k]

        def first_key(n, carry, lc=lc, ls=ls, sv1=sv1, cut=cut, best0=best0, inv_z=inv_z):
            s0 = sa_ref[0, n, :, ls]
            a_ref[lc, n] = jnp.exp(s0 - best0) * inv_z
            m_ref[lc, n] = _count_sorted(sv1, cut - s0, strict=False)
            return carry

        lax.fori_loop(0, PEER_NKEYS, first_key, 0, unroll=2)

    for h in range(PEER_HEADS):
        s1 = _dot(kb_ref[h], q_ref[half + h * PEER_DHALF:half + (h + 1) * PEER_DHALF, :])
        for lc in range(tm // LANES):
            ls = slice(lc * LANES, (lc + 1) * LANES)
            sorted1 = [st_ref[k, h:h + 1, ls] for k in range(PEER_TOPK)]
            for part in range(PEER_NKEYS // RANK_ROWS):
                x = s1[part * RANK_ROWS:(part + 1) * RANK_ROWS, ls]
                words = slice(part * RANK_ROWS // 2, (part + 1) * RANK_ROWS // 2)
                b_ref[lc, h, words, :] = _as_words(jnp.exp(x - sorted1[0]))
                r_ref[lc, h, words, :] = _as_words(_count_sorted(sorted1, x, strict=True))


def _peer_tables(peer_wq, sub_keys):
    wq = peer_wq.reshape(D_MODEL, PEER_HEADS, 2, PEER_DHALF)
    wq_t = jnp.transpose(wq, (2, 1, 3, 0)).reshape(2 * PEER_HEADS * PEER_DHALF, D_MODEL).astype(BF16)
    eye = jnp.eye(PEER_HEADS, dtype=F32)
    wka = jnp.einsum("hsnk,hg->snhgk", sub_keys, eye)
    wka = wka.reshape(2, PEER_NKEYS * PEER_HEADS, PEER_HEADS * PEER_DHALF).astype(BF16)
    kb = sub_keys[:, 1].astype(BF16)
    return wq_t, wka, kb


def _peer_prep(h2t, wq_t, wka, kb):
    T = h2t.shape[1]
    tm = PREP_TILE
    half = PEER_HEADS * PEER_DHALF
    first = jax.ShapeDtypeStruct((T // LANES, PEER_NKEYS, PEER_HEADS, LANES), F32)
    second = jax.ShapeDtypeStruct((T // LANES, PEER_HEADS, PEER_NKEYS // 2, LANES), jnp.uint32)
    fspec = pl.BlockSpec((tm // LANES, PEER_NKEYS, PEER_HEADS, LANES), lambda i: (i, 0, 0, 0))
    sspec = pl.BlockSpec((tm // LANES, PEER_HEADS, PEER_NKEYS // 2, LANES), lambda i: (i, 0, 0, 0))
    return pl.pallas_call(
        _peer_prep_kernel,
        grid=(T // tm,),
        in_specs=[
            pl.BlockSpec((D_MODEL, tm), lambda i: (0, i)),
            pl.BlockSpec((2 * half, D_MODEL), lambda i: (0, 0)),
            pl.BlockSpec((2, PEER_NKEYS * PEER_HEADS, half), lambda i: (0, 0, 0)),
            pl.BlockSpec((PEER_HEADS, PEER_NKEYS, PEER_DHALF), lambda i: (0, 0, 0)),
        ],
        out_specs=[fspec, fspec, sspec, sspec],
        out_shape=[first, first, second, second],
        scratch_shapes=[
            pltpu.VMEM((2 * half, tm), BF16),
            pltpu.VMEM((2, PEER_NKEYS, PEER_HEADS, tm), F32),
            pltpu.VMEM((PEER_TOPK, PEER_HEADS, tm), F32),
        ],
        compiler_params=_params(("parallel",)),
        name="peer_prep",
    )(h2t, wq_t, wka, kb)


def _peer_dense_kernel(ht_ref, a_ref, m_ref, b_ref, r_ref, x1_ref, g2_ref, fg_ref, u_hbm, vt_hbm,
                       y_ref, ubuf, vbuf, sem, acc_ref, act_ref, w_ref):
    tm = ht_ref.shape[1]
    n_chunks = tm // LANES
    n_sub = PEER_N // PEER_SB
    keys_per_sub = PEER_SB // PEER_NKEYS

    def u_copy(s, ph):
        return pltpu.make_async_copy(u_hbm.at[pl.ds(s * PEER_SB, PEER_SB), :], ubuf.at[ph], sem.at[0, ph])

    def v_copy(s, ph):
        return pltpu.make_async_copy(vt_hbm.at[s], vbuf.at[ph], sem.at[1, ph])

    def activations(s, ph):
        act = _dot(ubuf[ph], ht_ref[...])
        for tc in range(n_chunks):
            act_ref[ph % 2, tc] = act[:, tc * LANES:(tc + 1) * LANES]

    def gated(s, ph):
        for tc in range(n_chunks):
            gates = []
            for k in range(keys_per_sub):
                a_rows = a_ref[tc, s * keys_per_sub + k]
                m_rows = m_ref[tc, s * keys_per_sub + k]
                gate = None
                for h in range(PEER_HEADS):
                    a_row = jnp.broadcast_to(a_rows[h:h + 1, :], (PEER_NKEYS, LANES)).astype(BF16)
                    m_row = jnp.broadcast_to(m_rows[h:h + 1, :], (PEER_NKEYS, LANES)).astype(BF16)
                    b = _as_bf16(b_ref[tc, h])
                    term = jnp.where(_as_bf16(r_ref[tc, h]) < m_row, b, jnp.zeros_like(b)) * a_row
                    gate = term if gate is None else gate + term
                gates.append(gate)
            for k in range(keys_per_sub):
                g = _gelu(act_ref[ph % 2, tc, k * PEER_NKEYS:(k + 1) * PEER_NKEYS, :])
                words = slice(k * PEER_NKEYS // 2, (k + 1) * PEER_NKEYS // 2)
                w_ref[ph % 2, tc, words, :] = pltpu.bitcast(g.astype(BF16) * gates[k], jnp.uint32)

    def accumulate(s, ph):
        w = jnp.concatenate([w_ref[ph % 2, tc] for tc in range(n_chunks)], axis=1)
        acc_ref[...] += _dot(vbuf[ph], _as_bf16(w))

    def stage(s, ph):
        if isinstance(s, int):
            if s + 2 < n_sub:
                u_copy(s + 2, (ph + 2) % DMA_SLOTS).start()
            if s < n_sub:
                v_copy(s, ph).start()
                u_copy(s, ph).wait()
            if 2 <= s:
                v_copy(s - 2, (ph - 2) % DMA_SLOTS).wait()
            if s < n_sub:
                activations(s, ph)
            if 1 <= s <= n_sub:
                gated(s - 1, (ph - 1) % DMA_SLOTS)
            if 2 <= s:
                accumulate(s - 2, (ph - 2) % DMA_SLOTS)
        else:
            u_copy(s + 2, (ph + 2) % DMA_SLOTS).start()
            v_copy(s, ph).start()
            u_copy(s, ph).wait()
            v_copy(s - 2, (ph - 2) % DMA_SLOTS).wait()
            activations(s, ph)
            gated(s - 1, (ph - 1) % DMA_SLOTS)
            accumulate(s - 2, (ph - 2) % DMA_SLOTS)

    u_copy(0, 0).start()
    u_copy(1, 1).start()
    acc_ref[...] = jnp.zeros_like(acc_ref)
    stage(0, 0)
    stage(1, 1)

    first = 2
    n_loop = (n_sub - 2 - first) // DMA_SLOTS

    def stages(t, carry):
        base = first + t * DMA_SLOTS
        for jj in range(DMA_SLOTS):
            stage(base + jj, (first + jj) % DMA_SLOTS)
        return carry

    lax.fori_loop(0, n_loop, stages, 0)
    for s in range(first + n_loop * DMA_SLOTS, n_sub + 2):
        stage(s, s % DMA_SLOTS)


    out = x1_ref[...] + g2_ref[0, 0] * acc_ref[...].T
    ms = jnp.mean(out * out, axis=-1, keepdims=True)
    y_ref[...] = out * lax.rsqrt(ms + EPS) * fg_ref[...]


def _peer_dense(h2t, u_bf, vt_sub, a, m, b, r, x1, mods, final_g, tokens_per_row):
    T = h2t.shape[1]
    tm = PEER_TM
    if tokens_per_row is None:
        g2_map = lambda i: (CTX_ROW, 5, 0, 0)
    else:
        g2_map = lambda i: ((i * tm) // tokens_per_row, 5, 0, 0)
    fspec = pl.BlockSpec((tm // LANES, PEER_NKEYS, PEER_HEADS, LANES), lambda i: (i, 0, 0, 0))
    sspec = pl.BlockSpec((tm // LANES, PEER_HEADS, PEER_NKEYS // 2, LANES), lambda i: (i, 0, 0, 0))
    return pl.pallas_call(
        _peer_dense_kernel,
        grid=(T // tm,),
        in_specs=[
            pl.BlockSpec((D_MODEL, tm), lambda i: (0, i)),
            fspec, fspec, sspec, sspec,
            pl.BlockSpec((tm, D_MODEL), lambda i: (i, 0)),
            pl.BlockSpec((1, 1, 1, D_MODEL), g2_map),
            pl.BlockSpec((1, D_MODEL), lambda i: (0, 0)),
            pl.BlockSpec(memory_space=pl.ANY),
            pl.BlockSpec(memory_space=pl.ANY),
        ],
        out_specs=pl.BlockSpec((tm, D_MODEL), lambda i: (i, 0)),
        out_shape=jax.ShapeDtypeStruct((T, D_MODEL), F32),
        scratch_shapes=[
            pltpu.VMEM((DMA_SLOTS, PEER_SB, D_MODEL), BF16),
            pltpu.VMEM((DMA_SLOTS, D_MODEL, PEER_SB), BF16),
            pltpu.SemaphoreType.DMA((2, DMA_SLOTS)),
            pltpu.VMEM((D_MODEL, tm), F32),
            pltpu.VMEM((2, tm // LANES, PEER_SB, LANES), F32),
            pltpu.VMEM((2, tm // LANES, PEER_SB // 2, LANES), jnp.uint32),
        ],
        compiler_params=_params(("parallel",)),
        name="peer_dense",
    )(h2t, a, m, b, r, x1, mods, final_g.reshape(1, D_MODEL), u_bf, vt_sub)


def _trunk(x, mods, h0r, h0i, s5w, p, tables, final_g, per_batch, grid):
    B, L, _ = x.shape
    a_re, a_im, wb, wc_re, wc_im = s5w
    zs2d, zp = _inproj(x, mods, p["norm1_g"], p["w_in"], per_batch)
    zs = zs2d.reshape(L, B, D_SSM)
    yf, sfr, sfi = _s5_direction(zs, a_re, a_im, wb, wc_re, wc_im, h0r[:, 0], h0i[:, 0], 0)
    yb, sbr, sbi = _s5_direction(zs, a_re, a_im, wb, wc_re, wc_im, h0r[:, 1], h0i[:, 1], 1)
    x1, h2t = _mix(x, zs2d, zp, yf.reshape(L, B * D_SSM), yb.reshape(L, B * D_SSM), mods, p["ssm_d"],
                   p["w_glu"], p["w_pool"], p["pool_scale"], p["w_out"], p["norm2_g"], per_batch, grid)
    wq_t, wka, kb, u_bf, vt_bf = tables
    a, m, b, r = _peer_prep(h2t, wq_t, wka, kb)
    y = _peer_dense(h2t, u_bf, vt_bf, a, m, b, r, x1, mods, final_g, L if per_batch else None)
    new_re = jnp.stack([sfr, sbr], axis=1).reshape(B, 2, SSM_G, SSM_P)
    new_im = jnp.stack([sfi, sbi], axis=1).reshape(B, 2, SSM_G, SSM_P)
    return y.reshape(B, L, D_MODEL), new_re, new_im


def kernel(x_prompt, x_sample, state_ssm_re, state_ssm_im, c, c_ctx, norm1_g, w_mod, b_mod, w_in,
           ssm_lambda_re, ssm_lambda_im, ssm_log_dt, ssm_b_re, ssm_b_im, ssm_c_re, ssm_c_im, ssm_d,
           w_glu, w_pool, pool_scale, w_out, norm2_g, peer_wq, peer_subkeys, peer_u, peer_v, final_g):
    depth = w_mod.shape[0]
    assert depth == 1, "single trunk layer"
    l = 0
    n_dec = c.shape[0]
    cond = jnp.zeros((COND_ROWS, D_MODEL), F32).at[:n_dec].set(c).at[CTX_ROW].set(c_ctx)
    mods = _mod_vectors(cond, w_mod[l], b_mod[l]).reshape(COND_ROWS, N_MOD, 1, D_MODEL)

    ar, ai, bbr, bbi = _discretise(ssm_lambda_re[l], ssm_lambda_im[l], ssm_log_dt[l], ssm_b_re[l], ssm_b_im[l])
    s5w = _s5_weights(ar, ai, bbr, bbi, ssm_c_re[l], ssm_c_im[l])

    wq_t, wka, kb = _peer_tables(peer_wq[l], peer_subkeys[l])
    vt_sub = jnp.transpose(peer_v[l].reshape(PEER_N // PEER_SB, PEER_SB, D_MODEL), (0, 2, 1)).astype(BF16)
    tables = (wq_t, wka, kb, peer_u[l].astype(BF16), vt_sub)

    p = {"norm1_g": norm1_g[l], "w_in": w_in[l].astype(BF16), "ssm_d": ssm_d[l], "w_glu": w_glu[l],
         "w_pool": w_pool[l], "pool_scale": pool_scale[l], "w_out": w_out[l], "norm2_g": norm2_g[l]}

    bp = x_prompt.shape[0]
    zeros = jnp.zeros((bp, 2, SSM_N), F32)
    y_prompt, new_re, new_im = _trunk(x_prompt, mods, zeros, zeros, s5w, p, tables, final_g, False, False)
    h0r = state_ssm_re[:, l].reshape(n_dec, 2, SSM_N)
    h0i = state_ssm_im[:, l].reshape(n_dec, 2, SSM_N)
    y_sample, _, _ = _trunk(x_sample, mods, h0r, h0i, s5w, p, tables, final_g, True, True)
    return (y_prompt, y_sample, new_re[:, None], new_im[:, None])
```

```python
import functools
import math

import numpy as np
import jax
import jax.numpy as jnp
from jax import lax
from jax.experimental import pallas as pl
from jax.experimental.pallas import tpu as pltpu

F32 = jnp.float32
BF16 = jnp.bfloat16

D_MODEL = 1024
D_SSM = 512
D_POOL = 512
SSM_H = 16
SSM_G = 32
SSM_P = 64
SSM_N = SSM_G * SSM_P
POOL_WINDOWS = (2, 4, 8, 16)
POOL_G = 4
POOL_C = 128
GRID_W = 64
PEER_HEADS = 8
PEER_NKEYS = 128
PEER_N = PEER_NKEYS * PEER_NKEYS
PEER_DHALF = 128
PEER_TOPK = 16
N_MOD = 6
EPS = 1e-6

SUBLANES = 8
LANES = 128
VMEM_LIMIT = 56 * 1024 * 1024

COND_ROWS = 16
CTX_ROW = 8
SEQ_TILE = 256
S5_CHUNK = 64
S5_LANES = 512
PREP_TILE = 256
PEER_TM = 512
PEER_SB = 256
DMA_SLOTS = 4
RANK_ROWS = 64
GATE_ROWS = 64


def _gelu(x):
    return 0.5 * x * (1.0 + jnp.tanh(0.7978845608028654 * (x + 0.044715 * (x * x * x))))


def _split_bf16(x):
    hi = x.astype(BF16)
    lo = (x - hi.astype(F32)).astype(BF16)
    return hi, lo


def _dot(a, b):
    return jnp.dot(a, b, preferred_element_type=F32)


def _params(sem, flags=None):
    return pltpu.CompilerParams(dimension_semantics=sem, vmem_limit_bytes=VMEM_LIMIT, flags=flags)


def _mod_kernel(cond_ref, whi_ref, wlo_ref, b_ref, o_ref):
    c = cond_ref[...]
    s = c * jax.nn.sigmoid(c)
    shi, slo = _split_bf16(s)
    whi = whi_ref[...]
    o_ref[...] = _dot(shi, whi) + _dot(slo, whi) + _dot(shi, wlo_ref[...]) + b_ref[...]


def _mod_vectors(cond, w_mod, b_mod):
    whi, wlo = _split_bf16(w_mod)
    n = w_mod.shape[1]
    bn = D_MODEL
    return pl.pallas_call(
        _mod_kernel,
        grid=(n // bn,),
        in_specs=[
            pl.BlockSpec((COND_ROWS, D_MODEL), lambda k: (0, 0)),
            pl.BlockSpec((D_MODEL, bn), lambda k: (0, k)),
            pl.BlockSpec((D_MODEL, bn), lambda k: (0, k)),
            pl.BlockSpec((1, bn), lambda k: (0, k)),
        ],
        out_specs=pl.BlockSpec((COND_ROWS, bn), lambda k: (0, k)),
        out_shape=jax.ShapeDtypeStruct((COND_ROWS, n), F32),
        compiler_params=_params(("arbitrary",)),
        name="mod",
    )(cond, whi, wlo, b_mod.reshape(1, n))


def _disc_kernel(lr_ref, li_ref, ldt_ref, br_ref, bi_ref, ar_ref, ai_ref, bbr_ref, bbi_ref):
    lr = lr_ref[...]
    li = li_ref[...]
    dt = jnp.exp(ldt_ref[...])
    mag = jnp.exp(lr * dt)
    ar = mag * jnp.cos(li * dt)
    ai = mag * jnp.sin(li * dt)
    den = lr * lr + li * li
    nr = ar - 1.0
    ni = ai
    fr = (nr * lr + ni * li) / den
    fi = (ni * lr - nr * li) / den
    ar_ref[...] = ar
    ai_ref[...] = ai
    br = br_ref[...]
    bi = bi_ref[...]
    frb = fr[:, None, :]
    fib = fi[:, None, :]
    bbr_ref[...] = frb * br - fib * bi
    bbi_ref[...] = frb * bi + fib * br


def _discretise(lam_re, lam_im, log_dt, b_re, b_im):
    rows = 2 * SSM_G
    lr = lam_re.reshape(rows, SSM_P)
    li = lam_im.reshape(rows, SSM_P)
    ldt = jnp.broadcast_to(log_dt.reshape(rows, 1), (rows, SSM_P))
    br = jnp.swapaxes(b_re, -1, -2).reshape(rows, SSM_H, SSM_P)
    bi = jnp.swapaxes(b_im, -1, -2).reshape(rows, SSM_H, SSM_P)
    small = jax.ShapeDtypeStruct((rows, SSM_P), F32)
    big = jax.ShapeDtypeStruct((rows, SSM_H, SSM_P), F32)
    return pl.pallas_call(_disc_kernel, out_shape=(small, small, big, big), name="disc")(lr, li, ldt, br, bi)


def _s5_weights(ar, ai, bbr, bbi, c_re, c_im):
    half_g = SSM_G // 2
    eye = jnp.eye(half_g, dtype=F32)
    a_re = ar.reshape(2, 1, SSM_N)
    a_im = ai.reshape(2, 1, SSM_N)

    def bmat(t):
        t = t.reshape(2, 2, half_g, SSM_H, SSM_P)
        m = jnp.einsum("djghp,gk->djghkp", t, eye)
        return m.reshape(2, 2, half_g * SSM_H, half_g * SSM_P)

    def cmat(t):
        t = t.reshape(2, 2, half_g, SSM_H, SSM_P)
        m = jnp.einsum("djghp,gk->djgpkh", t, eye)
        return m.reshape(2, 2, half_g * SSM_P, half_g * SSM_H)

    wb = jnp.concatenate([bmat(bbr), bmat(bbi)], axis=-1).astype(BF16)
    wc_re = cmat(c_re).astype(BF16)
    wc_im = cmat(c_im).astype(BF16)
    return a_re, a_im, wb, wc_re, wc_im


def _inproj_kernel(x_ref, sh_ref, sc_ref, g_ref, w_ref, zs_ref, zp_ref):
    x = x_ref[0]
    ms = jnp.mean(x * x, axis=-1, keepdims=True)
    y = x * lax.rsqrt(ms + EPS) * g_ref[...]
    h = y * (1.0 + sc_ref[0, 0]) + sh_ref[0, 0]
    z = _dot(h.astype(BF16), w_ref[...])
    zs_ref[...] = z[:, :D_SSM]
    zp_ref[0] = z[:, D_SSM:]


def _mod_spec(k, per_batch, tokens_per_row=None):
    if per_batch:
        return pl.BlockSpec((1, 1, 1, D_MODEL), lambda b, i: (b, k, 0, 0))
    return pl.BlockSpec((1, 1, 1, D_MODEL), lambda b, i: (CTX_ROW, k, 0, 0))


def _inproj(x, mods, g, w_in, per_batch):
    B, L, _ = x.shape
    tm = SEQ_TILE
    return pl.pallas_call(
        _inproj_kernel,
        grid=(B, L // tm),
        in_specs=[
            pl.BlockSpec((1, tm, D_MODEL), lambda b, i: (b, i, 0)),
            _mod_spec(0, per_batch),
            _mod_spec(1, per_batch),
            pl.BlockSpec((1, D_MODEL), lambda b, i: (0, 0)),
            pl.BlockSpec((D_MODEL, D_MODEL), lambda b, i: (0, 0)),
        ],
        out_specs=[
            pl.BlockSpec((tm, D_SSM), lambda b, i: (i, b)),
            pl.BlockSpec((1, tm, D_POOL), lambda b, i: (b, i, 0)),
        ],
        out_shape=[
            jax.ShapeDtypeStruct((L, B * D_SSM), F32),
            jax.ShapeDtypeStruct((B, L, D_POOL), F32),
        ],
        compiler_params=_params(("parallel", "parallel")),
        name="inproj",
    )(x, mods, mods, g.reshape(1, D_MODEL), w_in)


def _s5_kernel(u_ref, are_ref, aim_ref, wb_ref, wcr_ref, wci_ref, h0r_ref, h0i_ref,
               y_ref, sr_ref, si_ref, bur_ref, bui_ref, cr_ref, ci_ref, *, reverse, n_chunks):
    c = pl.program_id(1)
    rows = S5_CHUNK * SUBLANES
    half = SSM_N // 2
    half_in = D_SSM // 2

    @pl.when(c == 0)
    def _():
        cr_ref[...] = h0r_ref[...]
        ci_ref[...] = h0i_ref[...]

    u = u_ref[...].reshape(rows, D_SSM).astype(BF16)
    for j in range(2):
        r = _dot(u[:, j * half_in:(j + 1) * half_in], wb_ref[0, j])
        bur_ref[:, j * half:(j + 1) * half] = r[:, :half]
        bui_ref[:, j * half:(j + 1) * half] = r[:, half:]

    for lc in range(SSM_N // S5_LANES):
        ls = slice(lc * S5_LANES, (lc + 1) * S5_LANES)
        ar = jnp.broadcast_to(are_ref[0, :, ls], (SUBLANES, S5_LANES))
        ai = jnp.broadcast_to(aim_ref[0, :, ls], (SUBLANES, S5_LANES))

        def step(i, carry, ls=ls, ar=ar, ai=ai):
            hr, hi = carry
            t = (S5_CHUNK - 1 - i) if reverse else i
            row = pl.multiple_of(t * SUBLANES, SUBLANES)
            nhr = ar * hr - ai * hi + bur_ref[pl.ds(row, SUBLANES), ls]
            nhi = ar * hi + ai * hr + bui_ref[pl.ds(row, SUBLANES), ls]
            bur_ref[pl.ds(row, SUBLANES), ls] = nhr
            bui_ref[pl.ds(row, SUBLANES), ls] = nhi
            return nhr, nhi

        hr, hi = lax.fori_loop(0, S5_CHUNK, step, (cr_ref[:, ls], ci_ref[:, ls]), unroll=4)
        cr_ref[:, ls] = hr
        ci_ref[:, ls] = hi

    ys = []
    for j in range(2):
        hs = slice(j * half, (j + 1) * half)
        yr = _dot(bur_ref[:, hs].astype(BF16), wcr_ref[0, j])
        yi = _dot(bui_ref[:, hs].astype(BF16), wci_ref[0, j])
        ys.append(yr - yi)
    y_ref[...] = jnp.concatenate(ys, axis=-1).reshape(S5_CHUNK, SUBLANES, D_SSM)

    @pl.when(c == n_chunks - 1)
    def _():
        sr_ref[...] = cr_ref[...]
        si_ref[...] = ci_ref[...]


def _s5_direction(zs, a_re, a_im, wb, wc_re, wc_im, h0r, h0i, d):
    L, B, _ = zs.shape
    n_chunks = L // S5_CHUNK
    reverse = d == 1
    cidx = (lambda g, c: n_chunks - 1 - c) if reverse else (lambda g, c: c)
    rows = S5_CHUNK * SUBLANES
    kern = functools.partial(_s5_kernel, reverse=reverse, n_chunks=n_chunks)
    state = jax.ShapeDtypeStruct((B, SSM_N), F32)
    return pl.pallas_call(
        kern,
        grid=(B // SUBLANES, n_chunks),
        in_specs=[
            pl.BlockSpec((S5_CHUNK, SUBLANES, D_SSM), lambda g, c: (cidx(g, c), g, 0)),
            pl.BlockSpec((1, 1, SSM_N), lambda g, c: (d, 0, 0)),
            pl.BlockSpec((1, 1, SSM_N), lambda g, c: (d, 0, 0)),
            pl.BlockSpec((1, 2, D_SSM // 2, SSM_N), lambda g, c: (d, 0, 0, 0)),
            pl.BlockSpec((1, 2, SSM_N // 2, D_SSM // 2), lambda g, c: (d, 0, 0, 0)),
            pl.BlockSpec((1, 2, SSM_N // 2, D_SSM // 2), lambda g, c: (d, 0, 0, 0)),
            pl.BlockSpec((SUBLANES, SSM_N), lambda g, c: (g, 0)),
            pl.BlockSpec((SUBLANES, SSM_N), lambda g, c: (g, 0)),
        ],
        out_specs=[
            pl.BlockSpec((S5_CHUNK, SUBLANES, D_SSM), lambda g, c: (cidx(g, c), g, 0)),
            pl.BlockSpec((SUBLANES, SSM_N), lambda g, c: (g, 0)),
            pl.BlockSpec((SUBLANES, SSM_N), lambda g, c: (g, 0)),
        ],
        out_shape=[jax.ShapeDtypeStruct((L, B, D_SSM), F32), state, state],
        scratch_shapes=[
            pltpu.VMEM((rows, SSM_N), F32),
            pltpu.VMEM((rows, SSM_N), F32),
            pltpu.VMEM((SUBLANES, SSM_N), F32),
            pltpu.VMEM((SUBLANES, SSM_N), F32),
        ],
        compiler_params=_params(("parallel", "arbitrary")),
        name="s5_bwd" if reverse else "s5_fwd",
    )(zs, a_re, a_im, wb, wc_re, wc_im, h0r, h0i)


def _pool_tables(grid):
    seg = GRID_W if grid else SEQ_TILE
    t = np.arange(SEQ_TILE)
    pos = t % seg
    base = t - pos
    mats, invs = [], []
    for w in POOL_WINDOWS:
        lo = np.clip(pos - w // 2, 0, seg)
        hi = np.clip(pos + (w - w // 2), 0, seg)
        s = t[None, :]
        m = (s >= (base + lo)[:, None]) & (s < (base + hi)[:, None])
        mats.append(m.astype(np.float32))
        invs.append(np.broadcast_to((1.0 / (hi - lo).astype(np.float32))[:, None], (SEQ_TILE, LANES)))
    return jnp.asarray(np.stack(mats), dtype=BF16), jnp.asarray(np.stack(invs), dtype=F32)


def _mix_kernel(x_ref, u_ref, zp_ref, yf_ref, yb_ref, g1_ref, sh2_ref, sc2_ref, d_ref, wglu_ref,
                pm_ref, pinv_ref, wpool_ref, pscale_ref, wout_ref, n2_ref, x1_ref, h2t_ref):
    x = x_ref[0]
    y = d_ref[...] * u_ref[...] + yf_ref[...] + yb_ref[...]
    g = _gelu(y)
    ys = g * jax.nn.sigmoid(_dot(g.astype(BF16), wglu_ref[...]))

    zp = zp_ref[0]
    parts = [ys]
    for gi in range(POOL_G):
        zg = zp[:, gi * POOL_C:(gi + 1) * POOL_C]
        hi, lo = _split_bf16(zg)
        pm = pm_ref[gi]
        win = _dot(pm, hi) + _dot(pm, lo)
        pooled = win * pinv_ref[gi] - zg
        og = _dot(pooled.astype(BF16), wpool_ref[gi])
        parts.append(og * pscale_ref[:, gi * POOL_C:(gi + 1) * POOL_C])
    mix = _dot(jnp.concatenate(parts, axis=-1).astype(BF16), wout_ref[...])

    x1 = x + g1_ref[0, 0] * mix
    x1_ref[...] = x1
    ms = jnp.mean(x1 * x1, axis=-1, keepdims=True)
    h2 = x1 * lax.rsqrt(ms + EPS) * n2_ref[...]
    h2 = h2 * (1.0 + sc2_ref[0, 0]) + sh2_ref[0, 0]
    h2t_ref[...] = h2.T.astype(BF16)


def _mix(x, zs2d, zp, yf2d, yb2d, mods, ssm_d, w_glu, w_pool, pool_scale, w_out, norm2_g, per_batch, grid):
    B, L, _ = x.shape
    tm = SEQ_TILE
    nt = L // tm
    pm, pinv = _pool_tables(grid)
    tm_spec = pl.BlockSpec((tm, D_SSM), lambda b, i: (i, b))
    const2 = lambda b, i: (0, 0)
    const3 = lambda b, i: (0, 0, 0)
    return pl.pallas_call(
        _mix_kernel,
        grid=(B, nt),
        in_specs=[
            pl.BlockSpec((1, tm, D_MODEL), lambda b, i: (b, i, 0)),
            tm_spec,
            pl.BlockSpec((1, tm, D_POOL), lambda b, i: (b, i, 0)),
            tm_spec,
            tm_spec,
            _mod_spec(2, per_batch),
            _mod_spec(3, per_batch),
            _mod_spec(4, per_batch),
            pl.BlockSpec((1, D_SSM), const2),
            pl.BlockSpec((D_SSM, D_SSM), const2),
            pl.BlockSpec((POOL_G, tm, tm), const3),
            pl.BlockSpec((POOL_G, tm, LANES), const3),
            pl.BlockSpec((POOL_G, POOL_C, POOL_C), const3),
            pl.BlockSpec((1, D_POOL), const2),
            pl.BlockSpec((D_MODEL, D_MODEL), const2),
            pl.BlockSpec((1, D_MODEL), const2),
        ],
        out_specs=[
            pl.BlockSpec((tm, D_MODEL), lambda b, i: (b * nt + i, 0)),
            pl.BlockSpec((D_MODEL, tm), lambda b, i: (0, b * nt + i)),
        ],
        out_shape=[
            jax.ShapeDtypeStruct((B * L, D_MODEL), F32),
            jax.ShapeDtypeStruct((D_MODEL, B * L), BF16),
        ],
        compiler_params=_params(("parallel", "parallel")),
        name="mix",
    )(x, zs2d, zp, yf2d, yb2d, mods, mods, mods, ssm_d.reshape(1, D_SSM), w_glu.astype(BF16),
      pm, pinv, w_pool.astype(BF16), pool_scale.reshape(1, D_POOL), w_out.astype(BF16),
      norm2_g.reshape(1, D_MODEL))


def _sort_pairs(n):
    pairs = []

    def merge(lo, hi, r):
        step = r * 2
        if step < hi - lo:
            merge(lo, hi, step)
            merge(lo + r, hi, step)
            for i in range(lo + r, hi - r, step):
                pairs.append((i, i + r))
        else:
            pairs.append((lo, lo + r))

    def sort(lo, hi):
        if hi - lo >= 1:
            mid = lo + (hi - lo) // 2
            sort(lo, mid)
            sort(mid + 1, hi)
            merge(lo, hi, 1)

    sort(0, n - 1)
    return pairs


_SORT16 = _sort_pairs(PEER_TOPK)


def _vmax(a, b):
    if a is None:
        return b
    if b is None:
        return a
    return jnp.maximum(a, b)


def _vmin(a, b):
    if a is None or b is None:
        return None
    return jnp.minimum(a, b)


def _sort16_desc(w):
    w = list(w)
    for i, j in _SORT16:
        w[i], w[j] = _vmax(w[i], w[j]), _vmin(w[i], w[j])
    return w


def _merge_top16(a, b):
    k = PEER_TOPK
    w = [_vmax(a[i], b[k - 1 - i]) for i in range(k)]
    d = k // 2
    while d >= 1:
        for i in range(k):
            if i & d == 0:
                w[i], w[i + d] = _vmax(w[i], w[i + d]), _vmin(w[i], w[i + d])
        d //= 2
    return w


def _top16_of_128(load):
    cur = None
    for grp in range(PEER_NKEYS // PEER_TOPK):
        s = _sort16_desc([load(grp * PEER_TOPK + i) for i in range(PEER_TOPK)])
        cur = s if cur is None else _merge_top16(cur, s)
    return cur


def _top16_pair_sums(sv0, sv1):
    k = PEER_TOPK
    pad = lambda row: row + [None] * (k - len(row))
    cur = [sv0[0] + sv1[j] for j in range(k)]
    for i in range(1, k // 2):
        cur = _merge_top16(cur, pad([sv0[i] + sv1[j] for j in range(k // (i + 1))]))
    return _merge_top16(cur, pad([sv0[i] + sv1[0] for i in range(k // 2, k)]))


def _count_sorted(v, x, strict):
    above = (lambda p: p > x) if strict else (lambda p: p >= x)
    sel = jnp.where
    g8 = above(v[7])
    g4 = above(sel(g8, v[11], v[3]))
    g2 = above(sel(g8, sel(g4, v[13], v[9]), sel(g4, v[5], v[1])))
    g1 = above(sel(g8, sel(g4, sel(g2, v[14], v[12]), sel(g2, v[10], v[8])),
                   sel(g4, sel(g2, v[6], v[4]), sel(g2, v[2], v[0]))))
    low = sel(g8, 8.0, 0.0) + sel(g4, 4.0, 0.0) + sel(g2, 2.0, 0.0) + sel(g1, 1.0, 0.0)
    return sel(above(v[15]), 16.0, low)


def _as_words(x):
    return pltpu.bitcast(x.astype(BF16), jnp.uint32)


def _as_bf16(words):
    return pltpu.bitcast(words, BF16)


def _peer_prep_kernel(ht_ref, wq_ref, wka_ref, kb_ref, a_ref, m_ref, b_ref, r_ref, q_ref, sa_ref, st_ref):
    tm = ht_ref.shape[1]
    half = PEER_HEADS * PEER_DHALF
    q_ref[...] = _dot(wq_ref[...], ht_ref[...]).astype(BF16)
    for side in range(2):
        s = _dot(wka_ref[side], q_ref[side * half:(side + 1) * half, :])
        sa_ref[side] = s.reshape(PEER_NKEYS, PEER_HEADS, tm)

    for lc in range(tm // LANES):
        ls = slice(lc * LANES, (lc + 1) * LANES)
        sv0 = _top16_of_128(lambda n: sa_ref[0, n, :, ls])
        sv1 = _top16_of_128(lambda n: sa_ref[1, n, :, ls])
        top = _top16_pair_sums(sv0, sv1)
        tau = top[PEER_TOPK - 1]
        best = sv0[0] + sv1[0]
        z = jnp.exp(top[0] - best)
        for k in range(1, PEER_TOPK):
            z = z + jnp.exp(top[k] - best)
        inv_z = 1.0 / z
        slack = (jnp.abs(tau) + jnp.abs(sv0[0]) + jnp.abs(sv0[PEER_TOPK - 1])
                 + jnp.abs(sv1[0]) + jnp.abs(sv1[PEER_TOPK - 1])) * (2.0 ** -21) + 1e-30
        cut = tau - slack
        best0 = sv0[0]
        for k in range(PEER_TOPK):
            st_ref[k, :, ls] = sv1[k]

        def first_key(n, carry, lc=lc, ls=ls, sv1=sv1, cut=cut, best0=best0, inv_z=inv_z):
            s0 = sa_ref[0, n, :, ls]
            a_ref[lc, n] = jnp.exp(s0 - best0) * inv_z
            m_ref[lc, n] = _count_sorted(sv1, cut - s0, strict=False)
            return carry

        lax.fori_loop(0, PEER_NKEYS, first_key, 0, unroll=2)

    for h in range(PEER_HEADS):
        s1 = _dot(kb_ref[h], q_ref[half + h * PEER_DHALF:half + (h + 1) * PEER_DHALF, :])
        for lc in range(tm // LANES):
            ls = slice(lc * LANES, (lc + 1) * LANES)
            sorted1 = [st_ref[k, h:h + 1, ls] for k in range(PEER_TOPK)]
            for part in range(PEER_NKEYS // RANK_ROWS):
                x = s1[part * RANK_ROWS:(part + 1) * RANK_ROWS, ls]
                words = slice(part * RANK_ROWS // 2, (part + 1) * RANK_ROWS // 2)
                b_ref[lc, h, words, :] = _as_words(jnp.exp(x - sorted1[0]))
                r_ref[lc, h, words, :] = _as_words(_count_sorted(sorted1, x, strict=True))


def _peer_tables(peer_wq, sub_keys):
    wq = peer_wq.reshape(D_MODEL, PEER_HEADS, 2, PEER_DHALF)
    wq_t = jnp.transpose(wq, (2, 1, 3, 0)).reshape(2 * PEER_HEADS * PEER_DHALF, D_MODEL).astype(BF16)
    eye = jnp.eye(PEER_HEADS, dtype=F32)
    wka = jnp.einsum("hsnk,hg->snhgk", sub_keys, eye)
    wka = wka.reshape(2, PEER_NKEYS * PEER_HEADS, PEER_HEADS * PEER_DHALF).astype(BF16)
    kb = sub_keys[:, 1].astype(BF16)
    return wq_t, wka, kb


def _peer_prep(h2t, wq_t, wka, kb):
    T = h2t.shape[1]
    tm = PREP_TILE
    half = PEER_HEADS * PEER_DHALF
    first = jax.ShapeDtypeStruct((T // LANES, PEER_NKEYS, PEER_HEADS, LANES), F32)
    second = jax.ShapeDtypeStruct((T // LANES, PEER_HEADS, PEER_NKEYS // 2, LANES), jnp.uint32)
    fspec = pl.BlockSpec((tm // LANES, PEER_NKEYS, PEER_HEADS, LANES), lambda i: (i, 0, 0, 0))
    sspec = pl.BlockSpec((tm // LANES, PEER_HEADS, PEER_NKEYS // 2, LANES), lambda i: (i, 0, 0, 0))
    return pl.pallas_call(
        _peer_prep_kernel,
        grid=(T // tm,),
        in_specs=[
            pl.BlockSpec((D_MODEL, tm), lambda i: (0, i)),
            pl.BlockSpec((2 * half, D_MODEL), lambda i: (0, 0)),
            pl.BlockSpec((2, PEER_NKEYS * PEER_HEADS, half), lambda i: (0, 0, 0)),
            pl.BlockSpec((PEER_HEADS, PEER_NKEYS, PEER_DHALF), lambda i: (0, 0, 0)),
        ],
        out_specs=[fspec, fspec, sspec, sspec],
        out_shape=[first, first, second, second],
        scratch_shapes=[
            pltpu.VMEM((2 * half, tm), BF16),
            pltpu.VMEM((2, PEER_NKEYS, PEER_HEADS, tm), F32),
            pltpu.VMEM((PEER_TOPK, PEER_HEADS, tm), F32),
        ],
        compiler_params=_params(("parallel",)),
        name="peer_prep",
    )(h2t, wq_t, wka, kb)


def _peer_dense_kernel(ht_ref, a_ref, m_ref, b_ref, r_ref, x1_ref, g2_ref, fg_ref, u_hbm, vt_hbm,
                       y_ref, ubuf, vbuf, usem, vsem, acc_ref, act0_ref, act1_ref, w0_ref, w1_ref):
    act_refs = (act0_ref, act1_ref)
    w_refs = (w0_ref, w1_ref)
    tm = ht_ref.shape[1]
    n_chunks = tm // LANES
    n_sub = PEER_N // PEER_SB
    n_pairs = n_sub // 2
    keys_per_sub = PEER_SB // PEER_NKEYS
    sub_words = PEER_SB // 2
    half_rows = D_MODEL // 2

    def u_copy(s, slot):
        return pltpu.make_async_copy(u_hbm.at[pl.ds(s * PEER_SB, PEER_SB), :], ubuf.at[slot], usem.at[slot])

    def v_copy(pair, slot):
        return pltpu.make_async_copy(vt_hbm.at[pair], vbuf.at[slot], vsem.at[slot])

    def activations(uslot):
        act = _dot(ubuf[uslot], ht_ref[...])
        for tc in range(n_chunks):
            act_refs[uslot % 2][tc] = act[:, tc * LANES:(tc + 1) * LANES]

    def gated(s, aslot, wslot, pos):
        for tc in range(n_chunks):
            a_rows = [a_ref[tc, s * keys_per_sub + k] for k in range(keys_per_sub)]
            m_rows = [m_ref[tc, s * keys_per_sub + k] for k in range(keys_per_sub)]
            for part in range(PEER_NKEYS // GATE_ROWS):
                pwords = slice(part * GATE_ROWS // 2, (part + 1) * GATE_ROWS // 2)
                gates = [None] * keys_per_sub
                for h in range(PEER_HEADS):
                    b = _as_bf16(b_ref[tc, h, pwords, :])
                    r = _as_bf16(r_ref[tc, h, pwords, :])
                    for k in range(keys_per_sub):
                        a_row = jnp.broadcast_to(a_rows[k][h:h + 1, :], (GATE_ROWS, LANES)).astype(BF16)
                        m_row = jnp.broadcast_to(m_rows[k][h:h + 1, :], (GATE_ROWS, LANES)).astype(BF16)
                        term = jnp.where(r < m_row, b, jnp.zeros_like(b)) * a_row
                        gates[k] = term if gates[k] is None else gates[k] + term
                for k in range(keys_per_sub):
                    row0 = k * PEER_NKEYS + part * GATE_ROWS
                    g = _gelu(act_refs[aslot][tc, row0:row0 + GATE_ROWS, :])
                    word0 = pos * sub_words + row0 // 2
                    w_refs[wslot][tc, word0:word0 + GATE_ROWS // 2, :] = pltpu.bitcast(
                        g.astype(BF16) * gates[k], jnp.uint32)

    def accumulate(vslot, wslot, half):
        w = _as_bf16(jnp.concatenate([w_refs[wslot][tc] for tc in range(n_chunks)], axis=1))
        rows = slice(half * half_rows, (half + 1) * half_rows)
        acc_ref[rows, :] += _dot(vbuf[vslot, rows, :], w)

    def stage_dma(s, jj):
        static = isinstance(s, int)

        def when(cond, fn):
            if static:
                if cond:
                    fn()
            else:
                pl.when(cond)(fn)

        when(s + 2 < n_sub, lambda: u_copy(s + 2, (jj + 3) % DMA_SLOTS).start())
        if jj % 2 == 0:
            when(s < n_sub, lambda: v_copy((s - 1) // 2, (jj // 2) % 2).start())
        when(s < n_sub, lambda: u_copy(s, (jj + 1) % DMA_SLOTS).wait())
        if jj % 2 == 0:
            when(s >= 3, lambda: v_copy((s - 3) // 2, (jj // 2 + 1) % 2).wait())

    def stage_compute(s, jj, with_activations=True, with_gate=True):
        if with_activations:
            activations((jj + 1) % DMA_SLOTS)
        if with_gate:
            gated(s - 1, jj % 2, (jj // 2) % 2, jj % 2)
        accumulate((jj // 2 + 1) % 2, (jj // 2 + 1) % 2, jj % 2)

    acc_ref[...] = jnp.zeros_like(acc_ref)
    w_refs[1][...] = jnp.zeros_like(w_refs[1])
    vbuf[1] = jnp.zeros(vbuf.shape[1:], vbuf.dtype)
    u_copy(0, 0).start()
    u_copy(1, 1).start()
    u_copy(2, 2).start()
    u_copy(0, 0).wait()
    activations(0)

    def stages(t, carry):
        for jj in range(DMA_SLOTS):
            s = 1 + t * DMA_SLOTS + jj
            stage_dma(s, jj)
            stage_compute(s, jj)
        return carry

    lax.fori_loop(0, n_sub // DMA_SLOTS, stages, 0)
    for s in (n_sub + 1, n_sub + 2):
        jj = (s - 1) % DMA_SLOTS
        stage_dma(s, jj)
        stage_compute(s, jj, with_activations=False, with_gate=False)


    out = x1_ref[...] + g2_ref[0, 0] * acc_ref[...].T
    ms = jnp.mean(out * out, axis=-1, keepdims=True)
    y_ref[...] = out * lax.rsqrt(ms + EPS) * fg_ref[...]


def _peer_dense(h2t, u_bf, vt_sub, a, m, b, r, x1, mods, final_g, tokens_per_row):
    T = h2t.shape[1]
    tm = PEER_TM
    if tokens_per_row is None:
        g2_map = lambda i: (CTX_ROW, 5, 0, 0)
    else:
        g2_map = lambda i: ((i * tm) // tokens_per_row, 5, 0, 0)
    fspec = pl.BlockSpec((tm // LANES, PEER_NKEYS, PEER_HEADS, LANES), lambda i: (i, 0, 0, 0))
    sspec = pl.BlockSpec((tm // LANES, PEER_HEADS, PEER_NKEYS // 2, LANES), lambda i: (i, 0, 0, 0))
    return pl.pallas_call(
        _peer_dense_kernel,
        grid=(T // tm,),
        in_specs=[
            pl.BlockSpec((D_MODEL, tm), lambda i: (0, i)),
            fspec, fspec, sspec, sspec,
            pl.BlockSpec((tm, D_MODEL), lambda i: (i, 0)),
            pl.BlockSpec((1, 1, 1, D_MODEL), g2_map),
            pl.BlockSpec((1, D_MODEL), lambda i: (0, 0)),
            pl.BlockSpec(memory_space=pl.ANY),
            pl.BlockSpec(memory_space=pl.ANY),
        ],
        out_specs=pl.BlockSpec((tm, D_MODEL), lambda i: (i, 0)),
        out_shape=jax.ShapeDtypeStruct((T, D_MODEL), F32),
        scratch_shapes=[
            pltpu.VMEM((DMA_SLOTS, PEER_SB, D_MODEL), BF16),
            pltpu.VMEM((2, D_MODEL, 2 * PEER_SB), BF16),
            pltpu.SemaphoreType.DMA((DMA_SLOTS,)),
            pltpu.SemaphoreType.DMA((2,)),
            pltpu.VMEM((D_MODEL, tm), F32),
            pltpu.VMEM((tm // LANES, PEER_SB, LANES), F32),
            pltpu.VMEM((tm // LANES, PEER_SB, LANES), F32),
            pltpu.VMEM((tm // LANES, PEER_SB, LANES), jnp.uint32),
            pltpu.VMEM((tm // LANES, PEER_SB, LANES), jnp.uint32),
        ],
        compiler_params=_params(("parallel",)),
        name="peer_dense",
    )(h2t, a, m, b, r, x1, mods, final_g.reshape(1, D_MODEL), u_bf, vt_sub)


def _trunk(x, mods, h0r, h0i, s5w, p, tables, final_g, per_batch, grid):
    B, L, _ = x.shape
    a_re, a_im, wb, wc_re, wc_im = s5w
    zs2d, zp = _inproj(x, mods, p["norm1_g"], p["w_in"], per_batch)
    zs = zs2d.reshape(L, B, D_SSM)
    yf, sfr, sfi = _s5_direction(zs, a_re, a_im, wb, wc_re, wc_im, h0r[:, 0], h0i[:, 0], 0)
    yb, sbr, sbi = _s5_direction(zs, a_re, a_im, wb, wc_re, wc_im, h0r[:, 1], h0i[:, 1], 1)
    x1, h2t = _mix(x, zs2d, zp, yf.reshape(L, B * D_SSM), yb.reshape(L, B * D_SSM), mods, p["ssm_d"],
                   p["w_glu"], p["w_pool"], p["pool_scale"], p["w_out"], p["norm2_g"], per_batch, grid)
    wq_t, wka, kb, u_bf, vt_bf = tables
    a, m, b, r = _peer_prep(h2t, wq_t, wka, kb)
    y = _peer_dense(h2t, u_bf, vt_bf, a, m, b, r, x1, mods, final_g, L if per_batch else None)
    new_re = jnp.stack([sfr, sbr], axis=1).reshape(B, 2, SSM_G, SSM_P)
    new_im = jnp.stack([sfi, sbi], axis=1).reshape(B, 2, SSM_G, SSM_P)
    return y.reshape(B, L, D_MODEL), new_re, new_im


def kernel(x_prompt, x_sample, state_ssm_re, state_ssm_im, c, c_ctx, norm1_g, w_mod, b_mod, w_in,
           ssm_lambda_re, ssm_lambda_im, ssm_log_dt, ssm_b_re, ssm_b_im, ssm_c_re, ssm_c_im, ssm_d,
           w_glu, w_pool, pool_scale, w_out, norm2_g, peer_wq, peer_subkeys, peer_u, peer_v, final_g):
    depth = w_mod.shape[0]
    assert depth == 1, "single trunk layer"
    l = 0
    n_dec = c.shape[0]
    cond = jnp.zeros((COND_ROWS, D_MODEL), F32).at[:n_dec].set(c).at[CTX_ROW].set(c_ctx)
    mods = _mod_vectors(cond, w_mod[l], b_mod[l]).reshape(COND_ROWS, N_MOD, 1, D_MODEL)

    ar, ai, bbr, bbi = _discretise(ssm_lambda_re[l], ssm_lambda_im[l], ssm_log_dt[l], ssm_b_re[l], ssm_b_im[l])
    s5w = _s5_weights(ar, ai, bbr, bbi, ssm_c_re[l], ssm_c_im[l])

    wq_t, wka, kb = _peer_tables(peer_wq[l], peer_subkeys[l])
    vt_sub = jnp.transpose(peer_v[l].reshape(PEER_N // (2 * PEER_SB), 2 * PEER_SB, D_MODEL), (0, 2, 1)).astype(BF16)
    tables = (wq_t, wka, kb, peer_u[l].astype(BF16), vt_sub)

    p = {"norm1_g": norm1_g[l], "w_in": w_in[l].astype(BF16), "ssm_d": ssm_d[l], "w_glu": w_glu[l],
         "w_pool": w_pool[l], "pool_scale": pool_scale[l], "w_out": w_out[l], "norm2_g": norm2_g[l]}

    bp = x_prompt.shape[0]
    zeros = jnp.zeros((bp, 2, SSM_N), F32)
    y_prompt, new_re, new_im = _trunk(x_prompt, mods, zeros, zeros, s5w, p, tables, final_g, False, False)
    h0r = state_ssm_re[:, l].reshape(n_dec, 2, SSM_N)
    h0i = state_ssm_im[:, l].reshape(n_dec, 2, SSM_N)
    y_sample, _, _ = _trunk(x_sample, mods, h0r, h0i, s5w, p, tables, final_g, True, True)
    return (y_prompt, y_sample, new_re[:, None], new_im[:, None])
```

```python
import functools
import math

import numpy as np
import jax
import jax.numpy as jnp
from jax import lax
from jax.experimental import pallas as pl
from jax.experimental.pallas import tpu as pltpu

F32 = jnp.float32
BF16 = jnp.bfloat16

D_MODEL = 1024
D_SSM = 512
D_POOL = 512
SSM_H = 16
SSM_G = 32
SSM_P = 64
SSM_N = SSM_G * SSM_P
POOL_WINDOWS = (2, 4, 8, 16)
POOL_G = 4
POOL_C = 128
GRID_W = 64
PEER_HEADS = 8
PEER_NKEYS = 128
PEER_N = PEER_NKEYS * PEER_NKEYS
PEER_DHALF = 128
PEER_TOPK = 16
N_MOD = 6
EPS = 1e-6

SUBLANES = 8
LANES = 128
VMEM_LIMIT = 56 * 1024 * 1024

COND_ROWS = 16
CTX_ROW = 8
SEQ_TILE = 256
S5_CHUNK = 64
S5_LANES = 512
PREP_TILE = 256
PEER_TM = 512
PEER_SB = 256
STAGE_UNROLL = 4
U_SLOTS = 6
U_AHEAD = 4
V_SLOTS = 3
RANK_ROWS = 64
GATE_ROWS = 64


def _gelu(x):
    return 0.5 * x * (1.0 + jnp.tanh(0.7978845608028654 * (x + 0.044715 * (x * x * x))))


def _split_bf16(x):
    hi = x.astype(BF16)
    lo = (x - hi.astype(F32)).astype(BF16)
    return hi, lo


def _dot(a, b):
    return jnp.dot(a, b, preferred_element_type=F32)


def _params(sem, flags=None):
    return pltpu.CompilerParams(dimension_semantics=sem, vmem_limit_bytes=VMEM_LIMIT, flags=flags)


def _mod_kernel(cond_ref, whi_ref, wlo_ref, b_ref, o_ref):
    c = cond_ref[...]
    s = c * jax.nn.sigmoid(c)
    shi, slo = _split_bf16(s)
    whi = whi_ref[...]
    o_ref[...] = _dot(shi, whi) + _dot(slo, whi) + _dot(shi, wlo_ref[...]) + b_ref[...]


def _mod_vectors(cond, w_mod, b_mod):
    whi, wlo = _split_bf16(w_mod)
    n = w_mod.shape[1]
    bn = D_MODEL
    return pl.pallas_call(
        _mod_kernel,
        grid=(n // bn,),
        in_specs=[
            pl.BlockSpec((COND_ROWS, D_MODEL), lambda k: (0, 0)),
            pl.BlockSpec((D_MODEL, bn), lambda k: (0, k)),
            pl.BlockSpec((D_MODEL, bn), lambda k: (0, k)),
            pl.BlockSpec((1, bn), lambda k: (0, k)),
        ],
        out_specs=pl.BlockSpec((COND_ROWS, bn), lambda k: (0, k)),
        out_shape=jax.ShapeDtypeStruct((COND_ROWS, n), F32),
        compiler_params=_params(("arbitrary",)),
        name="mod",
    )(cond, whi, wlo, b_mod.reshape(1, n))


def _disc_kernel(lr_ref, li_ref, ldt_ref, br_ref, bi_ref, ar_ref, ai_ref, bbr_ref, bbi_ref):
    lr = lr_ref[...]
    li = li_ref[...]
    dt = jnp.exp(ldt_ref[...])
    mag = jnp.exp(lr * dt)
    ar = mag * jnp.cos(li * dt)
    ai = mag * jnp.sin(li * dt)
    den = lr * lr + li * li
    nr = ar - 1.0
    ni = ai
    fr = (nr * lr + ni * li) / den
    fi = (ni * lr - nr * li) / den
    ar_ref[...] = ar
    ai_ref[...] = ai
    br = br_ref[...]
    bi = bi_ref[...]
    frb = fr[:, None, :]
    fib = fi[:, None, :]
    bbr_ref[...] = frb * br - fib * bi
    bbi_ref[...] = frb * bi + fib * br


def _discretise(lam_re, lam_im, log_dt, b_re, b_im):
    rows = 2 * SSM_G
    lr = lam_re.reshape(rows, SSM_P)
    li = lam_im.reshape(rows, SSM_P)
    ldt = jnp.broadcast_to(log_dt.reshape(rows, 1), (rows, SSM_P))
    br = jnp.swapaxes(b_re, -1, -2).reshape(rows, SSM_H, SSM_P)
    bi = jnp.swapaxes(b_im, -1, -2).reshape(rows, SSM_H, SSM_P)
    small = jax.ShapeDtypeStruct((rows, SSM_P), F32)
    big = jax.ShapeDtypeStruct((rows, SSM_H, SSM_P), F32)
    return pl.pallas_call(_disc_kernel, out_shape=(small, small, big, big), name="disc")(lr, li, ldt, br, bi)


def _s5_weights(ar, ai, bbr, bbi, c_re, c_im):
    half_g = SSM_G // 2
    eye = jnp.eye(half_g, dtype=F32)
    a_re = ar.reshape(2, 1, SSM_N)
    a_im = ai.reshape(2, 1, SSM_N)

    def bmat(t):
        t = t.reshape(2, 2, half_g, SSM_H, SSM_P)
        m = jnp.einsum("djghp,gk->djghkp", t, eye)
        return m.reshape(2, 2, half_g * SSM_H, half_g * SSM_P)

    def cmat(t):
        t = t.reshape(2, 2, half_g, SSM_H, SSM_P)
        m = jnp.einsum("djghp,gk->djgpkh", t, eye)
        return m.reshape(2, 2, half_g * SSM_P, half_g * SSM_H)

    wb = jnp.concatenate([bmat(bbr), bmat(bbi)], axis=-1).astype(BF16)
    wc_re = cmat(c_re).astype(BF16)
    wc_im = cmat(c_im).astype(BF16)
    return a_re, a_im, wb, wc_re, wc_im


def _inproj_kernel(x_ref, sh_ref, sc_ref, g_ref, w_ref, zs_ref, zp_ref):
    x = x_ref[0]
    ms = jnp.mean(x * x, axis=-1, keepdims=True)
    y = x * lax.rsqrt(ms + EPS) * g_ref[...]
    h = y * (1.0 + sc_ref[0, 0]) + sh_ref[0, 0]
    z = _dot(h.astype(BF16), w_ref[...])
    zs_ref[...] = z[:, :D_SSM]
    zp_ref[0] = z[:, D_SSM:]


def _mod_spec(k, per_batch, tokens_per_row=None):
    if per_batch:
        return pl.BlockSpec((1, 1, 1, D_MODEL), lambda b, i: (b, k, 0, 0))
    return pl.BlockSpec((1, 1, 1, D_MODEL), lambda b, i: (CTX_ROW, k, 0, 0))


def _inproj(x, mods, g, w_in, per_batch):
    B, L, _ = x.shape
    tm = SEQ_TILE
    return pl.pallas_call(
        _inproj_kernel,
        grid=(B, L // tm),
        in_specs=[
            pl.BlockSpec((1, tm, D_MODEL), lambda b, i: (b, i, 0)),
            _mod_spec(0, per_batch),
            _mod_spec(1, per_batch),
            pl.BlockSpec((1, D_MODEL), lambda b, i: (0, 0)),
            pl.BlockSpec((D_MODEL, D_MODEL), lambda b, i: (0, 0)),
        ],
        out_specs=[
            pl.BlockSpec((tm, D_SSM), lambda b, i: (i, b)),
            pl.BlockSpec((1, tm, D_POOL), lambda b, i: (b, i, 0)),
        ],
        out_shape=[
            jax.ShapeDtypeStruct((L, B * D_SSM), F32),
            jax.ShapeDtypeStruct((B, L, D_POOL), F32),
        ],
        compiler_params=_params(("parallel", "parallel")),
        name="inproj",
    )(x, mods, mods, g.reshape(1, D_MODEL), w_in)


def _s5_kernel(u_ref, are_ref, aim_ref, wb_ref, wcr_ref, wci_ref, h0r_ref, h0i_ref,
               y_ref, sr_ref, si_ref, bur_ref, bui_ref, cr_ref, ci_ref, *, reverse, n_chunks):
    c = pl.program_id(1)
    rows = S5_CHUNK * SUBLANES
    half = SSM_N // 2
    half_in = D_SSM // 2

    @pl.when(c == 0)
    def _():
        cr_ref[...] = h0r_ref[...]
        ci_ref[...] = h0i_ref[...]

    u = u_ref[...].reshape(rows, D_SSM).astype(BF16)
    for j in range(2):
        r = _dot(u[:, j * half_in:(j + 1) * half_in], wb_ref[0, j])
        bur_ref[:, j * half:(j + 1) * half] = r[:, :half]
        bui_ref[:, j * half:(j + 1) * half] = r[:, half:]

    for lc in range(SSM_N // S5_LANES):
        ls = slice(lc * S5_LANES, (lc + 1) * S5_LANES)
        ar = jnp.broadcast_to(are_ref[0, :, ls], (SUBLANES, S5_LANES))
        ai = jnp.broadcast_to(aim_ref[0, :, ls], (SUBLANES, S5_LANES))

        def step(i, carry, ls=ls, ar=ar, ai=ai):
            hr, hi = carry
            t = (S5_CHUNK - 1 - i) if reverse else i
            row = pl.multiple_of(t * SUBLANES, SUBLANES)
            nhr = ar * hr - ai * hi + bur_ref[pl.ds(row, SUBLANES), ls]
            nhi = ar * hi + ai * hr + bui_ref[pl.ds(row, SUBLANES), ls]
            bur_ref[pl.ds(row, SUBLANES), ls] = nhr
            bui_ref[pl.ds(row, SUBLANES), ls] = nhi
            return nhr, nhi

        hr, hi = lax.fori_loop(0, S5_CHUNK, step, (cr_ref[:, ls], ci_ref[:, ls]), unroll=4)
        cr_ref[:, ls] = hr
        ci_ref[:, ls] = hi

    ys = []
    for j in range(2):
        hs = slice(j * half, (j + 1) * half)
        yr = _dot(bur_ref[:, hs].astype(BF16), wcr_ref[0, j])
        yi = _dot(bui_ref[:, hs].astype(BF16), wci_ref[0, j])
        ys.append(yr - yi)
    y_ref[...] = jnp.concatenate(ys, axis=-1).reshape(S5_CHUNK, SUBLANES, D_SSM)

    @pl.when(c == n_chunks - 1)
    def _():
        sr_ref[...] = cr_ref[...]
        si_ref[...] = ci_ref[...]


def _s5_direction(zs, a_re, a_im, wb, wc_re, wc_im, h0r, h0i, d):
    L, B, _ = zs.shape
    n_chunks = L // S5_CHUNK
    reverse = d == 1
    cidx = (lambda g, c: n_chunks - 1 - c) if reverse else (lambda g, c: c)
    rows = S5_CHUNK * SUBLANES
    kern = functools.partial(_s5_kernel, reverse=reverse, n_chunks=n_chunks)
    state = jax.ShapeDtypeStruct((B, SSM_N), F32)
    return pl.pallas_call(
        kern,
        grid=(B // SUBLANES, n_chunks),
        in_specs=[
            pl.BlockSpec((S5_CHUNK, SUBLANES, D_SSM), lambda g, c: (cidx(g, c), g, 0)),
            pl.BlockSpec((1, 1, SSM_N), lambda g, c: (d, 0, 0)),
            pl.BlockSpec((1, 1, SSM_N), lambda g, c: (d, 0, 0)),
            pl.BlockSpec((1, 2, D_SSM // 2, SSM_N), lambda g, c: (d, 0, 0, 0)),
            pl.BlockSpec((1, 2, SSM_N // 2, D_SSM // 2), lambda g, c: (d, 0, 0, 0)),
            pl.BlockSpec((1, 2, SSM_N // 2, D_SSM // 2), lambda g, c: (d, 0, 0, 0)),
            pl.BlockSpec((SUBLANES, SSM_N), lambda g, c: (g, 0)),
            pl.BlockSpec((SUBLANES, SSM_N), lambda g, c: (g, 0)),
        ],
        out_specs=[
            pl.BlockSpec((S5_CHUNK, SUBLANES, D_SSM), lambda g, c: (cidx(g, c), g, 0)),
            pl.BlockSpec((SUBLANES, SSM_N), lambda g, c: (g, 0)),
            pl.BlockSpec((SUBLANES, SSM_N), lambda g, c: (g, 0)),
        ],
        out_shape=[jax.ShapeDtypeStruct((L, B, D_SSM), F32), state, state],
        scratch_shapes=[
            pltpu.VMEM((rows, SSM_N), F32),
            pltpu.VMEM((rows, SSM_N), F32),
            pltpu.VMEM((SUBLANES, SSM_N), F32),
            pltpu.VMEM((SUBLANES, SSM_N), F32),
        ],
        compiler_params=_params(("parallel", "arbitrary")),
        name="s5_bwd" if reverse else "s5_fwd",
    )(zs, a_re, a_im, wb, wc_re, wc_im, h0r, h0i)


def _pool_tables(grid):
    seg = GRID_W if grid else SEQ_TILE
    t = np.arange(SEQ_TILE)
    pos = t % seg
    base = t - pos
    mats, invs = [], []
    for w in POOL_WINDOWS:
        lo = np.clip(pos - w // 2, 0, seg)
        hi = np.clip(pos + (w - w // 2), 0, seg)
        s = t[None, :]
        m = (s >= (base + lo)[:, None]) & (s < (base + hi)[:, None])
        mats.append(m.astype(np.float32))
        invs.append(np.broadcast_to((1.0 / (hi - lo).astype(np.float32))[:, None], (SEQ_TILE, LANES)))
    return jnp.asarray(np.stack(mats), dtype=BF16), jnp.asarray(np.stack(invs), dtype=F32)


def _mix_kernel(x_ref, u_ref, zp_ref, yf_ref, yb_ref, g1_ref, sh2_ref, sc2_ref, d_ref, wglu_ref,
                pm_ref, pinv_ref, wpool_ref, pscale_ref, wout_ref, n2_ref, x1_ref, h2t_ref):
    x = x_ref[0]
    y = d_ref[...] * u_ref[...] + yf_ref[...] + yb_ref[...]
    g = _gelu(y)
    ys = g * jax.nn.sigmoid(_dot(g.astype(BF16), wglu_ref[...]))

    zp = zp_ref[0]
    parts = [ys]
    for gi in range(POOL_G):
        zg = zp[:, gi * POOL_C:(gi + 1) * POOL_C]
        hi, lo = _split_bf16(zg)
        pm = pm_ref[gi]
        win = _dot(pm, hi) + _dot(pm, lo)
        pooled = win * pinv_ref[gi] - zg
        og = _dot(pooled.astype(BF16), wpool_ref[gi])
        parts.append(og * pscale_ref[:, gi * POOL_C:(gi + 1) * POOL_C])
    mix = _dot(jnp.concatenate(parts, axis=-1).astype(BF16), wout_ref[...])

    x1 = x + g1_ref[0, 0] * mix
    x1_ref[...] = x1
    ms = jnp.mean(x1 * x1, axis=-1, keepdims=True)
    h2 = x1 * lax.rsqrt(ms + EPS) * n2_ref[...]
    h2 = h2 * (1.0 + sc2_ref[0, 0]) + sh2_ref[0, 0]
    h2t_ref[...] = h2.T.astype(BF16)


def _mix(x, zs2d, zp, yf2d, yb2d, mods, ssm_d, w_glu, w_pool, pool_scale, w_out, norm2_g, per_batch, grid):
    B, L, _ = x.shape
    tm = SEQ_TILE
    nt = L // tm
    pm, pinv = _pool_tables(grid)
    tm_spec = pl.BlockSpec((tm, D_SSM), lambda b, i: (i, b))
    const2 = lambda b, i: (0, 0)
    const3 = lambda b, i: (0, 0, 0)
    return pl.pallas_call(
        _mix_kernel,
        grid=(B, nt),
        in_specs=[
            pl.BlockSpec((1, tm, D_MODEL), lambda b, i: (b, i, 0)),
            tm_spec,
            pl.BlockSpec((1, tm, D_POOL), lambda b, i: (b, i, 0)),
            tm_spec,
            tm_spec,
            _mod_spec(2, per_batch),
            _mod_spec(3, per_batch),
            _mod_spec(4, per_batch),
            pl.BlockSpec((1, D_SSM), const2),
            pl.BlockSpec((D_SSM, D_SSM), const2),
            pl.BlockSpec((POOL_G, tm, tm), const3),
            pl.BlockSpec((POOL_G, tm, LANES), const3),
            pl.BlockSpec((POOL_G, POOL_C, POOL_C), const3),
            pl.BlockSpec((1, D_POOL), const2),
            pl.BlockSpec((D_MODEL, D_MODEL), const2),
            pl.BlockSpec((1, D_MODEL), const2),
        ],
        out_specs=[
            pl.BlockSpec((tm, D_MODEL), lambda b, i: (b * nt + i, 0)),
            pl.BlockSpec((D_MODEL, tm), lambda b, i: (0, b * nt + i)),
        ],
        out_shape=[
            jax.ShapeDtypeStruct((B * L, D_MODEL), F32),
            jax.ShapeDtypeStruct((D_MODEL, B * L), BF16),
        ],
        compiler_params=_params(("parallel", "parallel")),
        name="mix",
    )(x, zs2d, zp, yf2d, yb2d, mods, mods, mods, ssm_d.reshape(1, D_SSM), w_glu.astype(BF16),
      pm, pinv, w_pool.astype(BF16), pool_scale.reshape(1, D_POOL), w_out.astype(BF16),
      norm2_g.reshape(1, D_MODEL))


def _sort_pairs(n):
    pairs = []

    def merge(lo, hi, r):
        step = r * 2
        if step < hi - lo:
            merge(lo, hi, step)
            merge(lo + r, hi, step)
            for i in range(lo + r, hi - r, step):
                pairs.append((i, i + r))
        else:
            pairs.append((lo, lo + r))

    def sort(lo, hi):
        if hi - lo >= 1:
            mid = lo + (hi - lo) // 2
            sort(lo, mid)
            sort(mid + 1, hi)
            merge(lo, hi, 1)

    sort(0, n - 1)
    return pairs


_SORT16 = _sort_pairs(PEER_TOPK)


def _vmax(a, b):
    if a is None:
        return b
    if b is None:
        return a
    return jnp.maximum(a, b)


def _vmin(a, b):
    if a is None or b is None:
        return None
    return jnp.minimum(a, b)


def _sort16_desc(w):
    w = list(w)
    for i, j in _SORT16:
        w[i], w[j] = _vmax(w[i], w[j]), _vmin(w[i], w[j])
    return w


def _merge_top16(a, b):
    k = PEER_TOPK
    w = [_vmax(a[i], b[k - 1 - i]) for i in range(k)]
    d = k // 2
    while d >= 1:
        for i in range(k):
            if i & d == 0:
                w[i], w[i + d] = _vmax(w[i], w[i + d]), _vmin(w[i], w[i + d])
        d //= 2
    return w


def _top16_of_128(load):
    cur = None
    for grp in range(PEER_NKEYS // PEER_TOPK):
        s = _sort16_desc([load(grp * PEER_TOPK + i) for i in range(PEER_TOPK)])
        cur = s if cur is None else _merge_top16(cur, s)
    return cur


def _top16_pair_sums(sv0, sv1):
    k = PEER_TOPK
    pad = lambda row: row + [None] * (k - len(row))
    cur = [sv0[0] + sv1[j] for j in range(k)]
    for i in range(1, k // 2):
        cur = _merge_top16(cur, pad([sv0[i] + sv1[j] for j in range(k // (i + 1))]))
    return _merge_top16(cur, pad([sv0[i] + sv1[0] for i in range(k // 2, k)]))


def _count_sorted(v, x, strict):
    above = (lambda p: p > x) if strict else (lambda p: p >= x)
    sel = jnp.where
    g8 = above(v[7])
    g4 = above(sel(g8, v[11], v[3]))
    g2 = above(sel(g8, sel(g4, v[13], v[9]), sel(g4, v[5], v[1])))
    g1 = above(sel(g8, sel(g4, sel(g2, v[14], v[12]), sel(g2, v[10], v[8])),
                   sel(g4, sel(g2, v[6], v[4]), sel(g2, v[2], v[0]))))
    low = sel(g8, 8.0, 0.0) + sel(g4, 4.0, 0.0) + sel(g2, 2.0, 0.0) + sel(g1, 1.0, 0.0)
    return sel(above(v[15]), 16.0, low)


def _as_words(x):
    return pltpu.bitcast(x.astype(BF16), jnp.uint32)


def _as_bf16(words):
    return pltpu.bitcast(words, BF16)


def _peer_prep_kernel(ht_ref, wq_ref, wka_ref, kb_ref, a_ref, m_ref, b_ref, r_ref, q_ref, sa_ref, st_ref):
    tm = ht_ref.shape[1]
    half = PEER_HEADS * PEER_DHALF
    q_ref[...] = _dot(wq_ref[...], ht_ref[...]).astype(BF16)
    for side in range(2):
        s = _dot(wka_ref[side], q_ref[side * half:(side + 1) * half, :])
        sa_ref[side] = s.reshape(PEER_NKEYS, PEER_HEADS, tm)

    for lc in range(tm // LANES):
        ls = slice(lc * LANES, (lc + 1) * LANES)
        sv0 = _top16_of_128(lambda n: sa_ref[0, n, :, ls])
        sv1 = _top16_of_128(lambda n: sa_ref[1, n, :, ls])
        top = _top16_pair_sums(sv0, sv1)
        tau = top[PEER_TOPK - 1]
        best = sv0[0] + sv1[0]
        z = jnp.exp(top[0] - best)
        for k in range(1, PEER_TOPK):
            z = z + jnp.exp(top[k] - best)
        inv_z = 1.0 / z
        slack = (jnp.abs(tau) + jnp.abs(sv0[0]) + jnp.abs(sv0[PEER_TOPK - 1])
                 + jnp.abs(sv1[0]) + jnp.abs(sv1[PEER_TOPK - 1])) * (2.0 ** -21) + 1e-30
        cut = tau - slack
        best0 = sv0[0]
        for k in range(PEER_TOPK):
            st_ref[k, :, ls] = sv1[k]

        def first_key(n, carry, lc=lc, ls=ls, sv1=sv1, cut=cut, best0=best0, inv_z=inv_z):
            s0 = sa_ref[0, n, :, ls]
            a_ref[lc, n] = jnp.exp(s0 - best0) * inv_z
            m_ref[lc, n] = _count_sorted(sv1, cut - s0, strict=False)
            return carry

        lax.fori_loop(0, PEER_NKEYS, first_key, 0, unroll=2)

    for h in range(PEER_HEADS):
        s1 = _dot(kb_ref[h], q_ref[half + h * PEER_DHALF:half + (h + 1) * PEER_DHALF, :])
        for lc in range(tm // LANES):
            ls = slice(lc * LANES, (lc + 1) * LANES)
            sorted1 = [st_ref[k, h:h + 1, ls] for k in range(PEER_TOPK)]
            for part in range(PEER_NKEYS // RANK_ROWS):
                x = s1[part * RANK_ROWS:(part + 1) * RANK_ROWS, ls]
                words = slice(part * RANK_ROWS // 2, (part + 1) * RANK_ROWS // 2)
                b_ref[lc, h, words, :] = _as_words(jnp.exp(x - sorted1[0]))
                r_ref[lc, h, words, :] = _as_words(_count_sorted(sorted1, x, strict=True))


def _peer_tables(peer_wq, sub_keys):
    wq = peer_wq.reshape(D_MODEL, PEER_HEADS, 2, PEER_DHALF)
    wq_t = jnp.transpose(wq, (2, 1, 3, 0)).reshape(2 * PEER_HEADS * PEER_DHALF, D_MODEL).astype(BF16)
    eye = jnp.eye(PEER_HEADS, dtype=F32)
    wka = jnp.einsum("hsnk,hg->snhgk", sub_keys, eye)
    wka = wka.reshape(2, PEER_NKEYS * PEER_HEADS, PEER_HEADS * PEER_DHALF).astype(BF16)
    kb = sub_keys[:, 1].astype(BF16)
    return wq_t, wka, kb


def _peer_prep(h2t, wq_t, wka, kb):
    T = h2t.shape[1]
    tm = PREP_TILE
    half = PEER_HEADS * PEER_DHALF
    first = jax.ShapeDtypeStruct((T // LANES, PEER_NKEYS, PEER_HEADS, LANES), F32)
    second = jax.ShapeDtypeStruct((T // LANES, PEER_HEADS, PEER_NKEYS // 2, LANES), jnp.uint32)
    fspec = pl.BlockSpec((tm // LANES, PEER_NKEYS, PEER_HEADS, LANES), lambda i: (i, 0, 0, 0))
    sspec = pl.BlockSpec((tm // LANES, PEER_HEADS, PEER_NKEYS // 2, LANES), lambda i: (i, 0, 0, 0))
    return pl.pallas_call(
        _peer_prep_kernel,
        grid=(T // tm,),
        in_specs=[
            pl.BlockSpec((D_MODEL, tm), lambda i: (0, i)),
            pl.BlockSpec((2 * half, D_MODEL), lambda i: (0, 0)),
            pl.BlockSpec((2, PEER_NKEYS * PEER_HEADS, half), lambda i: (0, 0, 0)),
            pl.BlockSpec((PEER_HEADS, PEER_NKEYS, PEER_DHALF), lambda i: (0, 0, 0)),
        ],
        out_specs=[fspec, fspec, sspec, sspec],
        out_shape=[first, first, second, second],
        scratch_shapes=[
            pltpu.VMEM((2 * half, tm), BF16),
            pltpu.VMEM((2, PEER_NKEYS, PEER_HEADS, tm), F32),
            pltpu.VMEM((PEER_TOPK, PEER_HEADS, tm), F32),
        ],
        compiler_params=_params(("parallel",)),
        name="peer_prep",
    )(h2t, wq_t, wka, kb)


def _peer_dense_kernel(ht_ref, a_ref, m_ref, b_ref, r_ref, x1_ref, g2_ref, fg_ref, u_hbm, vt_hbm,
                       y_ref, ubuf, vbuf, usem, vsem, acc_ref, act0_ref, act1_ref, w0_ref, w1_ref):
    act_refs = (act0_ref, act1_ref)
    w_refs = (w0_ref, w1_ref)
    tm = ht_ref.shape[1]
    n_chunks = tm // LANES
    n_sub = PEER_N // PEER_SB
    n_pairs = n_sub // 2
    keys_per_sub = PEER_SB // PEER_NKEYS
    sub_words = PEER_SB // 2
    half_rows = D_MODEL // 2

    def u_copy(s):
        slot = s % U_SLOTS
        return pltpu.make_async_copy(u_hbm.at[pl.ds(s * PEER_SB, PEER_SB), :], ubuf.at[slot], usem.at[slot])

    def v_copy(pair):
        slot = pair % V_SLOTS
        return pltpu.make_async_copy(vt_hbm.at[pair], vbuf.at[slot], vsem.at[slot])

    def activations(s, aslot):
        act = _dot(ubuf[s % U_SLOTS], ht_ref[...])
        for tc in range(n_chunks):
            act_refs[aslot][tc] = act[:, tc * LANES:(tc + 1) * LANES]

    def gated(s, aslot, wslot, pos):
        for tc in range(n_chunks):
            a_rows = [a_ref[tc, s * keys_per_sub + k] for k in range(keys_per_sub)]
            m_rows = [m_ref[tc, s * keys_per_sub + k] for k in range(keys_per_sub)]
            for part in range(PEER_NKEYS // GATE_ROWS):
                pwords = slice(part * GATE_ROWS // 2, (part + 1) * GATE_ROWS // 2)
                gates = [None] * keys_per_sub
                for h in range(PEER_HEADS):
                    b = _as_bf16(b_ref[tc, h, pwords, :])
                    r = _as_bf16(r_ref[tc, h, pwords, :])
                    for k in range(keys_per_sub):
                        a_row = jnp.broadcast_to(a_rows[k][h:h + 1, :], (GATE_ROWS, LANES)).astype(BF16)
                        m_row = jnp.broadcast_to(m_rows[k][h:h + 1, :], (GATE_ROWS, LANES)).astype(BF16)
                        term = jnp.where(r < m_row, b, jnp.zeros_like(b)) * a_row
                        gates[k] = term if gates[k] is None else gates[k] + term
                for k in range(keys_per_sub):
                    row0 = k * PEER_NKEYS + part * GATE_ROWS
                    g = _gelu(act_refs[aslot][tc, row0:row0 + GATE_ROWS, :].astype(BF16))
                    word0 = pos * sub_words + row0 // 2
                    w_refs[wslot][tc, word0:word0 + GATE_ROWS // 2, :] = pltpu.bitcast(g * gates[k], jnp.uint32)

    def accumulate(pair, wslot, half):
        w = _as_bf16(jnp.concatenate([w_refs[wslot][tc] for tc in range(n_chunks)], axis=1))
        rows = slice(half * half_rows, (half + 1) * half_rows)
        acc_ref[rows, :] += _dot(vbuf[pair % V_SLOTS, rows, :], w)

    def stage_dma(s, jj):
        static = isinstance(s, int)

        def when(cond, fn):
            if static:
                if cond:
                    fn()
            else:
                pl.when(cond)(fn)

        when(s + U_AHEAD < n_sub, lambda: u_copy(s + U_AHEAD).start())
        if jj % 2 == 0:
            when((s + 1) // 2 < n_pairs, lambda: v_copy((s + 1) // 2).start())
        when(s < n_sub, lambda: u_copy(s).wait())
        if jj % 2 == 0:
            when(s >= 3, lambda: v_copy((s - 3) // 2).wait())

    def stage_compute(s, jj, with_activations=True, with_gate=True):
        if with_activations:
            activations(s, (jj + 1) % 2)
        if with_gate:
            gated(s - 1, jj % 2, (jj // 2) % 2, jj % 2)
        accumulate((s - 3) // 2, (jj // 2 + 1) % 2, jj % 2)

    acc_ref[...] = jnp.zeros_like(acc_ref)
    w_refs[1][...] = jnp.zeros_like(w_refs[1])
    vbuf[(-1) % V_SLOTS] = jnp.zeros(vbuf.shape[1:], vbuf.dtype)
    for s in range(U_AHEAD + 1):
        u_copy(s).start()
    v_copy(0).start()
    u_copy(0).wait()
    activations(0, 0)

    def stages(t, carry):
        for jj in range(STAGE_UNROLL):
            s = 1 + t * STAGE_UNROLL + jj
            stage_dma(s, jj)
            stage_compute(s, jj)
        return carry

    lax.fori_loop(0, n_sub // STAGE_UNROLL, stages, 0)
    for s in (n_sub + 1, n_sub + 2):
        jj = (s - 1) % STAGE_UNROLL
        stage_dma(s, jj)
        stage_compute(s, jj, with_activations=False, with_gate=False)

    out = x1_ref[...] + g2_ref[0, 0] * acc_ref[...].T
    ms = jnp.mean(out * out, axis=-1, keepdims=True)
    y_ref[...] = out * lax.rsqrt(ms + EPS) * fg_ref[...]


def _peer_dense(h2t, u_bf, vt_sub, a, m, b, r, x1, mods, final_g, tokens_per_row):
    T = h2t.shape[1]
    tm = PEER_TM
    if tokens_per_row is None:
        g2_map = lambda i: (CTX_ROW, 5, 0, 0)
    else:
        g2_map = lambda i: ((i * tm) // tokens_per_row, 5, 0, 0)
    fspec = pl.BlockSpec((tm // LANES, PEER_NKEYS, PEER_HEADS, LANES), lambda i: (i, 0, 0, 0))
    sspec = pl.BlockSpec((tm // LANES, PEER_HEADS, PEER_NKEYS // 2, LANES), lambda i: (i, 0, 0, 0))
    return pl.pallas_call(
        _peer_dense_kernel,
        grid=(T // tm,),
        in_specs=[
            pl.BlockSpec((D_MODEL, tm), lambda i: (0, i)),
            fspec, fspec, sspec, sspec,
            pl.BlockSpec((tm, D_MODEL), lambda i: (i, 0)),
            pl.BlockSpec((1, 1, 1, D_MODEL), g2_map),
            pl.BlockSpec((1, D_MODEL), lambda i: (0, 0)),
            pl.BlockSpec(memory_space=pl.ANY),
            pl.BlockSpec(memory_space=pl.ANY),
        ],
        out_specs=pl.BlockSpec((tm, D_MODEL), lambda i: (i, 0)),
        out_shape=jax.ShapeDtypeStruct((T, D_MODEL), F32),
        scratch_shapes=[
            pltpu.VMEM((U_SLOTS, PEER_SB, D_MODEL), BF16),
            pltpu.VMEM((V_SLOTS, D_MODEL, 2 * PEER_SB), BF16),
            pltpu.SemaphoreType.DMA((U_SLOTS,)),
            pltpu.SemaphoreType.DMA((V_SLOTS,)),
            pltpu.VMEM((D_MODEL, tm), F32),
            pltpu.VMEM((tm // LANES, PEER_SB, LANES), F32),
            pltpu.VMEM((tm // LANES, PEER_SB, LANES), F32),
            pltpu.VMEM((tm // LANES, PEER_SB, LANES), jnp.uint32),
            pltpu.VMEM((tm // LANES, PEER_SB, LANES), jnp.uint32),
        ],
        compiler_params=_params(("parallel",)),
        name="peer_dense",
    )(h2t, a, m, b, r, x1, mods, final_g.reshape(1, D_MODEL), u_bf, vt_sub)


def _trunk(x, mods, h0r, h0i, s5w, p, tables, final_g, per_batch, grid):
    B, L, _ = x.shape
    a_re, a_im, wb, wc_re, wc_im = s5w
    zs2d, zp = _inproj(x, mods, p["norm1_g"], p["w_in"], per_batch)
    zs = zs2d.reshape(L, B, D_SSM)
    yf, sfr, sfi = _s5_direction(zs, a_re, a_im, wb, wc_re, wc_im, h0r[:, 0], h0i[:, 0], 0)
    yb, sbr, sbi = _s5_direction(zs, a_re, a_im, wb, wc_re, wc_im, h0r[:, 1], h0i[:, 1], 1)
    x1, h2t = _mix(x, zs2d, zp, yf.reshape(L, B * D_SSM), yb.reshape(L, B * D_SSM), mods, p["ssm_d"],
                   p["w_glu"], p["w_pool"], p["pool_scale"], p["w_out"], p["norm2_g"], per_batch, grid)
    wq_t, wka, kb, u_bf, vt_bf = tables
    a, m, b, r = _peer_prep(h2t, wq_t, wka, kb)
    y = _peer_dense(h2t, u_bf, vt_bf, a, m, b, r, x1, mods, final_g, L if per_batch else None)
    new_re = jnp.stack([sfr, sbr], axis=1).reshape(B, 2, SSM_G, SSM_P)
    new_im = jnp.stack([sfi, sbi], axis=1).reshape(B, 2, SSM_G, SSM_P)
    return y.reshape(B, L, D_MODEL), new_re, new_im


def kernel(x_prompt, x_sample, state_ssm_re, state_ssm_im, c, c_ctx, norm1_g, w_mod, b_mod, w_in,
           ssm_lambda_re, ssm_lambda_im, ssm_log_dt, ssm_b_re, ssm_b_im, ssm_c_re, ssm_c_im, ssm_d,
           w_glu, w_pool, pool_scale, w_out, norm2_g, peer_wq, peer_subkeys, peer_u, peer_v, final_g):
    depth = w_mod.shape[0]
    assert depth == 1, "single trunk layer"
    l = 0
    n_dec = c.shape[0]
    cond = jnp.zeros((COND_ROWS, D_MODEL), F32).at[:n_dec].set(c).at[CTX_ROW].set(c_ctx)
    mods = _mod_vectors(cond, w_mod[l], b_mod[l]).reshape(COND_ROWS, N_MOD, 1, D_MODEL)

    ar, ai, bbr, bbi = _discretise(ssm_lambda_re[l], ssm_lambda_im[l], ssm_log_dt[l], ssm_b_re[l], ssm_b_im[l])
    s5w = _s5_weights(ar, ai, bbr, bbi, ssm_c_re[l], ssm_c_im[l])

    wq_t, wka, kb = _peer_tables(peer_wq[l], peer_subkeys[l])
    vt_sub = jnp.transpose(peer_v[l].reshape(PEER_N // (2 * PEER_SB), 2 * PEER_SB, D_MODEL), (0, 2, 1)).astype(BF16)
    tables = (wq_t, wka, kb, peer_u[l].astype(BF16), vt_sub)

    p = {"norm1_g": norm1_g[l], "w_in": w_in[l].astype(BF16), "ssm_d": ssm_d[l], "w_glu": w_glu[l],
         "w_pool": w_pool[l], "pool_scale": pool_scale[l], "w_out": w_out[l], "norm2_g": norm2_g[l]}

    bp = x_prompt.shape[0]
    zeros = jnp.zeros((bp, 2, SSM_N), F32)
    y_prompt, new_re, new_im = _trunk(x_prompt, mods, zeros, zeros, s5w, p, tables, final_g, False, False)
    h0r = state_ssm_re[:, l].reshape(n_dec, 2, SSM_N)
    h0i = state_ssm_im[:, l].reshape(n_dec, 2, SSM_N)
    y_sample, _, _ = _trunk(x_sample, mods, h0r, h0i, s5w, p, tables, final_g, True, True)
    return (y_prompt, y_sample, new_re[:, None], new_im[:, None])
```

```python
import functools
import math

import numpy as np
import jax
import jax.numpy as jnp
from jax import lax
from jax.experimental import pallas as pl
from jax.experimental.pallas import tpu as pltpu

F32 = jnp.float32
BF16 = jnp.bfloat16

D_MODEL = 1024
D_SSM = 512
D_POOL = 512
SSM_H = 16
SSM_G = 32
SSM_P = 64
SSM_N = SSM_G * SSM_P
POOL_WINDOWS = (2, 4, 8, 16)
POOL_G = 4
POOL_C = 128
GRID_W = 64
PEER_HEADS = 8
PEER_NKEYS = 128
PEER_N = PEER_NKEYS * PEER_NKEYS
PEER_DHALF = 128
PEER_TOPK = 16
N_MOD = 6
EPS = 1e-6

SUBLANES = 8
LANES = 128
VMEM_LIMIT = 56 * 1024 * 1024

COND_ROWS = 16
CTX_ROW = 8
SEQ_TILE = 256
S5_CHUNK = 64
S5_LANES = 512
PREP_TILE = 256
PEER_TM = 512
PEER_SB = 512
STAGE_UNROLL = 4
U_SLOTS = 6
U_AHEAD = 4
V_SLOTS = 3
RANK_ROWS = 64
GATE_ROWS = 64


def _gelu(x):
    return 0.5 * x * (1.0 + jnp.tanh(0.7978845608028654 * (x + 0.044715 * (x * x * x))))


def _split_bf16(x):
    hi = x.astype(BF16)
    lo = (x - hi.astype(F32)).astype(BF16)
    return hi, lo


def _dot(a, b):
    return jnp.dot(a, b, preferred_element_type=F32)


def _params(sem, flags=None):
    return pltpu.CompilerParams(dimension_semantics=sem, vmem_limit_bytes=VMEM_LIMIT, flags=flags)


def _mod_kernel(cond_ref, whi_ref, wlo_ref, b_ref, o_ref):
    c = cond_ref[...]
    s = c * jax.nn.sigmoid(c)
    shi, slo = _split_bf16(s)
    whi = whi_ref[...]
    o_ref[...] = _dot(shi, whi) + _dot(slo, whi) + _dot(shi, wlo_ref[...]) + b_ref[...]


def _mod_vectors(cond, w_mod, b_mod):
    whi, wlo = _split_bf16(w_mod)
    n = w_mod.shape[1]
    bn = D_MODEL
    return pl.pallas_call(
        _mod_kernel,
        grid=(n // bn,),
        in_specs=[
            pl.BlockSpec((COND_ROWS, D_MODEL), lambda k: (0, 0)),
            pl.BlockSpec((D_MODEL, bn), lambda k: (0, k)),
            pl.BlockSpec((D_MODEL, bn), lambda k: (0, k)),
            pl.BlockSpec((1, bn), lambda k: (0, k)),
        ],
        out_specs=pl.BlockSpec((COND_ROWS, bn), lambda k: (0, k)),
        out_shape=jax.ShapeDtypeStruct((COND_ROWS, n), F32),
        compiler_params=_params(("arbitrary",)),
        name="mod",
    )(cond, whi, wlo, b_mod.reshape(1, n))


def _disc_kernel(lr_ref, li_ref, ldt_ref, br_ref, bi_ref, ar_ref, ai_ref, bbr_ref, bbi_ref):
    lr = lr_ref[...]
    li = li_ref[...]
    dt = jnp.exp(ldt_ref[...])
    mag = jnp.exp(lr * dt)
    ar = mag * jnp.cos(li * dt)
    ai = mag * jnp.sin(li * dt)
    den = lr * lr + li * li
    nr = ar - 1.0
    ni = ai
    fr = (nr * lr + ni * li) / den
    fi = (ni * lr - nr * li) / den
    ar_ref[...] = ar
    ai_ref[...] = ai
    br = br_ref[...]
    bi = bi_ref[...]
    frb = fr[:, None, :]
    fib = fi[:, None, :]
    bbr_ref[...] = frb * br - fib * bi
    bbi_ref[...] = frb * bi + fib * br


def _discretise(lam_re, lam_im, log_dt, b_re, b_im):
    rows = 2 * SSM_G
    lr = lam_re.reshape(rows, SSM_P)
    li = lam_im.reshape(rows, SSM_P)
    ldt = jnp.broadcast_to(log_dt.reshape(rows, 1), (rows, SSM_P))
    br = jnp.swapaxes(b_re, -1, -2).reshape(rows, SSM_H, SSM_P)
    bi = jnp.swapaxes(b_im, -1, -2).reshape(rows, SSM_H, SSM_P)
    small = jax.ShapeDtypeStruct((rows, SSM_P), F32)
    big = jax.ShapeDtypeStruct((rows, SSM_H, SSM_P), F32)
    return pl.pallas_call(_disc_kernel, out_shape=(small, small, big, big), name="disc")(lr, li, ldt, br, bi)


def _s5_weights(ar, ai, bbr, bbi, c_re, c_im):
    half_g = SSM_G // 2
    eye = jnp.eye(half_g, dtype=F32)
    a_re = ar.reshape(2, 1, SSM_N)
    a_im = ai.reshape(2, 1, SSM_N)

    def bmat(t):
        t = t.reshape(2, 2, half_g, SSM_H, SSM_P)
        m = jnp.einsum("djghp,gk->djghkp", t, eye)
        return m.reshape(2, 2, half_g * SSM_H, half_g * SSM_P)

    def cmat(t):
        t = t.reshape(2, 2, half_g, SSM_H, SSM_P)
        m = jnp.einsum("djghp,gk->djgpkh", t, eye)
        return m.reshape(2, 2, half_g * SSM_P, half_g * SSM_H)

    wb = jnp.concatenate([bmat(bbr), bmat(bbi)], axis=-1).astype(BF16)
    wc_re = cmat(c_re).astype(BF16)
    wc_im = cmat(c_im).astype(BF16)
    return a_re, a_im, wb, wc_re, wc_im


def _inproj_kernel(x_ref, sh_ref, sc_ref, g_ref, w_ref, zs_ref, zp_ref):
    x = x_ref[0]
    ms = jnp.mean(x * x, axis=-1, keepdims=True)
    y = x * lax.rsqrt(ms + EPS) * g_ref[...]
    h = y * (1.0 + sc_ref[0, 0]) + sh_ref[0, 0]
    z = _dot(h.astype(BF16), w_ref[...])
    zs_ref[...] = z[:, :D_SSM]
    zp_ref[0] = z[:, D_SSM:]


def _mod_spec(k, per_batch, tokens_per_row=None):
    if per_batch:
        return pl.BlockSpec((1, 1, 1, D_MODEL), lambda b, i: (b, k, 0, 0))
    return pl.BlockSpec((1, 1, 1, D_MODEL), lambda b, i: (CTX_ROW, k, 0, 0))


def _inproj(x, mods, g, w_in, per_batch):
    B, L, _ = x.shape
    tm = SEQ_TILE
    return pl.pallas_call(
        _inproj_kernel,
        grid=(B, L // tm),
        in_specs=[
            pl.BlockSpec((1, tm, D_MODEL), lambda b, i: (b, i, 0)),
            _mod_spec(0, per_batch),
            _mod_spec(1, per_batch),
            pl.BlockSpec((1, D_MODEL), lambda b, i: (0, 0)),
            pl.BlockSpec((D_MODEL, D_MODEL), lambda b, i: (0, 0)),
        ],
        out_specs=[
            pl.BlockSpec((tm, D_SSM), lambda b, i: (i, b)),
            pl.BlockSpec((1, tm, D_POOL), lambda b, i: (b, i, 0)),
        ],
        out_shape=[
            jax.ShapeDtypeStruct((L, B * D_SSM), F32),
            jax.ShapeDtypeStruct((B, L, D_POOL), F32),
        ],
        compiler_params=_params(("parallel", "parallel")),
        name="inproj",
    )(x, mods, mods, g.reshape(1, D_MODEL), w_in)


def _s5_kernel(u_ref, are_ref, aim_ref, wb_ref, wcr_ref, wci_ref, h0r_ref, h0i_ref,
               y_ref, sr_ref, si_ref, bur_ref, bui_ref, cr_ref, ci_ref, *, reverse, n_chunks):
    c = pl.program_id(1)
    rows = S5_CHUNK * SUBLANES
    half = SSM_N // 2
    half_in = D_SSM // 2

    @pl.when(c == 0)
    def _():
        cr_ref[...] = h0r_ref[...]
        ci_ref[...] = h0i_ref[...]

    u = u_ref[...].reshape(rows, D_SSM).astype(BF16)
    for j in range(2):
        r = _dot(u[:, j * half_in:(j + 1) * half_in], wb_ref[0, j])
        bur_ref[:, j * half:(j + 1) * half] = r[:, :half]
        bui_ref[:, j * half:(j + 1) * half] = r[:, half:]

    for lc in range(SSM_N // S5_LANES):
        ls = slice(lc * S5_LANES, (lc + 1) * S5_LANES)
        ar = jnp.broadcast_to(are_ref[0, :, ls], (SUBLANES, S5_LANES))
        ai = jnp.broadcast_to(aim_ref[0, :, ls], (SUBLANES, S5_LANES))

        def step(i, carry, ls=ls, ar=ar, ai=ai):
            hr, hi = carry
            t = (S5_CHUNK - 1 - i) if reverse else i
            row = pl.multiple_of(t * SUBLANES, SUBLANES)
            nhr = ar * hr - ai * hi + bur_ref[pl.ds(row, SUBLANES), ls]
            nhi = ar * hi + ai * hr + bui_ref[pl.ds(row, SUBLANES), ls]
            bur_ref[pl.ds(row, SUBLANES), ls] = nhr
            bui_ref[pl.ds(row, SUBLANES), ls] = nhi
            return nhr, nhi

        hr, hi = lax.fori_loop(0, S5_CHUNK, step, (cr_ref[:, ls], ci_ref[:, ls]), unroll=4)
        cr_ref[:, ls] = hr
        ci_ref[:, ls] = hi

    ys = []
    for j in range(2):
        hs = slice(j * half, (j + 1) * half)
        yr = _dot(bur_ref[:, hs].astype(BF16), wcr_ref[0, j])
        yi = _dot(bui_ref[:, hs].astype(BF16), wci_ref[0, j])
        ys.append(yr - yi)
    y_ref[...] = jnp.concatenate(ys, axis=-1).reshape(S5_CHUNK, SUBLANES, D_SSM)

    @pl.when(c == n_chunks - 1)
    def _():
        sr_ref[...] = cr_ref[...]
        si_ref[...] = ci_ref[...]


def _s5_direction(zs, a_re, a_im, wb, wc_re, wc_im, h0r, h0i, d):
    L, B, _ = zs.shape
    n_chunks = L // S5_CHUNK
    reverse = d == 1
    cidx = (lambda g, c: n_chunks - 1 - c) if reverse else (lambda g, c: c)
    rows = S5_CHUNK * SUBLANES
    kern = functools.partial(_s5_kernel, reverse=reverse, n_chunks=n_chunks)
    state = jax.ShapeDtypeStruct((B, SSM_N), F32)
    return pl.pallas_call(
        kern,
        grid=(B // SUBLANES, n_chunks),
        in_specs=[
            pl.BlockSpec((S5_CHUNK, SUBLANES, D_SSM), lambda g, c: (cidx(g, c), g, 0)),
            pl.BlockSpec((1, 1, SSM_N), lambda g, c: (d, 0, 0)),
            pl.BlockSpec((1, 1, SSM_N), lambda g, c: (d, 0, 0)),
            pl.BlockSpec((1, 2, D_SSM // 2, SSM_N), lambda g, c: (d, 0, 0, 0)),
            pl.BlockSpec((1, 2, SSM_N // 2, D_SSM // 2), lambda g, c: (d, 0, 0, 0)),
            pl.BlockSpec((1, 2, SSM_N // 2, D_SSM // 2), lambda g, c: (d, 0, 0, 0)),
            pl.BlockSpec((SUBLANES, SSM_N), lambda g, c: (g, 0)),
            pl.BlockSpec((SUBLANES, SSM_N), lambda g, c: (g, 0)),
        ],
        out_specs=[
            pl.BlockSpec((S5_CHUNK, SUBLANES, D_SSM), lambda g, c: (cidx(g, c), g, 0)),
            pl.BlockSpec((SUBLANES, SSM_N), lambda g, c: (g, 0)),
            pl.BlockSpec((SUBLANES, SSM_N), lambda g, c: (g, 0)),
        ],
        out_shape=[jax.ShapeDtypeStruct((L, B, D_SSM), F32), state, state],
        scratch_shapes=[
            pltpu.VMEM((rows, SSM_N), F32),
            pltpu.VMEM((rows, SSM_N), F32),
            pltpu.VMEM((SUBLANES, SSM_N), F32),
            pltpu.VMEM((SUBLANES, SSM_N), F32),
        ],
        compiler_params=_params(("parallel", "arbitrary")),
        name="s5_bwd" if reverse else "s5_fwd",
    )(zs, a_re, a_im, wb, wc_re, wc_im, h0r, h0i)


def _pool_tables(grid):
    seg = GRID_W if grid else SEQ_TILE
    t = np.arange(SEQ_TILE)
    pos = t % seg
    base = t - pos
    mats, invs = [], []
    for w in POOL_WINDOWS:
        lo = np.clip(pos - w // 2, 0, seg)
        hi = np.clip(pos + (w - w // 2), 0, seg)
        s = t[None, :]
        m = (s >= (base + lo)[:, None]) & (s < (base + hi)[:, None])
        mats.append(m.astype(np.float32))
        invs.append(np.broadcast_to((1.0 / (hi - lo).astype(np.float32))[:, None], (SEQ_TILE, LANES)))
    return jnp.asarray(np.stack(mats), dtype=BF16), jnp.asarray(np.stack(invs), dtype=F32)


def _mix_kernel(x_ref, u_ref, zp_ref, yf_ref, yb_ref, g1_ref, sh2_ref, sc2_ref, d_ref, wglu_ref,
                pm_ref, pinv_ref, wpool_ref, pscale_ref, wout_ref, n2_ref, x1_ref, h2t_ref):
    x = x_ref[0]
    y = d_ref[...] * u_ref[...] + yf_ref[...] + yb_ref[...]
    g = _gelu(y)
    ys = g * jax.nn.sigmoid(_dot(g.astype(BF16), wglu_ref[...]))

    zp = zp_ref[0]
    parts = [ys]
    for gi in range(POOL_G):
        zg = zp[:, gi * POOL_C:(gi + 1) * POOL_C]
        hi, lo = _split_bf16(zg)
        pm = pm_ref[gi]
        win = _dot(pm, hi) + _dot(pm, lo)
        pooled = win * pinv_ref[gi] - zg
        og = _dot(pooled.astype(BF16), wpool_ref[gi])
        parts.append(og * pscale_ref[:, gi * POOL_C:(gi + 1) * POOL_C])
    mix = _dot(jnp.concatenate(parts, axis=-1).astype(BF16), wout_ref[...])

    x1 = x + g1_ref[0, 0] * mix
    x1_ref[...] = x1
    ms = jnp.mean(x1 * x1, axis=-1, keepdims=True)
    h2 = x1 * lax.rsqrt(ms + EPS) * n2_ref[...]
    h2 = h2 * (1.0 + sc2_ref[0, 0]) + sh2_ref[0, 0]
    h2t_ref[...] = h2.T.astype(BF16)


def _mix(x, zs2d, zp, yf2d, yb2d, mods, ssm_d, w_glu, w_pool, pool_scale, w_out, norm2_g, per_batch, grid):
    B, L, _ = x.shape
    tm = SEQ_TILE
    nt = L // tm
    pm, pinv = _pool_tables(grid)
    tm_spec = pl.BlockSpec((tm, D_SSM), lambda b, i: (i, b))
    const2 = lambda b, i: (0, 0)
    const3 = lambda b, i: (0, 0, 0)
    return pl.pallas_call(
        _mix_kernel,
        grid=(B, nt),
        in_specs=[
            pl.BlockSpec((1, tm, D_MODEL), lambda b, i: (b, i, 0)),
            tm_spec,
            pl.BlockSpec((1, tm, D_POOL), lambda b, i: (b, i, 0)),
            tm_spec,
            tm_spec,
            _mod_spec(2, per_batch),
            _mod_spec(3, per_batch),
            _mod_spec(4, per_batch),
            pl.BlockSpec((1, D_SSM), const2),
            pl.BlockSpec((D_SSM, D_SSM), const2),
            pl.BlockSpec((POOL_G, tm, tm), const3),
            pl.BlockSpec((POOL_G, tm, LANES), const3),
            pl.BlockSpec((POOL_G, POOL_C, POOL_C), const3),
            pl.BlockSpec((1, D_POOL), const2),
            pl.BlockSpec((D_MODEL, D_MODEL), const2),
            pl.BlockSpec((1, D_MODEL), const2),
        ],
        out_specs=[
            pl.BlockSpec((tm, D_MODEL), lambda b, i: (b * nt + i, 0)),
            pl.BlockSpec((D_MODEL, tm), lambda b, i: (0, b * nt + i)),
        ],
        out_shape=[
            jax.ShapeDtypeStruct((B * L, D_MODEL), F32),
            jax.ShapeDtypeStruct((D_MODEL, B * L), BF16),
        ],
        compiler_params=_params(("parallel", "parallel")),
        name="mix",
    )(x, zs2d, zp, yf2d, yb2d, mods, mods, mods, ssm_d.reshape(1, D_SSM), w_glu.astype(BF16),
      pm, pinv, w_pool.astype(BF16), pool_scale.reshape(1, D_POOL), w_out.astype(BF16),
      norm2_g.reshape(1, D_MODEL))


def _sort_pairs(n):
    pairs = []

    def merge(lo, hi, r):
        step = r * 2
        if step < hi - lo:
            merge(lo, hi, step)
            merge(lo + r, hi, step)
            for i in range(lo + r, hi - r, step):
                pairs.append((i, i + r))
        else:
            pairs.append((lo, lo + r))

    def sort(lo, hi):
        if hi - lo >= 1:
            mid = lo + (hi - lo) // 2
            sort(lo, mid)
            sort(mid + 1, hi)
            merge(lo, hi, 1)

    sort(0, n - 1)
    return pairs


_SORT16 = _sort_pairs(PEER_TOPK)


def _vmax(a, b):
    if a is None:
        return b
    if b is None:
        return a
    return jnp.maximum(a, b)


def _vmin(a, b):
    if a is None or b is None:
        return None
    return jnp.minimum(a, b)


def _sort16_desc(w):
    w = list(w)
    for i, j in _SORT16:
        w[i], w[j] = _vmax(w[i], w[j]), _vmin(w[i], w[j])
    return w


def _merge_top16(a, b):
    k = PEER_TOPK
    w = [_vmax(a[i], b[k - 1 - i]) for i in range(k)]
    d = k // 2
    while d >= 1:
        for i in range(k):
            if i & d == 0:
                w[i], w[i + d] = _vmax(w[i], w[i + d]), _vmin(w[i], w[i + d])
        d //= 2
    return w


def _top16_of_128(load):
    cur = None
    for grp in range(PEER_NKEYS // PEER_TOPK):
        s = _sort16_desc([load(grp * PEER_TOPK + i) for i in range(PEER_TOPK)])
        cur = s if cur is None else _merge_top16(cur, s)
    return cur


def _top16_pair_sums(sv0, sv1):
    k = PEER_TOPK
    pad = lambda row: row + [None] * (k - len(row))
    cur = [sv0[0] + sv1[j] for j in range(k)]
    for i in range(1, k // 2):
        cur = _merge_top16(cur, pad([sv0[i] + sv1[j] for j in range(k // (i + 1))]))
    return _merge_top16(cur, pad([sv0[i] + sv1[0] for i in range(k // 2, k)]))


def _count_sorted(v, x, strict):
    above = (lambda p: p > x) if strict else (lambda p: p >= x)
    sel = jnp.where
    g8 = above(v[7])
    g4 = above(sel(g8, v[11], v[3]))
    g2 = above(sel(g8, sel(g4, v[13], v[9]), sel(g4, v[5], v[1])))
    g1 = above(sel(g8, sel(g4, sel(g2, v[14], v[12]), sel(g2, v[10], v[8])),
                   sel(g4, sel(g2, v[6], v[4]), sel(g2, v[2], v[0]))))
    low = sel(g8, 8.0, 0.0) + sel(g4, 4.0, 0.0) + sel(g2, 2.0, 0.0) + sel(g1, 1.0, 0.0)
    return sel(above(v[15]), 16.0, low)


def _as_words(x):
    return pltpu.bitcast(x.astype(BF16), jnp.uint32)


def _as_bf16(words):
    return pltpu.bitcast(words, BF16)


def _peer_prep_kernel(ht_ref, wq_ref, wka_ref, kb_ref, a_ref, m_ref, b_ref, r_ref, q_ref, sa_ref, st_ref):
    tm = ht_ref.shape[1]
    half = PEER_HEADS * PEER_DHALF
    q_ref[...] = _dot(wq_ref[...], ht_ref[...]).astype(BF16)
    for side in range(2):
        s = _dot(wka_ref[side], q_ref[side * half:(side + 1) * half, :])
        sa_ref[side] = s.reshape(PEER_NKEYS, PEER_HEADS, tm)

    for lc in range(tm // LANES):
        ls = slice(lc * LANES, (lc + 1) * LANES)
        sv0 = _top16_of_128(lambda n: sa_ref[0, n, :, ls])
        sv1 = _top16_of_128(lambda n: sa_ref[1, n, :, ls])
        top = _top16_pair_sums(sv0, sv1)
        tau = top[PEER_TOPK - 1]
        best = sv0[0] + sv1[0]
        z = jnp.exp(top[0] - best)
        for k in range(1, PEER_TOPK):
            z = z + jnp.exp(top[k] - best)
        inv_z = 1.0 / z
        slack = (jnp.abs(tau) + jnp.abs(sv0[0]) + jnp.abs(sv0[PEER_TOPK - 1])
                 + jnp.abs(sv1[0]) + jnp.abs(sv1[PEER_TOPK - 1])) * (2.0 ** -21) + 1e-30
        cut = tau - slack
        best0 = sv0[0]
        for k in range(PEER_TOPK):
            st_ref[k, :, ls] = sv1[k]

        def first_key(n, carry, lc=lc, ls=ls, sv1=sv1, cut=cut, best0=best0, inv_z=inv_z):
            s0 = sa_ref[0, n, :, ls]
            a_ref[lc, n] = jnp.exp(s0 - best0) * inv_z
            m_ref[lc, n] = _count_sorted(sv1, cut - s0, strict=False)
            return carry

        lax.fori_loop(0, PEER_NKEYS, first_key, 0, unroll=2)

    for h in range(PEER_HEADS):
        s1 = _dot(kb_ref[h], q_ref[half + h * PEER_DHALF:half + (h + 1) * PEER_DHALF, :])
        for lc in range(tm // LANES):
            ls = slice(lc * LANES, (lc + 1) * LANES)
            sorted1 = [st_ref[k, h:h + 1, ls] for k in range(PEER_TOPK)]
            for part in range(PEER_NKEYS // RANK_ROWS):
                x = s1[part * RANK_ROWS:(part + 1) * RANK_ROWS, ls]
                words = slice(part * RANK_ROWS // 2, (part + 1) * RANK_ROWS // 2)
                b_ref[lc, h, words, :] = _as_words(jnp.exp(x - sorted1[0]))
                r_ref[lc, h, words, :] = _as_words(_count_sorted(sorted1, x, strict=True))


def _peer_tables(peer_wq, sub_keys):
    wq = peer_wq.reshape(D_MODEL, PEER_HEADS, 2, PEER_DHALF)
    wq_t = jnp.transpose(wq, (2, 1, 3, 0)).reshape(2 * PEER_HEADS * PEER_DHALF, D_MODEL).astype(BF16)
    eye = jnp.eye(PEER_HEADS, dtype=F32)
    wka = jnp.einsum("hsnk,hg->snhgk", sub_keys, eye)
    wka = wka.reshape(2, PEER_NKEYS * PEER_HEADS, PEER_HEADS * PEER_DHALF).astype(BF16)
    kb = sub_keys[:, 1].astype(BF16)
    return wq_t, wka, kb


def _peer_prep(h2t, wq_t, wka, kb):
    T = h2t.shape[1]
    tm = PREP_TILE
    half = PEER_HEADS * PEER_DHALF
    first = jax.ShapeDtypeStruct((T // LANES, PEER_NKEYS, PEER_HEADS, LANES), F32)
    second = jax.ShapeDtypeStruct((T // LANES, PEER_HEADS, PEER_NKEYS // 2, LANES), jnp.uint32)
    fspec = pl.BlockSpec((tm // LANES, PEER_NKEYS, PEER_HEADS, LANES), lambda i: (i, 0, 0, 0))
    sspec = pl.BlockSpec((tm // LANES, PEER_HEADS, PEER_NKEYS // 2, LANES), lambda i: (i, 0, 0, 0))
    return pl.pallas_call(
        _peer_prep_kernel,
        grid=(T // tm,),
        in_specs=[
            pl.BlockSpec((D_MODEL, tm), lambda i: (0, i)),
            pl.BlockSpec((2 * half, D_MODEL), lambda i: (0, 0)),
            pl.BlockSpec((2, PEER_NKEYS * PEER_HEADS, half), lambda i: (0, 0, 0)),
            pl.BlockSpec((PEER_HEADS, PEER_NKEYS, PEER_DHALF), lambda i: (0, 0, 0)),
        ],
        out_specs=[fspec, fspec, sspec, sspec],
        out_shape=[first, first, second, second],
        scratch_shapes=[
            pltpu.VMEM((2 * half, tm), BF16),
            pltpu.VMEM((2, PEER_NKEYS, PEER_HEADS, tm), F32),
            pltpu.VMEM((PEER_TOPK, PEER_HEADS, tm), F32),
        ],
        compiler_params=_params(("parallel",)),
        name="peer_prep",
    )(h2t, wq_t, wka, kb)


def _peer_dense_kernel(ht_ref, a_ref, m_ref, b_ref, r_ref, x1_ref, g2_ref, fg_ref, u_hbm, vt_hbm,
                       y_ref, ubuf, vbuf, usem, vsem, acc_ref, act0_ref, act1_ref, w0_ref, w1_ref):
    act_refs = (act0_ref, act1_ref)
    w_refs = (w0_ref, w1_ref)
    tm = ht_ref.shape[1]
    n_chunks = tm // LANES
    n_sub = PEER_N // PEER_SB
    n_pairs = n_sub // 2
    keys_per_sub = PEER_SB // PEER_NKEYS
    sub_words = PEER_SB // 2
    half_rows = D_MODEL // 2

    def u_copy(s):
        slot = s % U_SLOTS
        return pltpu.make_async_copy(u_hbm.at[pl.ds(s * PEER_SB, PEER_SB), :], ubuf.at[slot], usem.at[slot])

    def v_copy(pair):
        slot = pair % V_SLOTS
        return pltpu.make_async_copy(vt_hbm.at[pair], vbuf.at[slot], vsem.at[slot])

    def activations(s, aslot):
        act = _dot(ubuf[s % U_SLOTS], ht_ref[...])
        for tc in range(n_chunks):
            act_refs[aslot][tc] = act[:, tc * LANES:(tc + 1) * LANES]

    def gated(s, aslot, wslot, pos):
        for tc in range(n_chunks):
            a_rows = [a_ref[tc, s * keys_per_sub + k] for k in range(keys_per_sub)]
            m_rows = [m_ref[tc, s * keys_per_sub + k] for k in range(keys_per_sub)]
            for part in range(PEER_NKEYS // GATE_ROWS):
                pwords = slice(part * GATE_ROWS // 2, (part + 1) * GATE_ROWS // 2)
                gates = [None] * keys_per_sub
                for h in range(PEER_HEADS):
                    b = _as_bf16(b_ref[tc, h, pwords, :])
                    r = _as_bf16(r_ref[tc, h, pwords, :])
                    for k in range(keys_per_sub):
                        a_row = jnp.broadcast_to(a_rows[k][h:h + 1, :], (GATE_ROWS, LANES)).astype(BF16)
                        m_row = jnp.broadcast_to(m_rows[k][h:h + 1, :], (GATE_ROWS, LANES)).astype(BF16)
                        term = jnp.where(r < m_row, b, jnp.zeros_like(b)) * a_row
                        gates[k] = term if gates[k] is None else gates[k] + term
                for k in range(keys_per_sub):
                    row0 = k * PEER_NKEYS + part * GATE_ROWS
                    g = _gelu(act_refs[aslot][tc, row0:row0 + GATE_ROWS, :].astype(BF16))
                    word0 = pos * sub_words + row0 // 2
                    w_refs[wslot][tc, word0:word0 + GATE_ROWS // 2, :] = pltpu.bitcast(g * gates[k], jnp.uint32)

    def accumulate(pair, wslot, half):
        w = _as_bf16(jnp.concatenate([w_refs[wslot][tc] for tc in range(n_chunks)], axis=1))
        rows = slice(half * half_rows, (half + 1) * half_rows)
        acc_ref[rows, :] += _dot(vbuf[pair % V_SLOTS, rows, :], w)

    def stage_dma(s, jj):
        static = isinstance(s, int)

        def when(cond, fn):
            if static:
                if cond:
                    fn()
            else:
                pl.when(cond)(fn)

        when(s + U_AHEAD < n_sub, lambda: u_copy(s + U_AHEAD).start())
        if jj % 2 == 0:
            when((s + 1) // 2 < n_pairs, lambda: v_copy((s + 1) // 2).start())
        when(s < n_sub, lambda: u_copy(s).wait())
        if jj % 2 == 0:
            when(s >= 3, lambda: v_copy((s - 3) // 2).wait())

    def stage_compute(s, jj, with_activations=True, with_gate=True):
        if with_activations:
            activations(s, (jj + 1) % 2)
        if with_gate:
            gated(s - 1, jj % 2, (jj // 2) % 2, jj % 2)
        accumulate((s - 3) // 2, (jj // 2 + 1) % 2, jj % 2)

    acc_ref[...] = jnp.zeros_like(acc_ref)
    w_refs[1][...] = jnp.zeros_like(w_refs[1])
    vbuf[(-1) % V_SLOTS] = jnp.zeros(vbuf.shape[1:], vbuf.dtype)
    for s in range(U_AHEAD + 1):
        u_copy(s).start()
    v_copy(0).start()
    u_copy(0).wait()
    activations(0, 0)

    def stages(t, carry):
        for jj in range(STAGE_UNROLL):
            s = 1 + t * STAGE_UNROLL + jj
            stage_dma(s, jj)
            stage_compute(s, jj)
        return carry

    lax.fori_loop(0, n_sub // STAGE_UNROLL, stages, 0)
    for s in (n_sub + 1, n_sub + 2):
        jj = (s - 1) % STAGE_UNROLL
        stage_dma(s, jj)
        stage_compute(s, jj, with_activations=False, with_gate=False)

    out = x1_ref[...] + g2_ref[0, 0] * acc_ref[...].T
    ms = jnp.mean(out * out, axis=-1, keepdims=True)
    y_ref[...] = out * lax.rsqrt(ms + EPS) * fg_ref[...]


def _peer_dense(h2t, u_bf, vt_sub, a, m, b, r, x1, mods, final_g, tokens_per_row):
    T = h2t.shape[1]
    tm = PEER_TM
    if tokens_per_row is None:
        g2_map = lambda i: (CTX_ROW, 5, 0, 0)
    else:
        g2_map = lambda i: ((i * tm) // tokens_per_row, 5, 0, 0)
    fspec = pl.BlockSpec((tm // LANES, PEER_NKEYS, PEER_HEADS, LANES), lambda i: (i, 0, 0, 0))
    sspec = pl.BlockSpec((tm // LANES, PEER_HEADS, PEER_NKEYS // 2, LANES), lambda i: (i, 0, 0, 0))
    return pl.pallas_call(
        _peer_dense_kernel,
        grid=(T // tm,),
        in_specs=[
            pl.BlockSpec((D_MODEL, tm), lambda i: (0, i)),
            fspec, fspec, sspec, sspec,
            pl.BlockSpec((tm, D_MODEL), lambda i: (i, 0)),
            pl.BlockSpec((1, 1, 1, D_MODEL), g2_map),
            pl.BlockSpec((1, D_MODEL), lambda i: (0, 0)),
            pl.BlockSpec(memory_space=pl.ANY),
            pl.BlockSpec(memory_space=pl.ANY),
        ],
        out_specs=pl.BlockSpec((tm, D_MODEL), lambda i: (i, 0)),
        out_shape=jax.ShapeDtypeStruct((T, D_MODEL), F32),
        scratch_shapes=[
            pltpu.VMEM((U_SLOTS, PEER_SB, D_MODEL), BF16),
            pltpu.VMEM((V_SLOTS, D_MODEL, 2 * PEER_SB), BF16),
            pltpu.SemaphoreType.DMA((U_SLOTS,)),
            pltpu.SemaphoreType.DMA((V_SLOTS,)),
            pltpu.VMEM((D_MODEL, tm), F32),
            pltpu.VMEM((tm // LANES, PEER_SB, LANES), F32),
            pltpu.VMEM((tm // LANES, PEER_SB, LANES), F32),
            pltpu.VMEM((tm // LANES, PEER_SB, LANES), jnp.uint32),
            pltpu.VMEM((tm // LANES, PEER_SB, LANES), jnp.uint32),
        ],
        compiler_params=_params(("parallel",)),
        name="peer_dense",
    )(h2t, a, m, b, r, x1, mods, final_g.reshape(1, D_MODEL), u_bf, vt_sub)


def _trunk(x, mods, h0r, h0i, s5w, p, tables, final_g, per_batch, grid):
    B, L, _ = x.shape
    a_re, a_im, wb, wc_re, wc_im = s5w
    zs2d, zp = _inproj(x, mods, p["norm1_g"], p["w_in"], per_batch)
    zs = zs2d.reshape(L, B, D_SSM)
    yf, sfr, sfi = _s5_direction(zs, a_re, a_im, wb, wc_re, wc_im, h0r[:, 0], h0i[:, 0], 0)
    yb, sbr, sbi = _s5_direction(zs, a_re, a_im, wb, wc_re, wc_im, h0r[:, 1], h0i[:, 1], 1)
    x1, h2t = _mix(x, zs2d, zp, yf.reshape(L, B * D_SSM), yb.reshape(L, B * D_SSM), mods, p["ssm_d"],
                   p["w_glu"], p["w_pool"], p["pool_scale"], p["w_out"], p["norm2_g"], per_batch, grid)
    wq_t, wka, kb, u_bf, vt_bf = tables
    a, m, b, r = _peer_prep(h2t, wq_t, wka, kb)
    y = _peer_dense(h2t, u_bf, vt_bf, a, m, b, r, x1, mods, final_g, L if per_batch else None)
    new_re = jnp.stack([sfr, sbr], axis=1).reshape(B, 2, SSM_G, SSM_P)
    new_im = jnp.stack([sfi, sbi], axis=1).reshape(B, 2, SSM_G, SSM_P)
    return y.reshape(B, L, D_MODEL), new_re, new_im


def kernel(x_prompt, x_sample, state_ssm_re, state_ssm_im, c, c_ctx, norm1_g, w_mod, b_mod, w_in,
           ssm_lambda_re, ssm_lambda_im, ssm_log_dt, ssm_b_re, ssm_b_im, ssm_c_re, ssm_c_im, ssm_d,
           w_glu, w_pool, pool_scale, w_out, norm2_g, peer_wq, peer_subkeys, peer_u, peer_v, final_g):
    depth = w_mod.shape[0]
    assert depth == 1, "single trunk layer"
    l = 0
    n_dec = c.shape[0]
    cond = jnp.zeros((COND_ROWS, D_MODEL), F32).at[:n_dec].set(c).at[CTX_ROW].set(c_ctx)
    mods = _mod_vectors(cond, w_mod[l], b_mod[l]).reshape(COND_ROWS, N_MOD, 1, D_MODEL)

    ar, ai, bbr, bbi = _discretise(ssm_lambda_re[l], ssm_lambda_im[l], ssm_log_dt[l], ssm_b_re[l], ssm_b_im[l])
    s5w = _s5_weights(ar, ai, bbr, bbi, ssm_c_re[l], ssm_c_im[l])

    wq_t, wka, kb = _peer_tables(peer_wq[l], peer_subkeys[l])
    vt_sub = jnp.transpose(peer_v[l].reshape(PEER_N // (2 * PEER_SB), 2 * PEER_SB, D_MODEL), (0, 2, 1)).astype(BF16)
    tables = (wq_t, wka, kb, peer_u[l].astype(BF16), vt_sub)

    p = {"norm1_g": norm1_g[l], "w_in": w_in[l].astype(BF16), "ssm_d": ssm_d[l], "w_glu": w_glu[l],
         "w_pool": w_pool[l], "pool_scale": pool_scale[l], "w_out": w_out[l], "norm2_g": norm2_g[l]}

    bp = x_prompt.shape[0]
    zeros = jnp.zeros((bp, 2, SSM_N), F32)
    y_prompt, new_re, new_im = _trunk(x_prompt, mods, zeros, zeros, s5w, p, tables, final_g, False, False)
    h0r = state_ssm_re[:, l].reshape(n_dec, 2, SSM_N)
    h0i = state_ssm_im[:, l].reshape(n_dec, 2, SSM_N)
    y_sample, _, _ = _trunk(x_sample, mods, h0r, h0i, s5w, p, tables, final_g, True, True)
    return (y_prompt, y_sample, new_re[:, None], new_im[:, None])
```

```python
import functools
import math

import numpy as np
import jax
import jax.numpy as jnp
from jax import lax
from jax.experimental import pallas as pl
from jax.experimental.pallas import tpu as pltpu

F32 = jnp.float32
BF16 = jnp.bfloat16

D_MODEL = 1024
D_SSM = 512
D_POOL = 512
SSM_H = 16
SSM_G = 32
SSM_P = 64
SSM_N = SSM_G * SSM_P
POOL_WINDOWS = (2, 4, 8, 16)
POOL_G = 4
POOL_C = 128
GRID_W = 64
PEER_HEADS = 8
PEER_NKEYS = 128
PEER_N = PEER_NKEYS * PEER_NKEYS
PEER_DHALF = 128
PEER_TOPK = 16
N_MOD = 6
EPS = 1e-6

SUBLANES = 8
LANES = 128
VMEM_LIMIT = 56 * 1024 * 1024

COND_ROWS = 16
CTX_ROW = 8
SEQ_TILE = 256
S5_CHUNK = 128
S5_LANES = 512
PREP_TILE = 256
PEER_TM = 512
PEER_SB = 512
STAGE_UNROLL = 4
U_SLOTS = 6
U_AHEAD = 4
V_SLOTS = 3
RANK_ROWS = 64
GATE_ROWS = 64


def _gelu(x):
    return 0.5 * x * (1.0 + jnp.tanh(0.7978845608028654 * (x + 0.044715 * (x * x * x))))


def _split_bf16(x):
    hi = x.astype(BF16)
    lo = (x - hi.astype(F32)).astype(BF16)
    return hi, lo


def _dot(a, b):
    return jnp.dot(a, b, preferred_element_type=F32)


def _params(sem, flags=None):
    return pltpu.CompilerParams(dimension_semantics=sem, vmem_limit_bytes=VMEM_LIMIT, flags=flags)


def _mod_kernel(cond_ref, whi_ref, wlo_ref, b_ref, o_ref):
    c = cond_ref[...]
    s = c * jax.nn.sigmoid(c)
    shi, slo = _split_bf16(s)
    whi = whi_ref[...]
    o_ref[...] = _dot(shi, whi) + _dot(slo, whi) + _dot(shi, wlo_ref[...]) + b_ref[...]


def _mod_vectors(cond, w_mod, b_mod):
    whi, wlo = _split_bf16(w_mod)
    n = w_mod.shape[1]
    bn = D_MODEL
    return pl.pallas_call(
        _mod_kernel,
        grid=(n // bn,),
        in_specs=[
            pl.BlockSpec((COND_ROWS, D_MODEL), lambda k: (0, 0)),
            pl.BlockSpec((D_MODEL, bn), lambda k: (0, k)),
            pl.BlockSpec((D_MODEL, bn), lambda k: (0, k)),
            pl.BlockSpec((1, bn), lambda k: (0, k)),
        ],
        out_specs=pl.BlockSpec((COND_ROWS, bn), lambda k: (0, k)),
        out_shape=jax.ShapeDtypeStruct((COND_ROWS, n), F32),
        compiler_params=_params(("arbitrary",)),
        name="mod",
    )(cond, whi, wlo, b_mod.reshape(1, n))


def _disc_kernel(lr_ref, li_ref, ldt_ref, br_ref, bi_ref, ar_ref, ai_ref, bbr_ref, bbi_ref):
    lr = lr_ref[...]
    li = li_ref[...]
    dt = jnp.exp(ldt_ref[...])
    mag = jnp.exp(lr * dt)
    ar = mag * jnp.cos(li * dt)
    ai = mag * jnp.sin(li * dt)
    den = lr * lr + li * li
    nr = ar - 1.0
    ni = ai
    fr = (nr * lr + ni * li) / den
    fi = (ni * lr - nr * li) / den
    ar_ref[...] = ar
    ai_ref[...] = ai
    br = br_ref[...]
    bi = bi_ref[...]
    frb = fr[:, None, :]
    fib = fi[:, None, :]
    bbr_ref[...] = frb * br - fib * bi
    bbi_ref[...] = frb * bi + fib * br


def _discretise(lam_re, lam_im, log_dt, b_re, b_im):
    rows = 2 * SSM_G
    lr = lam_re.reshape(rows, SSM_P)
    li = lam_im.reshape(rows, SSM_P)
    ldt = jnp.broadcast_to(log_dt.reshape(rows, 1), (rows, SSM_P))
    br = jnp.swapaxes(b_re, -1, -2).reshape(rows, SSM_H, SSM_P)
    bi = jnp.swapaxes(b_im, -1, -2).reshape(rows, SSM_H, SSM_P)
    small = jax.ShapeDtypeStruct((rows, SSM_P), F32)
    big = jax.ShapeDtypeStruct((rows, SSM_H, SSM_P), F32)
    return pl.pallas_call(_disc_kernel, out_shape=(small, small, big, big), name="disc")(lr, li, ldt, br, bi)


def _s5_weights(ar, ai, bbr, bbi, c_re, c_im):
    half_g = SSM_G // 2
    eye = jnp.eye(half_g, dtype=F32)
    a_re = ar.reshape(2, 1, SSM_N)
    a_im = ai.reshape(2, 1, SSM_N)

    def bmat(t):
        t = t.reshape(2, 2, half_g, SSM_H, SSM_P)
        m = jnp.einsum("djghp,gk->djghkp", t, eye)
        return m.reshape(2, 2, half_g * SSM_H, half_g * SSM_P)

    def cmat(t):
        t = t.reshape(2, 2, half_g, SSM_H, SSM_P)
        m = jnp.einsum("djghp,gk->djgpkh", t, eye)
        return m.reshape(2, 2, half_g * SSM_P, half_g * SSM_H)

    wb = jnp.concatenate([bmat(bbr), bmat(bbi)], axis=-1).astype(BF16)
    wc_re = cmat(c_re).astype(BF16)
    wc_im = cmat(c_im).astype(BF16)
    return a_re, a_im, wb, wc_re, wc_im


def _inproj_kernel(x_ref, sh_ref, sc_ref, g_ref, w_ref, zs_ref, zp_ref):
    x = x_ref[0]
    ms = jnp.mean(x * x, axis=-1, keepdims=True)
    y = x * lax.rsqrt(ms + EPS) * g_ref[...]
    h = y * (1.0 + sc_ref[0, 0]) + sh_ref[0, 0]
    z = _dot(h.astype(BF16), w_ref[...])
    zs_ref[...] = z[:, :D_SSM]
    zp_ref[0] = z[:, D_SSM:]


def _mod_spec(k, per_batch, tokens_per_row=None):
    if per_batch:
        return pl.BlockSpec((1, 1, 1, D_MODEL), lambda b, i: (b, k, 0, 0))
    return pl.BlockSpec((1, 1, 1, D_MODEL), lambda b, i: (CTX_ROW, k, 0, 0))


def _inproj(x, mods, g, w_in, per_batch):
    B, L, _ = x.shape
    tm = SEQ_TILE
    return pl.pallas_call(
        _inproj_kernel,
        grid=(B, L // tm),
        in_specs=[
            pl.BlockSpec((1, tm, D_MODEL), lambda b, i: (b, i, 0)),
            _mod_spec(0, per_batch),
            _mod_spec(1, per_batch),
            pl.BlockSpec((1, D_MODEL), lambda b, i: (0, 0)),
            pl.BlockSpec((D_MODEL, D_MODEL), lambda b, i: (0, 0)),
        ],
        out_specs=[
            pl.BlockSpec((tm, D_SSM), lambda b, i: (i, b)),
            pl.BlockSpec((1, tm, D_POOL), lambda b, i: (b, i, 0)),
        ],
        out_shape=[
            jax.ShapeDtypeStruct((L, B * D_SSM), F32),
            jax.ShapeDtypeStruct((B, L, D_POOL), F32),
        ],
        compiler_params=_params(("parallel", "parallel")),
        name="inproj",
    )(x, mods, mods, g.reshape(1, D_MODEL), w_in)


def _s5_kernel(u_ref, are_ref, aim_ref, wb_ref, wcr_ref, wci_ref, h0r_ref, h0i_ref,
               y_ref, sr_ref, si_ref, bur_ref, bui_ref, cr_ref, ci_ref, *, reverse, n_chunks):
    c = pl.program_id(1)
    rows = S5_CHUNK * SUBLANES
    half = SSM_N // 2
    half_in = D_SSM // 2

    @pl.when(c == 0)
    def _():
        cr_ref[...] = h0r_ref[...]
        ci_ref[...] = h0i_ref[...]

    u = u_ref[...].reshape(rows, D_SSM).astype(BF16)
    for j in range(2):
        r = _dot(u[:, j * half_in:(j + 1) * half_in], wb_ref[0, j])
        bur_ref[:, j * half:(j + 1) * half] = r[:, :half]
        bui_ref[:, j * half:(j + 1) * half] = r[:, half:]

    for lc in range(SSM_N // S5_LANES):
        ls = slice(lc * S5_LANES, (lc + 1) * S5_LANES)
        ar = jnp.broadcast_to(are_ref[0, :, ls], (SUBLANES, S5_LANES))
        ai = jnp.broadcast_to(aim_ref[0, :, ls], (SUBLANES, S5_LANES))

        def step(i, carry, ls=ls, ar=ar, ai=ai):
            hr, hi = carry
            t = (S5_CHUNK - 1 - i) if reverse else i
            row = pl.multiple_of(t * SUBLANES, SUBLANES)
            nhr = ar * hr - ai * hi + bur_ref[pl.ds(row, SUBLANES), ls]
            nhi = ar * hi + ai * hr + bui_ref[pl.ds(row, SUBLANES), ls]
            bur_ref[pl.ds(row, SUBLANES), ls] = nhr
            bui_ref[pl.ds(row, SUBLANES), ls] = nhi
            return nhr, nhi

        hr, hi = lax.fori_loop(0, S5_CHUNK, step, (cr_ref[:, ls], ci_ref[:, ls]), unroll=4)
        cr_ref[:, ls] = hr
        ci_ref[:, ls] = hi

    ys = []
    for j in range(2):
        hs = slice(j * half, (j + 1) * half)
        yr = _dot(bur_ref[:, hs].astype(BF16), wcr_ref[0, j])
        yi = _dot(bui_ref[:, hs].astype(BF16), wci_ref[0, j])
        ys.append(yr - yi)
    y_ref[...] = jnp.concatenate(ys, axis=-1).reshape(S5_CHUNK, SUBLANES, D_SSM)

    @pl.when(c == n_chunks - 1)
    def _():
        sr_ref[...] = cr_ref[...]
        si_ref[...] = ci_ref[...]


def _s5_direction(zs, a_re, a_im, wb, wc_re, wc_im, h0r, h0i, d):
    L, B, _ = zs.shape
    n_chunks = L // S5_CHUNK
    reverse = d == 1
    cidx = (lambda g, c: n_chunks - 1 - c) if reverse else (lambda g, c: c)
    rows = S5_CHUNK * SUBLANES
    kern = functools.partial(_s5_kernel, reverse=reverse, n_chunks=n_chunks)
    state = jax.ShapeDtypeStruct((B, SSM_N), F32)
    return pl.pallas_call(
        kern,
        grid=(B // SUBLANES, n_chunks),
        in_specs=[
            pl.BlockSpec((S5_CHUNK, SUBLANES, D_SSM), lambda g, c: (cidx(g, c), g, 0)),
            pl.BlockSpec((1, 1, SSM_N), lambda g, c: (d, 0, 0)),
            pl.BlockSpec((1, 1, SSM_N), lambda g, c: (d, 0, 0)),
            pl.BlockSpec((1, 2, D_SSM // 2, SSM_N), lambda g, c: (d, 0, 0, 0)),
            pl.BlockSpec((1, 2, SSM_N // 2, D_SSM // 2), lambda g, c: (d, 0, 0, 0)),
            pl.BlockSpec((1, 2, SSM_N // 2, D_SSM // 2), lambda g, c: (d, 0, 0, 0)),
            pl.BlockSpec((SUBLANES, SSM_N), lambda g, c: (g, 0)),
            pl.BlockSpec((SUBLANES, SSM_N), lambda g, c: (g, 0)),
        ],
        out_specs=[
            pl.BlockSpec((S5_CHUNK, SUBLANES, D_SSM), lambda g, c: (cidx(g, c), g, 0)),
            pl.BlockSpec((SUBLANES, SSM_N), lambda g, c: (g, 0)),
            pl.BlockSpec((SUBLANES, SSM_N), lambda g, c: (g, 0)),
        ],
        out_shape=[jax.ShapeDtypeStruct((L, B, D_SSM), F32), state, state],
        scratch_shapes=[
            pltpu.VMEM((rows, SSM_N), F32),
            pltpu.VMEM((rows, SSM_N), F32),
            pltpu.VMEM((SUBLANES, SSM_N), F32),
            pltpu.VMEM((SUBLANES, SSM_N), F32),
        ],
        compiler_params=_params(("parallel", "arbitrary")),
        name="s5_bwd" if reverse else "s5_fwd",
    )(zs, a_re, a_im, wb, wc_re, wc_im, h0r, h0i)


def _pool_tables(grid):
    seg = GRID_W if grid else SEQ_TILE
    t = np.arange(SEQ_TILE)
    pos = t % seg
    base = t - pos
    mats, invs = [], []
    for w in POOL_WINDOWS:
        lo = np.clip(pos - w // 2, 0, seg)
        hi = np.clip(pos + (w - w // 2), 0, seg)
        s = t[None, :]
        m = (s >= (base + lo)[:, None]) & (s < (base + hi)[:, None])
        mats.append(m.astype(np.float32))
        invs.append(np.broadcast_to((1.0 / (hi - lo).astype(np.float32))[:, None], (SEQ_TILE, LANES)))
    return jnp.asarray(np.stack(mats), dtype=BF16), jnp.asarray(np.stack(invs), dtype=F32)


def _mix_kernel(x_ref, u_ref, zp_ref, yf_ref, yb_ref, g1_ref, sh2_ref, sc2_ref, d_ref, wglu_ref,
                pm_ref, pinv_ref, wpool_ref, pscale_ref, wout_ref, n2_ref, x1_ref, h2t_ref):
    x = x_ref[0]
    y = d_ref[...] * u_ref[...] + yf_ref[...] + yb_ref[...]
    g = _gelu(y)
    ys = g * jax.nn.sigmoid(_dot(g.astype(BF16), wglu_ref[...]))

    zp = zp_ref[0]
    parts = [ys]
    for gi in range(POOL_G):
        zg = zp[:, gi * POOL_C:(gi + 1) * POOL_C]
        hi, lo = _split_bf16(zg)
        pm = pm_ref[gi]
        win = _dot(pm, hi) + _dot(pm, lo)
        pooled = win * pinv_ref[gi] - zg
        og = _dot(pooled.astype(BF16), wpool_ref[gi])
        parts.append(og * pscale_ref[:, gi * POOL_C:(gi + 1) * POOL_C])
    mix = _dot(jnp.concatenate(parts, axis=-1).astype(BF16), wout_ref[...])

    x1 = x + g1_ref[0, 0] * mix
    x1_ref[...] = x1
    ms = jnp.mean(x1 * x1, axis=-1, keepdims=True)
    h2 = x1 * lax.rsqrt(ms + EPS) * n2_ref[...]
    h2 = h2 * (1.0 + sc2_ref[0, 0]) + sh2_ref[0, 0]
    h2t_ref[...] = h2.T.astype(BF16)


def _mix(x, zs2d, zp, yf2d, yb2d, mods, ssm_d, w_glu, w_pool, pool_scale, w_out, norm2_g, per_batch, grid):
    B, L, _ = x.shape
    tm = SEQ_TILE
    nt = L // tm
    pm, pinv = _pool_tables(grid)
    tm_spec = pl.BlockSpec((tm, D_SSM), lambda b, i: (i, b))
    const2 = lambda b, i: (0, 0)
    const3 = lambda b, i: (0, 0, 0)
    return pl.pallas_call(
        _mix_kernel,
        grid=(B, nt),
        in_specs=[
            pl.BlockSpec((1, tm, D_MODEL), lambda b, i: (b, i, 0)),
            tm_spec,
            pl.BlockSpec((1, tm, D_POOL), lambda b, i: (b, i, 0)),
            tm_spec,
            tm_spec,
            _mod_spec(2, per_batch),
            _mod_spec(3, per_batch),
            _mod_spec(4, per_batch),
            pl.BlockSpec((1, D_SSM), const2),
            pl.BlockSpec((D_SSM, D_SSM), const2),
            pl.BlockSpec((POOL_G, tm, tm), const3),
            pl.BlockSpec((POOL_G, tm, LANES), const3),
            pl.BlockSpec((POOL_G, POOL_C, POOL_C), const3),
            pl.BlockSpec((1, D_POOL), const2),
            pl.BlockSpec((D_MODEL, D_MODEL), const2),
            pl.BlockSpec((1, D_MODEL), const2),
        ],
        out_specs=[
            pl.BlockSpec((tm, D_MODEL), lambda b, i: (b * nt + i, 0)),
            pl.BlockSpec((D_MODEL, tm), lambda b, i: (0, b * nt + i)),
        ],
        out_shape=[
            jax.ShapeDtypeStruct((B * L, D_MODEL), F32),
            jax.ShapeDtypeStruct((D_MODEL, B * L), BF16),
        ],
        compiler_params=_params(("parallel", "parallel")),
        name="mix",
    )(x, zs2d, zp, yf2d, yb2d, mods, mods, mods, ssm_d.reshape(1, D_SSM), w_glu.astype(BF16),
      pm, pinv, w_pool.astype(BF16), pool_scale.reshape(1, D_POOL), w_out.astype(BF16),
      norm2_g.reshape(1, D_MODEL))


def _sort_pairs(n):
    pairs = []

    def merge(lo, hi, r):
        step = r * 2
        if step < hi - lo:
            merge(lo, hi, step)
            merge(lo + r, hi, step)
            for i in range(lo + r, hi - r, step):
                pairs.append((i, i + r))
        else:
            pairs.append((lo, lo + r))

    def sort(lo, hi):
        if hi - lo >= 1:
            mid = lo + (hi - lo) // 2
            sort(lo, mid)
            sort(mid + 1, hi)
            merge(lo, hi, 1)

    sort(0, n - 1)
    return pairs


_SORT16 = _sort_pairs(PEER_TOPK)


def _vmax(a, b):
    if a is None:
        return b
    if b is None:
        return a
    return jnp.maximum(a, b)


def _vmin(a, b):
    if a is None or b is None:
        return None
    return jnp.minimum(a, b)


def _sort16_desc(w):
    w = list(w)
    for i, j in _SORT16:
        w[i], w[j] = _vmax(w[i], w[j]), _vmin(w[i], w[j])
    return w


def _merge_top16(a, b):
    k = PEER_TOPK
    w = [_vmax(a[i], b[k - 1 - i]) for i in range(k)]
    d = k // 2
    while d >= 1:
        for i in range(k):
            if i & d == 0:
                w[i], w[i + d] = _vmax(w[i], w[i + d]), _vmin(w[i], w[i + d])
        d //= 2
    return w


def _top16_of_128(load):
    cur = None
    for grp in range(PEER_NKEYS // PEER_TOPK):
        s = _sort16_desc([load(grp * PEER_TOPK + i) for i in range(PEER_TOPK)])
        cur = s if cur is None else _merge_top16(cur, s)
    return cur


def _top16_pair_sums(sv0, sv1):
    k = PEER_TOPK
    pad = lambda row: row + [None] * (k - len(row))
    cur = [sv0[0] + sv1[j] for j in range(k)]
    for i in range(1, k // 2):
        cur = _merge_top16(cur, pad([sv0[i] + sv1[j] for j in range(k // (i + 1))]))
    return _merge_top16(cur, pad([sv0[i] + sv1[0] for i in range(k // 2, k)]))


def _count_sorted(v, above):
    sel = jnp.where
    g8 = above(v[7])
    g4 = above(sel(g8, v[11], v[3]))
    g2 = above(sel(g8, sel(g4, v[13], v[9]), sel(g4, v[5], v[1])))
    g1 = above(sel(g8, sel(g4, sel(g2, v[14], v[12]), sel(g2, v[10], v[8])),
                   sel(g4, sel(g2, v[6], v[4]), sel(g2, v[2], v[0]))))
    low = sel(g8, 8.0, 0.0) + sel(g4, 4.0, 0.0) + sel(g2, 2.0, 0.0) + sel(g1, 1.0, 0.0)
    return sel(above(v[15]), 16.0, low)


def _as_words(x):
    return pltpu.bitcast(x.astype(BF16), jnp.uint32)


def _as_bf16(words):
    return pltpu.bitcast(words, BF16)


def _peer_prep_kernel(ht_ref, wq_ref, wka_ref, kb_ref, a_ref, m_ref, b_ref, r_ref, q_ref, sa_ref, st_ref):
    tm = ht_ref.shape[1]
    half = PEER_HEADS * PEER_DHALF
    q_ref[...] = _dot(wq_ref[...], ht_ref[...]).astype(BF16)
    for side in range(2):
        s = _dot(wka_ref[side], q_ref[side * half:(side + 1) * half, :])
        sa_ref[side] = s.reshape(PEER_NKEYS, PEER_HEADS, tm)

    for lc in range(tm // LANES):
        ls = slice(lc * LANES, (lc + 1) * LANES)
        sv0 = _top16_of_128(lambda n: sa_ref[0, n, :, ls])
        sv1 = _top16_of_128(lambda n: sa_ref[1, n, :, ls])
        top = _top16_pair_sums(sv0, sv1)
        tau = top[PEER_TOPK - 1]
        best = sv0[0] + sv1[0]
        z = jnp.exp(top[0] - best)
        for k in range(1, PEER_TOPK):
            z = z + jnp.exp(top[k] - best)
        inv_z = 1.0 / z
        best0 = sv0[0]
        for k in range(PEER_TOPK):
            st_ref[k, :, ls] = sv1[k]

        def first_key(n, carry, lc=lc, ls=ls, sv1=sv1, tau=tau, best0=best0, inv_z=inv_z):
            s0 = sa_ref[0, n, :, ls]
            a_ref[lc, n] = jnp.exp(s0 - best0) * inv_z
            m_ref[lc, n] = _count_sorted(sv1, lambda p: s0 + p >= tau)
            return carry

        lax.fori_loop(0, PEER_NKEYS, first_key, 0, unroll=2)

    for h in range(PEER_HEADS):
        s1 = _dot(kb_ref[h], q_ref[half + h * PEER_DHALF:half + (h + 1) * PEER_DHALF, :])
        for lc in range(tm // LANES):
            ls = slice(lc * LANES, (lc + 1) * LANES)
            sorted1 = [st_ref[k, h:h + 1, ls] for k in range(PEER_TOPK)]
            for part in range(PEER_NKEYS // RANK_ROWS):
                x = s1[part * RANK_ROWS:(part + 1) * RANK_ROWS, ls]
                words = slice(part * RANK_ROWS // 2, (part + 1) * RANK_ROWS // 2)
                b_ref[lc, h, words, :] = _as_words(jnp.exp(x - sorted1[0]))
                r_ref[lc, h, words, :] = _as_words(_count_sorted(sorted1, lambda p: p > x))


def _peer_tables(peer_wq, sub_keys):
    wq = peer_wq.reshape(D_MODEL, PEER_HEADS, 2, PEER_DHALF)
    wq_t = jnp.transpose(wq, (2, 1, 3, 0)).reshape(2 * PEER_HEADS * PEER_DHALF, D_MODEL).astype(BF16)
    eye = jnp.eye(PEER_HEADS, dtype=F32)
    wka = jnp.einsum("hsnk,hg->snhgk", sub_keys, eye)
    wka = wka.reshape(2, PEER_NKEYS * PEER_HEADS, PEER_HEADS * PEER_DHALF).astype(BF16)
    kb = sub_keys[:, 1].astype(BF16)
    return wq_t, wka, kb


def _peer_prep(h2t, wq_t, wka, kb):
    T = h2t.shape[1]
    tm = PREP_TILE
    half = PEER_HEADS * PEER_DHALF
    first = jax.ShapeDtypeStruct((T // LANES, PEER_NKEYS, PEER_HEADS, LANES), F32)
    second = jax.ShapeDtypeStruct((T // LANES, PEER_HEADS, PEER_NKEYS // 2, LANES), jnp.uint32)
    fspec = pl.BlockSpec((tm // LANES, PEER_NKEYS, PEER_HEADS, LANES), lambda i: (i, 0, 0, 0))
    sspec = pl.BlockSpec((tm // LANES, PEER_HEADS, PEER_NKEYS // 2, LANES), lambda i: (i, 0, 0, 0))
    return pl.pallas_call(
        _peer_prep_kernel,
        grid=(T // tm,),
        in_specs=[
            pl.BlockSpec((D_MODEL, tm), lambda i: (0, i)),
            pl.BlockSpec((2 * half, D_MODEL), lambda i: (0, 0)),
            pl.BlockSpec((2, PEER_NKEYS * PEER_HEADS, half), lambda i: (0, 0, 0)),
            pl.BlockSpec((PEER_HEADS, PEER_NKEYS, PEER_DHALF), lambda i: (0, 0, 0)),
        ],
        out_specs=[fspec, fspec, sspec, sspec],
        out_shape=[first, first, second, second],
        scratch_shapes=[
            pltpu.VMEM((2 * half, tm), BF16),
            pltpu.VMEM((2, PEER_NKEYS, PEER_HEADS, tm), F32),
            pltpu.VMEM((PEER_TOPK, PEER_HEADS, tm), F32),
        ],
        compiler_params=_params(("parallel",)),
        name="peer_prep",
    )(h2t, wq_t, wka, kb)


def _peer_dense_kernel(ht_ref, a_ref, m_ref, b_ref, r_ref, x1_ref, g2_ref, fg_ref, u_hbm, vt_hbm,
                       y_ref, ubuf, vbuf, usem, vsem, acc_ref, act0_ref, act1_ref, w0_ref, w1_ref):
    act_refs = (act0_ref, act1_ref)
    w_refs = (w0_ref, w1_ref)
    tm = ht_ref.shape[1]
    n_chunks = tm // LANES
    n_sub = PEER_N // PEER_SB
    n_pairs = n_sub // 2
    keys_per_sub = PEER_SB // PEER_NKEYS
    sub_words = PEER_SB // 2
    half_rows = D_MODEL // 2

    def u_copy(s):
        slot = s % U_SLOTS
        return pltpu.make_async_copy(u_hbm.at[pl.ds(s * PEER_SB, PEER_SB), :], ubuf.at[slot], usem.at[slot])

    def v_copy(pair):
        slot = pair % V_SLOTS
        return pltpu.make_async_copy(vt_hbm.at[pair], vbuf.at[slot], vsem.at[slot])

    def activations(s, aslot):
        act = _dot(ubuf[s % U_SLOTS], ht_ref[...])
        for tc in range(n_chunks):
            act_refs[aslot][tc] = act[:, tc * LANES:(tc + 1) * LANES]

    def gated(s, aslot, wslot, pos):
        for tc in range(n_chunks):
            a_rows = [a_ref[tc, s * keys_per_sub + k] for k in range(keys_per_sub)]
            m_rows = [m_ref[tc, s * keys_per_sub + k] for k in range(keys_per_sub)]
            for part in range(PEER_NKEYS // GATE_ROWS):
                pwords = slice(part * GATE_ROWS // 2, (part + 1) * GATE_ROWS // 2)
                gates = [None] * keys_per_sub
                for h in range(PEER_HEADS):
                    b = _as_bf16(b_ref[tc, h, pwords, :])
                    r = _as_bf16(r_ref[tc, h, pwords, :])
                    for k in range(keys_per_sub):
                        a_row = jnp.broadcast_to(a_rows[k][h:h + 1, :], (GATE_ROWS, LANES)).astype(BF16)
                        m_row = jnp.broadcast_to(m_rows[k][h:h + 1, :], (GATE_ROWS, LANES)).astype(BF16)
                        term = jnp.where(r < m_row, b, jnp.zeros_like(b)) * a_row
                        gates[k] = term if gates[k] is None else gates[k] + term
                for k in range(keys_per_sub):
                    row0 = k * PEER_NKEYS + part * GATE_ROWS
                    g = _gelu(act_refs[aslot][tc, row0:row0 + GATE_ROWS, :].astype(BF16))
                    word0 = pos * sub_words + row0 // 2
                    w_refs[wslot][tc, word0:word0 + GATE_ROWS // 2, :] = pltpu.bitcast(g * gates[k], jnp.uint32)

    def accumulate(pair, wslot, half):
        w = _as_bf16(jnp.concatenate([w_refs[wslot][tc] for tc in range(n_chunks)], axis=1))
        rows = slice(half * half_rows, (half + 1) * half_rows)
        acc_ref[rows, :] += _dot(vbuf[pair % V_SLOTS, rows, :], w)

    def stage_dma(s, jj):
        static = isinstance(s, int)

        def when(cond, fn):
            if static:
                if cond:
                    fn()
            else:
                pl.when(cond)(fn)

        when(s + U_AHEAD < n_sub, lambda: u_copy(s + U_AHEAD).start())
        if jj % 2 == 0:
            when((s + 1) // 2 < n_pairs, lambda: v_copy((s + 1) // 2).start())
        when(s < n_sub, lambda: u_copy(s).wait())
        if jj % 2 == 0:
            when(s >= 3, lambda: v_copy((s - 3) // 2).wait())

    def stage_compute(s, jj, with_activations=True, with_gate=True):
        if with_activations:
            activations(s, (jj + 1) % 2)
        if with_gate:
            gated(s - 1, jj % 2, (jj // 2) % 2, jj % 2)
        accumulate((s - 3) // 2, (jj // 2 + 1) % 2, jj % 2)

    acc_ref[...] = jnp.zeros_like(acc_ref)
    w_refs[1][...] = jnp.zeros_like(w_refs[1])
    vbuf[(-1) % V_SLOTS] = jnp.zeros(vbuf.shape[1:], vbuf.dtype)
    for s in range(U_AHEAD + 1):
        u_copy(s).start()
    v_copy(0).start()
    u_copy(0).wait()
    activations(0, 0)

    def stages(t, carry):
        for jj in range(STAGE_UNROLL):
            s = 1 + t * STAGE_UNROLL + jj
            stage_dma(s, jj)
            stage_compute(s, jj)
        return carry

    lax.fori_loop(0, n_sub // STAGE_UNROLL, stages, 0)
    for s in (n_sub + 1, n_sub + 2):
        jj = (s - 1) % STAGE_UNROLL
        stage_dma(s, jj)
        stage_compute(s, jj, with_activations=False, with_gate=False)

    out = x1_ref[...] + g2_ref[0, 0] * acc_ref[...].T
    ms = jnp.mean(out * out, axis=-1, keepdims=True)
    y_ref[...] = out * lax.rsqrt(ms + EPS) * fg_ref[...]


def _peer_dense(h2t, u_bf, vt_sub, a, m, b, r, x1, mods, final_g, tokens_per_row):
    T = h2t.shape[1]
    tm = PEER_TM
    if tokens_per_row is None:
        g2_map = lambda i: (CTX_ROW, 5, 0, 0)
    else:
        g2_map = lambda i: ((i * tm) // tokens_per_row, 5, 0, 0)
    fspec = pl.BlockSpec((tm // LANES, PEER_NKEYS, PEER_HEADS, LANES), lambda i: (i, 0, 0, 0))
    sspec = pl.BlockSpec((tm // LANES, PEER_HEADS, PEER_NKEYS // 2, LANES), lambda i: (i, 0, 0, 0))
    return pl.pallas_call(
        _peer_dense_kernel,
        grid=(T // tm,),
        in_specs=[
            pl.BlockSpec((D_MODEL, tm), lambda i: (0, i)),
            fspec, fspec, sspec, sspec,
            pl.BlockSpec((tm, D_MODEL), lambda i: (i, 0)),
            pl.BlockSpec((1, 1, 1, D_MODEL), g2_map),
            pl.BlockSpec((1, D_MODEL), lambda i: (0, 0)),
            pl.BlockSpec(memory_space=pl.ANY),
            pl.BlockSpec(memory_space=pl.ANY),
        ],
        out_specs=pl.BlockSpec((tm, D_MODEL), lambda i: (i, 0)),
        out_shape=jax.ShapeDtypeStruct((T, D_MODEL), F32),
        scratch_shapes=[
            pltpu.VMEM((U_SLOTS, PEER_SB, D_MODEL), BF16),
            pltpu.VMEM((V_SLOTS, D_MODEL, 2 * PEER_SB), BF16),
            pltpu.SemaphoreType.DMA((U_SLOTS,)),
            pltpu.SemaphoreType.DMA((V_SLOTS,)),
            pltpu.VMEM((D_MODEL, tm), F32),
            pltpu.VMEM((tm // LANES, PEER_SB, LANES), F32),
            pltpu.VMEM((tm // LANES, PEER_SB, LANES), F32),
            pltpu.VMEM((tm // LANES, PEER_SB, LANES), jnp.uint32),
            pltpu.VMEM((tm // LANES, PEER_SB, LANES), jnp.uint32),
        ],
        compiler_params=_params(("parallel",)),
        name="peer_dense",
    )(h2t, a, m, b, r, x1, mods, final_g.reshape(1, D_MODEL), u_bf, vt_sub)


def _trunk(x, mods, h0r, h0i, s5w, p, tables, final_g, per_batch, grid):
    B, L, _ = x.shape
    a_re, a_im, wb, wc_re, wc_im = s5w
    zs2d, zp = _inproj(x, mods, p["norm1_g"], p["w_in"], per_batch)
    zs = zs2d.reshape(L, B, D_SSM)
    yf, sfr, sfi = _s5_direction(zs, a_re, a_im, wb, wc_re, wc_im, h0r[:, 0], h0i[:, 0], 0)
    yb, sbr, sbi = _s5_direction(zs, a_re, a_im, wb, wc_re, wc_im, h0r[:, 1], h0i[:, 1], 1)
    x1, h2t = _mix(x, zs2d, zp, yf.reshape(L, B * D_SSM), yb.reshape(L, B * D_SSM), mods, p["ssm_d"],
                   p["w_glu"], p["w_pool"], p["pool_scale"], p["w_out"], p["norm2_g"], per_batch, grid)
    wq_t, wka, kb, u_bf, vt_bf = tables
    a, m, b, r = _peer_prep(h2t, wq_t, wka, kb)
    y = _peer_dense(h2t, u_bf, vt_bf, a, m, b, r, x1, mods, final_g, L if per_batch else None)
    new_re = jnp.stack([sfr, sbr], axis=1).reshape(B, 2, SSM_G, SSM_P)
    new_im = jnp.stack([sfi, sbi], axis=1).reshape(B, 2, SSM_G, SSM_P)
    return y.reshape(B, L, D_MODEL), new_re, new_im


def kernel(x_prompt, x_sample, state_ssm_re, state_ssm_im, c, c_ctx, norm1_g, w_mod, b_mod, w_in,
           ssm_lambda_re, ssm_lambda_im, ssm_log_dt, ssm_b_re, ssm_b_im, ssm_c_re, ssm_c_im, ssm_d,
           w_glu, w_pool, pool_scale, w_out, norm2_g, peer_wq, peer_subkeys, peer_u, peer_v, final_g):
    depth = w_mod.shape[0]
    assert depth == 1, "single trunk layer"
    l = 0
    n_dec = c.shape[0]
    cond = jnp.zeros((COND_ROWS, D_MODEL), F32).at[:n_dec].set(c).at[CTX_ROW].set(c_ctx)
    mods = _mod_vectors(cond, w_mod[l], b_mod[l]).reshape(COND_ROWS, N_MOD, 1, D_MODEL)

    ar, ai, bbr, bbi = _discretise(ssm_lambda_re[l], ssm_lambda_im[l], ssm_log_dt[l], ssm_b_re[l], ssm_b_im[l])
    s5w = _s5_weights(ar, ai, bbr, bbi, ssm_c_re[l], ssm_c_im[l])

    wq_t, wka, kb = _peer_tables(peer_wq[l], peer_subkeys[l])
    vt_sub = jnp.transpose(peer_v[l].reshape(PEER_N // (2 * PEER_SB), 2 * PEER_SB, D_MODEL), (0, 2, 1)).astype(BF16)
    tables = (wq_t, wka, kb, peer_u[l].astype(BF16), vt_sub)

    p = {"norm1_g": norm1_g[l], "w_in": w_in[l].astype(BF16), "ssm_d": ssm_d[l], "w_glu": w_glu[l],
         "w_pool": w_pool[l], "pool_scale": pool_scale[l], "w_out": w_out[l], "norm2_g": norm2_g[l]}

    bp = x_prompt.shape[0]
    zeros = jnp.zeros((bp, 2, SSM_N), F32)
    y_prompt, new_re, new_im = _trunk(x_prompt, mods, zeros, zeros, s5w, p, tables, final_g, False, False)
    h0r = state_ssm_re[:, l].reshape(n_dec, 2, SSM_N)
    h0i = state_ssm_im[:, l].reshape(n_dec, 2, SSM_N)
    y_sample, _, _ = _trunk(x_sample, mods, h0r, h0i, s5w, p, tables, final_g, True, True)
    return (y_prompt, y_sample, new_re[:, None], new_im[:, None])
```

```python
import functools
import math

import numpy as np
import jax
import jax.numpy as jnp
from jax import lax
from jax.experimental import pallas as pl
from jax.experimental.pallas import tpu as pltpu

F32 = jnp.float32
BF16 = jnp.bfloat16

D_MODEL = 1024
D_SSM = 512
D_POOL = 512
SSM_H = 16
SSM_G = 32
SSM_P = 64
SSM_N = SSM_G * SSM_P
POOL_WINDOWS = (2, 4, 8, 16)
POOL_G = 4
POOL_C = 128
GRID_W = 64
PEER_HEADS = 8
PEER_NKEYS = 128
PEER_N = PEER_NKEYS * PEER_NKEYS
PEER_DHALF = 128
PEER_TOPK = 16
N_MOD = 6
EPS = 1e-6

SUBLANES = 8
LANES = 128
VMEM_LIMIT = 56 * 1024 * 1024

COND_ROWS = 16
CTX_ROW = 8
SEQ_TILE = 256
S5_CHUNK = 128
S5_LANES = 512
PREP_TILE = 512
PEER_TM = 512
PEER_SB = 512
STAGE_UNROLL = 4
U_SLOTS = 6
U_AHEAD = 4
V_SLOTS = 3
RANK_ROWS = 64
GATE_ROWS = 64


def _gelu(x):
    return 0.5 * x * (1.0 + jnp.tanh(0.7978845608028654 * (x + 0.044715 * (x * x * x))))


def _split_bf16(x):
    hi = x.astype(BF16)
    lo = (x - hi.astype(F32)).astype(BF16)
    return hi, lo


def _dot(a, b):
    return jnp.dot(a, b, preferred_element_type=F32)


def _params(sem, flags=None):
    return pltpu.CompilerParams(dimension_semantics=sem, vmem_limit_bytes=VMEM_LIMIT, flags=flags)


def _mod_kernel(cond_ref, whi_ref, wlo_ref, b_ref, o_ref):
    c = cond_ref[...]
    s = c * jax.nn.sigmoid(c)
    shi, slo = _split_bf16(s)
    whi = whi_ref[...]
    o_ref[...] = _dot(shi, whi) + _dot(slo, whi) + _dot(shi, wlo_ref[...]) + b_ref[...]


def _mod_vectors(cond, w_mod, b_mod):
    whi, wlo = _split_bf16(w_mod)
    n = w_mod.shape[1]
    bn = D_MODEL
    return pl.pallas_call(
        _mod_kernel,
        grid=(n // bn,),
        in_specs=[
            pl.BlockSpec((COND_ROWS, D_MODEL), lambda k: (0, 0)),
            pl.BlockSpec((D_MODEL, bn), lambda k: (0, k)),
            pl.BlockSpec((D_MODEL, bn), lambda k: (0, k)),
            pl.BlockSpec((1, bn), lambda k: (0, k)),
        ],
        out_specs=pl.BlockSpec((COND_ROWS, bn), lambda k: (0, k)),
        out_shape=jax.ShapeDtypeStruct((COND_ROWS, n), F32),
        compiler_params=_params(("arbitrary",)),
        name="mod",
    )(cond, whi, wlo, b_mod.reshape(1, n))


def _disc_kernel(lr_ref, li_ref, ldt_ref, br_ref, bi_ref, ar_ref, ai_ref, bbr_ref, bbi_ref):
    lr = lr_ref[...]
    li = li_ref[...]
    dt = jnp.exp(ldt_ref[...])
    mag = jnp.exp(lr * dt)
    ar = mag * jnp.cos(li * dt)
    ai = mag * jnp.sin(li * dt)
    den = lr * lr + li * li
    nr = ar - 1.0
    ni = ai
    fr = (nr * lr + ni * li) / den
    fi = (ni * lr - nr * li) / den
    ar_ref[...] = ar
    ai_ref[...] = ai
    br = br_ref[...]
    bi = bi_ref[...]
    frb = fr[:, None, :]
    fib = fi[:, None, :]
    bbr_ref[...] = frb * br - fib * bi
    bbi_ref[...] = frb * bi + fib * br


def _discretise(lam_re, lam_im, log_dt, b_re, b_im):
    rows = 2 * SSM_G
    lr = lam_re.reshape(rows, SSM_P)
    li = lam_im.reshape(rows, SSM_P)
    ldt = jnp.broadcast_to(log_dt.reshape(rows, 1), (rows, SSM_P))
    br = jnp.swapaxes(b_re, -1, -2).reshape(rows, SSM_H, SSM_P)
    bi = jnp.swapaxes(b_im, -1, -2).reshape(rows, SSM_H, SSM_P)
    small = jax.ShapeDtypeStruct((rows, SSM_P), F32)
    big = jax.ShapeDtypeStruct((rows, SSM_H, SSM_P), F32)
    return pl.pallas_call(_disc_kernel, out_shape=(small, small, big, big), name="disc")(lr, li, ldt, br, bi)


def _s5_weights(ar, ai, bbr, bbi, c_re, c_im):
    half_g = SSM_G // 2
    eye = jnp.eye(half_g, dtype=F32)
    a_re = ar.reshape(2, 1, SSM_N)
    a_im = ai.reshape(2, 1, SSM_N)

    def bmat(t):
        t = t.reshape(2, 2, half_g, SSM_H, SSM_P)
        m = jnp.einsum("djghp,gk->djghkp", t, eye)
        return m.reshape(2, 2, half_g * SSM_H, half_g * SSM_P)

    def cmat(t):
        t = t.reshape(2, 2, half_g, SSM_H, SSM_P)
        m = jnp.einsum("djghp,gk->djgpkh", t, eye)
        return m.reshape(2, 2, half_g * SSM_P, half_g * SSM_H)

    wb = jnp.concatenate([bmat(bbr), bmat(bbi)], axis=-1).astype(BF16)
    wc_re = cmat(c_re).astype(BF16)
    wc_im = cmat(c_im).astype(BF16)
    return a_re, a_im, wb, wc_re, wc_im


def _inproj_kernel(x_ref, sh_ref, sc_ref, g_ref, w_ref, zs_ref, zp_ref):
    x = x_ref[0]
    ms = jnp.mean(x * x, axis=-1, keepdims=True)
    y = x * lax.rsqrt(ms + EPS) * g_ref[...]
    h = y * (1.0 + sc_ref[0, 0]) + sh_ref[0, 0]
    z = _dot(h.astype(BF16), w_ref[...])
    zs_ref[...] = z[:, :D_SSM]
    zp_ref[0] = z[:, D_SSM:]


def _mod_spec(k, per_batch, tokens_per_row=None):
    if per_batch:
        return pl.BlockSpec((1, 1, 1, D_MODEL), lambda b, i: (b, k, 0, 0))
    return pl.BlockSpec((1, 1, 1, D_MODEL), lambda b, i: (CTX_ROW, k, 0, 0))


def _inproj(x, mods, g, w_in, per_batch):
    B, L, _ = x.shape
    tm = SEQ_TILE
    return pl.pallas_call(
        _inproj_kernel,
        grid=(B, L // tm),
        in_specs=[
            pl.BlockSpec((1, tm, D_MODEL), lambda b, i: (b, i, 0)),
            _mod_spec(0, per_batch),
            _mod_spec(1, per_batch),
            pl.BlockSpec((1, D_MODEL), lambda b, i: (0, 0)),
            pl.BlockSpec((D_MODEL, D_MODEL), lambda b, i: (0, 0)),
        ],
        out_specs=[
            pl.BlockSpec((tm, D_SSM), lambda b, i: (i, b)),
            pl.BlockSpec((1, tm, D_POOL), lambda b, i: (b, i, 0)),
        ],
        out_shape=[
            jax.ShapeDtypeStruct((L, B * D_SSM), F32),
            jax.ShapeDtypeStruct((B, L, D_POOL), F32),
        ],
        compiler_params=_params(("parallel", "parallel")),
        name="inproj",
    )(x, mods, mods, g.reshape(1, D_MODEL), w_in)


def _s5_kernel(u_ref, are_ref, aim_ref, wb_ref, wcr_ref, wci_ref, h0r_ref, h0i_ref,
               y_ref, sr_ref, si_ref, bur_ref, bui_ref, cr_ref, ci_ref, *, reverse, n_chunks):
    c = pl.program_id(1)
    rows = S5_CHUNK * SUBLANES
    half = SSM_N // 2
    half_in = D_SSM // 2

    @pl.when(c == 0)
    def _():
        cr_ref[...] = h0r_ref[...]
        ci_ref[...] = h0i_ref[...]

    u = u_ref[...].reshape(rows, D_SSM).astype(BF16)
    for j in range(2):
        r = _dot(u[:, j * half_in:(j + 1) * half_in], wb_ref[0, j])
        bur_ref[:, j * half:(j + 1) * half] = r[:, :half]
        bui_ref[:, j * half:(j + 1) * half] = r[:, half:]

    for lc in range(SSM_N // S5_LANES):
        ls = slice(lc * S5_LANES, (lc + 1) * S5_LANES)
        ar = jnp.broadcast_to(are_ref[0, :, ls], (SUBLANES, S5_LANES))
        ai = jnp.broadcast_to(aim_ref[0, :, ls], (SUBLANES, S5_LANES))

        def step(i, carry, ls=ls, ar=ar, ai=ai):
            hr, hi = carry
            t = (S5_CHUNK - 1 - i) if reverse else i
            row = pl.multiple_of(t * SUBLANES, SUBLANES)
            nhr = ar * hr - ai * hi + bur_ref[pl.ds(row, SUBLANES), ls]
            nhi = ar * hi + ai * hr + bui_ref[pl.ds(row, SUBLANES), ls]
            bur_ref[pl.ds(row, SUBLANES), ls] = nhr
            bui_ref[pl.ds(row, SUBLANES), ls] = nhi
            return nhr, nhi

        hr, hi = lax.fori_loop(0, S5_CHUNK, step, (cr_ref[:, ls], ci_ref[:, ls]), unroll=4)
        cr_ref[:, ls] = hr
        ci_ref[:, ls] = hi

    ys = []
    for j in range(2):
        hs = slice(j * half, (j + 1) * half)
        yr = _dot(bur_ref[:, hs].astype(BF16), wcr_ref[0, j])
        yi = _dot(bui_ref[:, hs].astype(BF16), wci_ref[0, j])
        ys.append(yr - yi)
    y_ref[...] = jnp.concatenate(ys, axis=-1).reshape(S5_CHUNK, SUBLANES, D_SSM)

    @pl.when(c == n_chunks - 1)
    def _():
        sr_ref[...] = cr_ref[...]
        si_ref[...] = ci_ref[...]


def _s5_direction(zs, a_re, a_im, wb, wc_re, wc_im, h0r, h0i, d):
    L, B, _ = zs.shape
    n_chunks = L // S5_CHUNK
    reverse = d == 1
    cidx = (lambda g, c: n_chunks - 1 - c) if reverse else (lambda g, c: c)
    rows = S5_CHUNK * SUBLANES
    kern = functools.partial(_s5_kernel, reverse=reverse, n_chunks=n_chunks)
    state = jax.ShapeDtypeStruct((B, SSM_N), F32)
    return pl.pallas_call(
        kern,
        grid=(B // SUBLANES, n_chunks),
        in_specs=[
            pl.BlockSpec((S5_CHUNK, SUBLANES, D_SSM), lambda g, c: (cidx(g, c), g, 0)),
            pl.BlockSpec((1, 1, SSM_N), lambda g, c: (d, 0, 0)),
            pl.BlockSpec((1, 1, SSM_N), lambda g, c: (d, 0, 0)),
            pl.BlockSpec((1, 2, D_SSM // 2, SSM_N), lambda g, c: (d, 0, 0, 0)),
            pl.BlockSpec((1, 2, SSM_N // 2, D_SSM // 2), lambda g, c: (d, 0, 0, 0)),
            pl.BlockSpec((1, 2, SSM_N // 2, D_SSM // 2), lambda g, c: (d, 0, 0, 0)),
            pl.BlockSpec((SUBLANES, SSM_N), lambda g, c: (g, 0)),
            pl.BlockSpec((SUBLANES, SSM_N), lambda g, c: (g, 0)),
        ],
        out_specs=[
            pl.BlockSpec((S5_CHUNK, SUBLANES, D_SSM), lambda g, c: (cidx(g, c), g, 0)),
            pl.BlockSpec((SUBLANES, SSM_N), lambda g, c: (g, 0)),
            pl.BlockSpec((SUBLANES, SSM_N), lambda g, c: (g, 0)),
        ],
        out_shape=[jax.ShapeDtypeStruct((L, B, D_SSM), F32), state, state],
        scratch_shapes=[
            pltpu.VMEM((rows, SSM_N), F32),
            pltpu.VMEM((rows, SSM_N), F32),
            pltpu.VMEM((SUBLANES, SSM_N), F32),
            pltpu.VMEM((SUBLANES, SSM_N), F32),
        ],
        compiler_params=_params(("parallel", "arbitrary")),
        name="s5_bwd" if reverse else "s5_fwd",
    )(zs, a_re, a_im, wb, wc_re, wc_im, h0r, h0i)


def _pool_tables(grid):
    seg = GRID_W if grid else SEQ_TILE
    t = np.arange(SEQ_TILE)
    pos = t % seg
    base = t - pos
    mats, invs = [], []
    for w in POOL_WINDOWS:
        lo = np.clip(pos - w // 2, 0, seg)
        hi = np.clip(pos + (w - w // 2), 0, seg)
        s = t[None, :]
        m = (s >= (base + lo)[:, None]) & (s < (base + hi)[:, None])
        mats.append(m.astype(np.float32))
        invs.append(np.broadcast_to((1.0 / (hi - lo).astype(np.float32))[:, None], (SEQ_TILE, LANES)))
    return jnp.asarray(np.stack(mats), dtype=BF16), jnp.asarray(np.stack(invs), dtype=F32)


def _mix_kernel(x_ref, u_ref, zp_ref, yf_ref, yb_ref, g1_ref, sh2_ref, sc2_ref, d_ref, wglu_ref,
                pm_ref, pinv_ref, wpool_ref, pscale_ref, wout_ref, n2_ref, x1_ref, h2t_ref):
    x = x_ref[0]
    y = d_ref[...] * u_ref[...] + yf_ref[...] + yb_ref[...]
    g = _gelu(y)
    ys = g * jax.nn.sigmoid(_dot(g.astype(BF16), wglu_ref[...]))

    zp = zp_ref[0]
    parts = [ys]
    for gi in range(POOL_G):
        zg = zp[:, gi * POOL_C:(gi + 1) * POOL_C]
        hi, lo = _split_bf16(zg)
        pm = pm_ref[gi]
        win = _dot(pm, hi) + _dot(pm, lo)
        pooled = win * pinv_ref[gi] - zg
        og = _dot(pooled.astype(BF16), wpool_ref[gi])
        parts.append(og * pscale_ref[:, gi * POOL_C:(gi + 1) * POOL_C])
    mix = _dot(jnp.concatenate(parts, axis=-1).astype(BF16), wout_ref[...])

    x1 = x + g1_ref[0, 0] * mix
    x1_ref[...] = x1
    ms = jnp.mean(x1 * x1, axis=-1, keepdims=True)
    h2 = x1 * lax.rsqrt(ms + EPS) * n2_ref[...]
    h2 = h2 * (1.0 + sc2_ref[0, 0]) + sh2_ref[0, 0]
    h2t_ref[...] = h2.T.astype(BF16)


def _mix(x, zs2d, zp, yf2d, yb2d, mods, ssm_d, w_glu, w_pool, pool_scale, w_out, norm2_g, per_batch, grid):
    B, L, _ = x.shape
    tm = SEQ_TILE
    nt = L // tm
    pm, pinv = _pool_tables(grid)
    tm_spec = pl.BlockSpec((tm, D_SSM), lambda b, i: (i, b))
    const2 = lambda b, i: (0, 0)
    const3 = lambda b, i: (0, 0, 0)
    return pl.pallas_call(
        _mix_kernel,
        grid=(B, nt),
        in_specs=[
            pl.BlockSpec((1, tm, D_MODEL), lambda b, i: (b, i, 0)),
            tm_spec,
            pl.BlockSpec((1, tm, D_POOL), lambda b, i: (b, i, 0)),
            tm_spec,
            tm_spec,
            _mod_spec(2, per_batch),
            _mod_spec(3, per_batch),
            _mod_spec(4, per_batch),
            pl.BlockSpec((1, D_SSM), const2),
            pl.BlockSpec((D_SSM, D_SSM), const2),
            pl.BlockSpec((POOL_G, tm, tm), const3),
            pl.BlockSpec((POOL_G, tm, LANES), const3),
            pl.BlockSpec((POOL_G, POOL_C, POOL_C), const3),
            pl.BlockSpec((1, D_POOL), const2),
            pl.BlockSpec((D_MODEL, D_MODEL), const2),
            pl.BlockSpec((1, D_MODEL), const2),
        ],
        out_specs=[
            pl.BlockSpec((tm, D_MODEL), lambda b, i: (b * nt + i, 0)),
            pl.BlockSpec((D_MODEL, tm), lambda b, i: (0, b * nt + i)),
        ],
        out_shape=[
            jax.ShapeDtypeStruct((B * L, D_MODEL), F32),
            jax.ShapeDtypeStruct((D_MODEL, B * L), BF16),
        ],
        compiler_params=_params(("parallel", "parallel")),
        name="mix",
    )(x, zs2d, zp, yf2d, yb2d, mods, mods, mods, ssm_d.reshape(1, D_SSM), w_glu.astype(BF16),
      pm, pinv, w_pool.astype(BF16), pool_scale.reshape(1, D_POOL), w_out.astype(BF16),
      norm2_g.reshape(1, D_MODEL))


def _sort_pairs(n):
    pairs = []

    def merge(lo, hi, r):
        step = r * 2
        if step < hi - lo:
            merge(lo, hi, step)
            merge(lo + r, hi, step)
            for i in range(lo + r, hi - r, step):
                pairs.append((i, i + r))
        else:
            pairs.append((lo, lo + r))

    def sort(lo, hi):
        if hi - lo >= 1:
            mid = lo + (hi - lo) // 2
            sort(lo, mid)
            sort(mid + 1, hi)
            merge(lo, hi, 1)

    sort(0, n - 1)
    return pairs


_SORT16 = _sort_pairs(PEER_TOPK)


def _vmax(a, b):
    if a is None:
        return b
    if b is None:
        return a
    return jnp.maximum(a, b)


def _vmin(a, b):
    if a is None or b is None:
        return None
    return jnp.minimum(a, b)


def _sort16_desc(w):
    w = list(w)
    for i, j in _SORT16:
        w[i], w[j] = _vmax(w[i], w[j]), _vmin(w[i], w[j])
    return w


def _merge_top16(a, b):
    k = PEER_TOPK
    w = [_vmax(a[i], b[k - 1 - i]) for i in range(k)]
    d = k // 2
    while d >= 1:
        for i in range(k):
            if i & d == 0:
                w[i], w[i + d] = _vmax(w[i], w[i + d]), _vmin(w[i], w[i + d])
        d //= 2
    return w


def _top16_of_128(load):
    cur = None
    for grp in range(PEER_NKEYS // PEER_TOPK):
        s = _sort16_desc([load(grp * PEER_TOPK + i) for i in range(PEER_TOPK)])
        cur = s if cur is None else _merge_top16(cur, s)
    return cur


def _top16_pair_sums(sv0, sv1):
    k = PEER_TOPK
    pad = lambda row: row + [None] * (k - len(row))
    cur = [sv0[0] + sv1[j] for j in range(k)]
    for i in range(1, k // 2):
        cur = _merge_top16(cur, pad([sv0[i] + sv1[j] for j in range(k // (i + 1))]))
    return _merge_top16(cur, pad([sv0[i] + sv1[0] for i in range(k // 2, k)]))


def _count_sorted(v, above):
    sel = jnp.where
    g8 = above(v[7])
    g4 = above(sel(g8, v[11], v[3]))
    g2 = above(sel(g8, sel(g4, v[13], v[9]), sel(g4, v[5], v[1])))
    g1 = above(sel(g8, sel(g4, sel(g2, v[14], v[12]), sel(g2, v[10], v[8])),
                   sel(g4, sel(g2, v[6], v[4]), sel(g2, v[2], v[0]))))
    low = sel(g8, 8.0, 0.0) + sel(g4, 4.0, 0.0) + sel(g2, 2.0, 0.0) + sel(g1, 1.0, 0.0)
    return sel(above(v[15]), 16.0, low)


def _as_words(x):
    return pltpu.bitcast(x.astype(BF16), jnp.uint32)


def _as_bf16(words):
    return pltpu.bitcast(words, BF16)


def _peer_prep_kernel(ht_ref, wq_ref, wka_ref, kb_ref, a_ref, m_ref, b_ref, r_ref, q_ref, sa_ref, st_ref):
    tm = ht_ref.shape[1]
    half = PEER_HEADS * PEER_DHALF
    q_ref[...] = _dot(wq_ref[...], ht_ref[...]).astype(BF16)
    for side in range(2):
        s = _dot(wka_ref[side], q_ref[side * half:(side + 1) * half, :])
        sa_ref[side] = s.reshape(PEER_NKEYS, PEER_HEADS, tm)

    for lc in range(tm // LANES):
        ls = slice(lc * LANES, (lc + 1) * LANES)
        sv0 = _top16_of_128(lambda n: sa_ref[0, n, :, ls])
        sv1 = _top16_of_128(lambda n: sa_ref[1, n, :, ls])
        top = _top16_pair_sums(sv0, sv1)
        tau = top[PEER_TOPK - 1]
        best = sv0[0] + sv1[0]
        z = jnp.exp(top[0] - best)
        for k in range(1, PEER_TOPK):
            z = z + jnp.exp(top[k] - best)
        inv_z = 1.0 / z
        best0 = sv0[0]
        for k in range(PEER_TOPK):
            st_ref[k, :, ls] = sv1[k]

        def first_key(n, carry, lc=lc, ls=ls, sv1=sv1, tau=tau, best0=best0, inv_z=inv_z):
            s0 = sa_ref[0, n, :, ls]
            a_ref[lc, n] = jnp.exp(s0 - best0) * inv_z
            m_ref[lc, n] = _count_sorted(sv1, lambda p: s0 + p >= tau)
            return carry

        lax.fori_loop(0, PEER_NKEYS, first_key, 0, unroll=2)

    for h in range(PEER_HEADS):
        s1 = _dot(kb_ref[h], q_ref[half + h * PEER_DHALF:half + (h + 1) * PEER_DHALF, :])
        for lc in range(tm // LANES):
            ls = slice(lc * LANES, (lc + 1) * LANES)
            sorted1 = [st_ref[k, h:h + 1, ls] for k in range(PEER_TOPK)]
            for part in range(PEER_NKEYS // RANK_ROWS):
                x = s1[part * RANK_ROWS:(part + 1) * RANK_ROWS, ls]
                words = slice(part * RANK_ROWS // 2, (part + 1) * RANK_ROWS // 2)
                b_ref[lc, h, words, :] = _as_words(jnp.exp(x - sorted1[0]))
                r_ref[lc, h, words, :] = _as_words(_count_sorted(sorted1, lambda p: p > x))


def _peer_tables(peer_wq, sub_keys):
    wq = peer_wq.reshape(D_MODEL, PEER_HEADS, 2, PEER_DHALF)
    wq_t = jnp.transpose(wq, (2, 1, 3, 0)).reshape(2 * PEER_HEADS * PEER_DHALF, D_MODEL).astype(BF16)
    eye = jnp.eye(PEER_HEADS, dtype=F32)
    wka = jnp.einsum("hsnk,hg->snhgk", sub_keys, eye)
    wka = wka.reshape(2, PEER_NKEYS * PEER_HEADS, PEER_HEADS * PEER_DHALF).astype(BF16)
    kb = sub_keys[:, 1].astype(BF16)
    return wq_t, wka, kb


def _peer_prep(h2t, wq_t, wka, kb):
    T = h2t.shape[1]
    tm = PREP_TILE
    half = PEER_HEADS * PEER_DHALF
    first = jax.ShapeDtypeStruct((T // LANES, PEER_NKEYS, PEER_HEADS, LANES), F32)
    second = jax.ShapeDtypeStruct((T // LANES, PEER_HEADS, PEER_NKEYS // 2, LANES), jnp.uint32)
    fspec = pl.BlockSpec((tm // LANES, PEER_NKEYS, PEER_HEADS, LANES), lambda i: (i, 0, 0, 0))
    sspec = pl.BlockSpec((tm // LANES, PEER_HEADS, PEER_NKEYS // 2, LANES), lambda i: (i, 0, 0, 0))
    return pl.pallas_call(
        _peer_prep_kernel,
        grid=(T // tm,),
        in_specs=[
            pl.BlockSpec((D_MODEL, tm), lambda i: (0, i)),
            pl.BlockSpec((2 * half, D_MODEL), lambda i: (0, 0)),
            pl.BlockSpec((2, PEER_NKEYS * PEER_HEADS, half), lambda i: (0, 0, 0)),
            pl.BlockSpec((PEER_HEADS, PEER_NKEYS, PEER_DHALF), lambda i: (0, 0, 0)),
        ],
        out_specs=[fspec, fspec, sspec, sspec],
        out_shape=[first, first, second, second],
        scratch_shapes=[
            pltpu.VMEM((2 * half, tm), BF16),
            pltpu.VMEM((2, PEER_NKEYS, PEER_HEADS, tm), F32),
            pltpu.VMEM((PEER_TOPK, PEER_HEADS, tm), F32),
        ],
        compiler_params=_params(("parallel",)),
        name="peer_prep",
    )(h2t, wq_t, wka, kb)


def _peer_dense_kernel(ht_ref, a_ref, m_ref, b_ref, r_ref, x1_ref, g2_ref, fg_ref, u_hbm, vt_hbm,
                       y_ref, ubuf, vbuf, usem, vsem, acc_ref, act0_ref, act1_ref, w0_ref, w1_ref):
    act_refs = (act0_ref, act1_ref)
    w_refs = (w0_ref, w1_ref)
    tm = ht_ref.shape[1]
    n_chunks = tm // LANES
    n_sub = PEER_N // PEER_SB
    n_pairs = n_sub // 2
    keys_per_sub = PEER_SB // PEER_NKEYS
    sub_words = PEER_SB // 2
    half_rows = D_MODEL // 2

    def u_copy(s):
        slot = s % U_SLOTS
        return pltpu.make_async_copy(u_hbm.at[pl.ds(s * PEER_SB, PEER_SB), :], ubuf.at[slot], usem.at[slot])

    def v_copy(pair):
        slot = pair % V_SLOTS
        return pltpu.make_async_copy(vt_hbm.at[pair], vbuf.at[slot], vsem.at[slot])

    def activations(s, aslot):
        act = _dot(ubuf[s % U_SLOTS], ht_ref[...])
        for tc in range(n_chunks):
            act_refs[aslot][tc] = act[:, tc * LANES:(tc + 1) * LANES]

    def gated(s, aslot, wslot, pos):
        for tc in range(n_chunks):
            a_rows = [a_ref[tc, s * keys_per_sub + k] for k in range(keys_per_sub)]
            m_rows = [m_ref[tc, s * keys_per_sub + k] for k in range(keys_per_sub)]
            for part in range(PEER_NKEYS // GATE_ROWS):
                pwords = slice(part * GATE_ROWS // 2, (part + 1) * GATE_ROWS // 2)
                gates = [None] * keys_per_sub
                for h in range(PEER_HEADS):
                    b = _as_bf16(b_ref[tc, h, pwords, :])
                    r = _as_bf16(r_ref[tc, h, pwords, :])
                    for k in range(keys_per_sub):
                        a_row = jnp.broadcast_to(a_rows[k][h:h + 1, :], (GATE_ROWS, LANES)).astype(BF16)
                        m_row = jnp.broadcast_to(m_rows[k][h:h + 1, :], (GATE_ROWS, LANES)).astype(BF16)
                        term = jnp.where(r < m_row, b, jnp.zeros_like(b)) * a_row
                        gates[k] = term if gates[k] is None else gates[k] + term
                for k in range(keys_per_sub):
                    row0 = k * PEER_NKEYS + part * GATE_ROWS
                    g = _gelu(act_refs[aslot][tc, row0:row0 + GATE_ROWS, :].astype(BF16))
                    word0 = pos * sub_words + row0 // 2
                    w_refs[wslot][tc, word0:word0 + GATE_ROWS // 2, :] = pltpu.bitcast(g * gates[k], jnp.uint32)

    def accumulate(pair, wslot, half):
        w = _as_bf16(jnp.concatenate([w_refs[wslot][tc] for tc in range(n_chunks)], axis=1))
        rows = slice(half * half_rows, (half + 1) * half_rows)
        acc_ref[rows, :] += _dot(vbuf[pair % V_SLOTS, rows, :], w)

    def stage_dma(s, jj):
        static = isinstance(s, int)

        def when(cond, fn):
            if static:
                if cond:
                    fn()
            else:
                pl.when(cond)(fn)

        when(s + U_AHEAD < n_sub, lambda: u_copy(s + U_AHEAD).start())
        if jj % 2 == 0:
            when((s + 1) // 2 < n_pairs, lambda: v_copy((s + 1) // 2).start())
        when(s < n_sub, lambda: u_copy(s).wait())
        if jj % 2 == 0:
            when(s >= 3, lambda: v_copy((s - 3) // 2).wait())

    def stage_compute(s, jj, with_activations=True, with_gate=True):
        if with_activations:
            activations(s, (jj + 1) % 2)
        if with_gate:
            gated(s - 1, jj % 2, (jj // 2) % 2, jj % 2)
        accumulate((s - 3) // 2, (jj // 2 + 1) % 2, jj % 2)

    acc_ref[...] = jnp.zeros_like(acc_ref)
    w_refs[1][...] = jnp.zeros_like(w_refs[1])
    vbuf[(-1) % V_SLOTS] = jnp.zeros(vbuf.shape[1:], vbuf.dtype)
    for s in range(U_AHEAD + 1):
        u_copy(s).start()
    v_copy(0).start()
    u_copy(0).wait()
    activations(0, 0)

    def stages(t, carry):
        for jj in range(STAGE_UNROLL):
            s = 1 + t * STAGE_UNROLL + jj
            stage_dma(s, jj)
            stage_compute(s, jj)
        return carry

    lax.fori_loop(0, n_sub // STAGE_UNROLL, stages, 0)
    for s in (n_sub + 1, n_sub + 2):
        jj = (s - 1) % STAGE_UNROLL
        stage_dma(s, jj)
        stage_compute(s, jj, with_activations=False, with_gate=False)

    out = x1_ref[...] + g2_ref[0, 0] * acc_ref[...].T
    ms = jnp.mean(out * out, axis=-1, keepdims=True)
    y_ref[...] = out * lax.rsqrt(ms + EPS) * fg_ref[...]


def _peer_dense(h2t, u_bf, vt_sub, a, m, b, r, x1, mods, final_g, tokens_per_row):
    T = h2t.shape[1]
    tm = PEER_TM
    if tokens_per_row is None:
        g2_map = lambda i: (CTX_ROW, 5, 0, 0)
    else:
        g2_map = lambda i: ((i * tm) // tokens_per_row, 5, 0, 0)
    fspec = pl.BlockSpec((tm // LANES, PEER_NKEYS, PEER_HEADS, LANES), lambda i: (i, 0, 0, 0))
    sspec = pl.BlockSpec((tm // LANES, PEER_HEADS, PEER_NKEYS // 2, LANES), lambda i: (i, 0, 0, 0))
    return pl.pallas_call(
        _peer_dense_kernel,
        grid=(T // tm,),
        in_specs=[
            pl.BlockSpec((D_MODEL, tm), lambda i: (0, i)),
            fspec, fspec, sspec, sspec,
            pl.BlockSpec((tm, D_MODEL), lambda i: (i, 0)),
            pl.BlockSpec((1, 1, 1, D_MODEL), g2_map),
            pl.BlockSpec((1, D_MODEL), lambda i: (0, 0)),
            pl.BlockSpec(memory_space=pl.ANY),
            pl.BlockSpec(memory_space=pl.ANY),
        ],
        out_specs=pl.BlockSpec((tm, D_MODEL), lambda i: (i, 0)),
        out_shape=jax.ShapeDtypeStruct((T, D_MODEL), F32),
        scratch_shapes=[
            pltpu.VMEM((U_SLOTS, PEER_SB, D_MODEL), BF16),
            pltpu.VMEM((V_SLOTS, D_MODEL, 2 * PEER_SB), BF16),
            pltpu.SemaphoreType.DMA((U_SLOTS,)),
            pltpu.SemaphoreType.DMA((V_SLOTS,)),
            pltpu.VMEM((D_MODEL, tm), F32),
            pltpu.VMEM((tm // LANES, PEER_SB, LANES), F32),
            pltpu.VMEM((tm // LANES, PEER_SB, LANES), F32),
            pltpu.VMEM((tm // LANES, PEER_SB, LANES), jnp.uint32),
            pltpu.VMEM((tm // LANES, PEER_SB, LANES), jnp.uint32),
        ],
        compiler_params=_params(("parallel",)),
        name="peer_dense",
    )(h2t, a, m, b, r, x1, mods, final_g.reshape(1, D_MODEL), u_bf, vt_sub)


def _trunk(x, mods, h0r, h0i, s5w, p, tables, final_g, per_batch, grid):
    B, L, _ = x.shape
    a_re, a_im, wb, wc_re, wc_im = s5w
    zs2d, zp = _inproj(x, mods, p["norm1_g"], p["w_in"], per_batch)
    zs = zs2d.reshape(L, B, D_SSM)
    yf, sfr, sfi = _s5_direction(zs, a_re, a_im, wb, wc_re, wc_im, h0r[:, 0], h0i[:, 0], 0)
    yb, sbr, sbi = _s5_direction(zs, a_re, a_im, wb, wc_re, wc_im, h0r[:, 1], h0i[:, 1], 1)
    x1, h2t = _mix(x, zs2d, zp, yf.reshape(L, B * D_SSM), yb.reshape(L, B * D_SSM), mods, p["ssm_d"],
                   p["w_glu"], p["w_pool"], p["pool_scale"], p["w_out"], p["norm2_g"], per_batch, grid)
    wq_t, wka, kb, u_bf, vt_bf = tables
    a, m, b, r = _peer_prep(h2t, wq_t, wka, kb)
    y = _peer_dense(h2t, u_bf, vt_bf, a, m, b, r, x1, mods, final_g, L if per_batch else None)
    new_re = jnp.stack([sfr, sbr], axis=1).reshape(B, 2, SSM_G, SSM_P)
    new_im = jnp.stack([sfi, sbi], axis=1).reshape(B, 2, SSM_G, SSM_P)
    return y.reshape(B, L, D_MODEL), new_re, new_im


def kernel(x_prompt, x_sample, state_ssm_re, state_ssm_im, c, c_ctx, norm1_g, w_mod, b_mod, w_in,
           ssm_lambda_re, ssm_lambda_im, ssm_log_dt, ssm_b_re, ssm_b_im, ssm_c_re, ssm_c_im, ssm_d,
           w_glu, w_pool, pool_scale, w_out, norm2_g, peer_wq, peer_subkeys, peer_u, peer_v, final_g):
    depth = w_mod.shape[0]
    assert depth == 1, "single trunk layer"
    l = 0
    n_dec = c.shape[0]
    cond = jnp.zeros((COND_ROWS, D_MODEL), F32).at[:n_dec].set(c).at[CTX_ROW].set(c_ctx)
    mods = _mod_vectors(cond, w_mod[l], b_mod[l]).reshape(COND_ROWS, N_MOD, 1, D_MODEL)

    ar, ai, bbr, bbi = _discretise(ssm_lambda_re[l], ssm_lambda_im[l], ssm_log_dt[l], ssm_b_re[l], ssm_b_im[l])
    s5w = _s5_weights(ar, ai, bbr, bbi, ssm_c_re[l], ssm_c_im[l])

    wq_t, wka, kb = _peer_tables(peer_wq[l], peer_subkeys[l])
    vt_sub = jnp.transpose(peer_v[l].reshape(PEER_N // (2 * PEER_SB), 2 * PEER_SB, D_MODEL), (0, 2, 1)).astype(BF16)
    tables = (wq_t, wka, kb, peer_u[l].astype(BF16), vt_sub)

    p = {"norm1_g": norm1_g[l], "w_in": w_in[l].astype(BF16), "ssm_d": ssm_d[l], "w_glu": w_glu[l],
         "w_pool": w_pool[l], "pool_scale": pool_scale[l], "w_out": w_out[l], "norm2_g": norm2_g[l]}

    bp = x_prompt.shape[0]
    zeros = jnp.zeros((bp, 2, SSM_N), F32)
    y_prompt, new_re, new_im = _trunk(x_prompt, mods, zeros, zeros, s5w, p, tables, final_g, False, False)
    h0r = state_ssm_re[:, l].reshape(n_dec, 2, SSM_N)
    h0i = state_ssm_im[:, l].reshape(n_dec, 2, SSM_N)
    y_sample, _, _ = _trunk(x_sample, mods, h0r, h0i, s5w, p, tables, final_g, True, True)
    return (y_prompt, y_sample, new_re[:, None], new_im[:, None])
```

```python
import functools
import math

import numpy as np
import jax
import jax.numpy as jnp
from jax import lax
from jax.experimental import pallas as pl
from jax.experimental.pallas import tpu as pltpu

F32 = jnp.float32
BF16 = jnp.bfloat16

D_MODEL = 1024
D_SSM = 512
D_POOL = 512
SSM_H = 16
SSM_G = 32
SSM_P = 64
SSM_N = SSM_G * SSM_P
POOL_WINDOWS = (2, 4, 8, 16)
POOL_G = 4
POOL_C = 128
GRID_W = 64
PEER_HEADS = 8
PEER_NKEYS = 128
PEER_N = PEER_NKEYS * PEER_NKEYS
PEER_DHALF = 128
PEER_TOPK = 16
N_MOD = 6
EPS = 1e-6

SUBLANES = 8
LANES = 128
VMEM_LIMIT = 56 * 1024 * 1024

COND_ROWS = 16
CTX_ROW = 8
SEQ_TILE = 256
S5_CHUNK = 128
S5_LANES = 512
PREP_TILE = 512
PEER_TM = 512
PEER_SB = 512
STAGE_UNROLL = 4
U_SLOTS = 6
U_AHEAD = 4
V_SLOTS = 3
RANK_ROWS = 64
GATE_ROWS = 64


def _gelu(x):
    return 0.5 * x * (1.0 + jnp.tanh(0.7978845608028654 * (x + 0.044715 * (x * x * x))))


def _split_bf16(x):
    hi = x.astype(BF16)
    lo = (x - hi.astype(F32)).astype(BF16)
    return hi, lo


def _dot(a, b):
    return jnp.dot(a, b, preferred_element_type=F32)


def _params(sem):
    return pltpu.CompilerParams(dimension_semantics=sem, vmem_limit_bytes=VMEM_LIMIT)


def _mod_kernel(cond_ref, w_ref, b_ref, o_ref):
    c = cond_ref[...]
    s = c * jax.nn.sigmoid(c)
    shi, slo = _split_bf16(s)
    whi, wlo = _split_bf16(w_ref[...])
    o_ref[...] = _dot(shi, whi) + _dot(slo, whi) + _dot(shi, wlo) + b_ref[...]


def _mod_vectors(cond, w_mod, b_mod):
    n = w_mod.shape[1]
    bn = D_MODEL
    return pl.pallas_call(
        _mod_kernel,
        grid=(n // bn,),
        in_specs=[
            pl.BlockSpec((COND_ROWS, D_MODEL), lambda k: (0, 0)),
            pl.BlockSpec((D_MODEL, bn), lambda k: (0, k)),
            pl.BlockSpec((1, bn), lambda k: (0, k)),
        ],
        out_specs=pl.BlockSpec((COND_ROWS, bn), lambda k: (0, k)),
        out_shape=jax.ShapeDtypeStruct((COND_ROWS, n), F32),
        compiler_params=_params(("arbitrary",)),
        name="mod",
    )(cond, w_mod, b_mod.reshape(1, n))


def _disc_kernel(lr_ref, li_ref, ldt_ref, br_ref, bi_ref, ar_ref, ai_ref, bbr_ref, bbi_ref):
    lr = lr_ref[...]
    li = li_ref[...]
    dt = jnp.exp(ldt_ref[...])
    mag = jnp.exp(lr * dt)
    ar = mag * jnp.cos(li * dt)
    ai = mag * jnp.sin(li * dt)
    den = lr * lr + li * li
    nr = ar - 1.0
    ni = ai
    fr = (nr * lr + ni * li) / den
    fi = (ni * lr - nr * li) / den
    ar_ref[...] = ar
    ai_ref[...] = ai
    br = br_ref[...]
    bi = bi_ref[...]
    frb = fr[:, None, :]
    fib = fi[:, None, :]
    bbr_ref[...] = frb * br - fib * bi
    bbi_ref[...] = frb * bi + fib * br


def _discretise(lam_re, lam_im, log_dt, b_re, b_im):
    rows = 2 * SSM_G
    lr = lam_re.reshape(rows, SSM_P)
    li = lam_im.reshape(rows, SSM_P)
    ldt = jnp.broadcast_to(log_dt.reshape(rows, 1), (rows, SSM_P))
    br = jnp.swapaxes(b_re, -1, -2).reshape(rows, SSM_H, SSM_P)
    bi = jnp.swapaxes(b_im, -1, -2).reshape(rows, SSM_H, SSM_P)
    small = jax.ShapeDtypeStruct((rows, SSM_P), F32)
    big = jax.ShapeDtypeStruct((rows, SSM_H, SSM_P), F32)
    return pl.pallas_call(_disc_kernel, out_shape=(small, small, big, big), name="disc")(lr, li, ldt, br, bi)


def _s5_weights(ar, ai, bbr, bbi, c_re, c_im):
    half_g = SSM_G // 2
    eye = jnp.eye(half_g, dtype=F32)
    a_re = ar.reshape(2, 1, SSM_N)
    a_im = ai.reshape(2, 1, SSM_N)

    def bmat(t):
        t = t.reshape(2, 2, half_g, SSM_H, SSM_P)
        m = jnp.einsum("djghp,gk->djghkp", t, eye)
        return m.reshape(2, 2, half_g * SSM_H, half_g * SSM_P)

    def cmat(t):
        t = t.reshape(2, 2, half_g, SSM_H, SSM_P)
        m = jnp.einsum("djghp,gk->djgpkh", t, eye)
        return m.reshape(2, 2, half_g * SSM_P, half_g * SSM_H)

    wb = jnp.concatenate([bmat(bbr), bmat(bbi)], axis=-1).astype(BF16)
    wc_re = cmat(c_re).astype(BF16)
    wc_im = cmat(c_im).astype(BF16)
    return a_re, a_im, wb, wc_re, wc_im


def _inproj_kernel(x_ref, sh_ref, sc_ref, g_ref, w_ref, zs_ref, zp_ref):
    x = x_ref[0]
    ms = jnp.mean(x * x, axis=-1, keepdims=True)
    y = x * lax.rsqrt(ms + EPS) * g_ref[...]
    h = y * (1.0 + sc_ref[0, 0]) + sh_ref[0, 0]
    z = _dot(h.astype(BF16), w_ref[...])
    zs_ref[...] = z[:, :D_SSM]
    zp_ref[0] = z[:, D_SSM:]


def _mod_spec(k, per_batch):
    if per_batch:
        return pl.BlockSpec((1, 1, 1, D_MODEL), lambda b, i: (b, k, 0, 0))
    return pl.BlockSpec((1, 1, 1, D_MODEL), lambda b, i: (CTX_ROW, k, 0, 0))


def _seq_tile(L, grid):
    if grid:
        tm = min(L, SEQ_TILE)
        assert L % tm == 0 and tm % GRID_W == 0
        return tm
    assert L <= SEQ_TILE, "sequence pooling needs the whole sequence in one tile"
    return L


def _inproj(x, mods, g, w_in, per_batch, grid):
    B, L, _ = x.shape
    tm = _seq_tile(L, grid)
    return pl.pallas_call(
        _inproj_kernel,
        grid=(B, L // tm),
        in_specs=[
            pl.BlockSpec((1, tm, D_MODEL), lambda b, i: (b, i, 0)),
            _mod_spec(0, per_batch),
            _mod_spec(1, per_batch),
            pl.BlockSpec((1, D_MODEL), lambda b, i: (0, 0)),
            pl.BlockSpec((D_MODEL, D_MODEL), lambda b, i: (0, 0)),
        ],
        out_specs=[
            pl.BlockSpec((tm, D_SSM), lambda b, i: (i, b)),
            pl.BlockSpec((1, tm, D_POOL), lambda b, i: (b, i, 0)),
        ],
        out_shape=[
            jax.ShapeDtypeStruct((L, B * D_SSM), F32),
            jax.ShapeDtypeStruct((B, L, D_POOL), F32),
        ],
        compiler_params=_params(("parallel", "parallel")),
        name="inproj",
    )(x, mods, mods, g.reshape(1, D_MODEL), w_in)


def _s5_kernel(u_ref, are_ref, aim_ref, wb_ref, wcr_ref, wci_ref, h0r_ref, h0i_ref,
               y_ref, sr_ref, si_ref, bur_ref, bui_ref, cr_ref, ci_ref, *, reverse, n_chunks):
    c = pl.program_id(1)
    rows = S5_CHUNK * SUBLANES
    half = SSM_N // 2
    half_in = D_SSM // 2

    @pl.when(c == 0)
    def _():
        cr_ref[...] = h0r_ref[...]
        ci_ref[...] = h0i_ref[...]

    u = u_ref[...].reshape(rows, D_SSM).astype(BF16)
    for j in range(2):
        r = _dot(u[:, j * half_in:(j + 1) * half_in], wb_ref[0, j])
        bur_ref[:, j * half:(j + 1) * half] = r[:, :half]
        bui_ref[:, j * half:(j + 1) * half] = r[:, half:]

    for lc in range(SSM_N // S5_LANES):
        ls = slice(lc * S5_LANES, (lc + 1) * S5_LANES)
        ar = jnp.broadcast_to(are_ref[0, :, ls], (SUBLANES, S5_LANES))
        ai = jnp.broadcast_to(aim_ref[0, :, ls], (SUBLANES, S5_LANES))

        def step(i, carry, ls=ls, ar=ar, ai=ai):
            hr, hi = carry
            t = (S5_CHUNK - 1 - i) if reverse else i
            row = pl.multiple_of(t * SUBLANES, SUBLANES)
            nhr = ar * hr - ai * hi + bur_ref[pl.ds(row, SUBLANES), ls]
            nhi = ar * hi + ai * hr + bui_ref[pl.ds(row, SUBLANES), ls]
            bur_ref[pl.ds(row, SUBLANES), ls] = nhr
            bui_ref[pl.ds(row, SUBLANES), ls] = nhi
            return nhr, nhi

        hr, hi = lax.fori_loop(0, S5_CHUNK, step, (cr_ref[:, ls], ci_ref[:, ls]), unroll=4)
        cr_ref[:, ls] = hr
        ci_ref[:, ls] = hi

    ys = []
    for j in range(2):
        hs = slice(j * half, (j + 1) * half)
        yr = _dot(bur_ref[:, hs].astype(BF16), wcr_ref[0, j])
        yi = _dot(bui_ref[:, hs].astype(BF16), wci_ref[0, j])
        ys.append(yr - yi)
    y_ref[...] = jnp.concatenate(ys, axis=-1).reshape(S5_CHUNK, SUBLANES, D_SSM)

    @pl.when(c == n_chunks - 1)
    def _():
        sr_ref[...] = cr_ref[...]
        si_ref[...] = ci_ref[...]


def _s5_direction(zs, a_re, a_im, wb, wc_re, wc_im, h0r, h0i, d):
    L, B, _ = zs.shape
    n_chunks = L // S5_CHUNK
    reverse = d == 1
    cidx = (lambda g, c: n_chunks - 1 - c) if reverse else (lambda g, c: c)
    rows = S5_CHUNK * SUBLANES
    kern = functools.partial(_s5_kernel, reverse=reverse, n_chunks=n_chunks)
    state = jax.ShapeDtypeStruct((B, SSM_N), F32)
    return pl.pallas_call(
        kern,
        grid=(B // SUBLANES, n_chunks),
        in_specs=[
            pl.BlockSpec((S5_CHUNK, SUBLANES, D_SSM), lambda g, c: (cidx(g, c), g, 0)),
            pl.BlockSpec((1, 1, SSM_N), lambda g, c: (d, 0, 0)),
            pl.BlockSpec((1, 1, SSM_N), lambda g, c: (d, 0, 0)),
            pl.BlockSpec((1, 2, D_SSM // 2, SSM_N), lambda g, c: (d, 0, 0, 0)),
            pl.BlockSpec((1, 2, SSM_N // 2, D_SSM // 2), lambda g, c: (d, 0, 0, 0)),
            pl.BlockSpec((1, 2, SSM_N // 2, D_SSM // 2), lambda g, c: (d, 0, 0, 0)),
            pl.BlockSpec((SUBLANES, SSM_N), lambda g, c: (g, 0)),
            pl.BlockSpec((SUBLANES, SSM_N), lambda g, c: (g, 0)),
        ],
        out_specs=[
            pl.BlockSpec((S5_CHUNK, SUBLANES, D_SSM), lambda g, c: (cidx(g, c), g, 0)),
            pl.BlockSpec((SUBLANES, SSM_N), lambda g, c: (g, 0)),
            pl.BlockSpec((SUBLANES, SSM_N), lambda g, c: (g, 0)),
        ],
        out_shape=[jax.ShapeDtypeStruct((L, B, D_SSM), F32), state, state],
        scratch_shapes=[
            pltpu.VMEM((rows, SSM_N), F32),
            pltpu.VMEM((rows, SSM_N), F32),
            pltpu.VMEM((SUBLANES, SSM_N), F32),
            pltpu.VMEM((SUBLANES, SSM_N), F32),
        ],
        compiler_params=_params(("parallel", "arbitrary")),
        name="s5_bwd" if reverse else "s5_fwd",
    )(zs, a_re, a_im, wb, wc_re, wc_im, h0r, h0i)


def _pool_tables(grid, tm):
    seg = GRID_W if grid else tm
    t = np.arange(tm)
    pos = t % seg
    base = t - pos
    mats, invs = [], []
    for w in POOL_WINDOWS:
        lo = np.clip(pos - w // 2, 0, seg)
        hi = np.clip(pos + (w - w // 2), 0, seg)
        s = t[None, :]
        m = (s >= (base + lo)[:, None]) & (s < (base + hi)[:, None])
        mats.append(m.astype(np.float32))
        invs.append(np.broadcast_to((1.0 / (hi - lo).astype(np.float32))[:, None], (tm, LANES)))
    return jnp.asarray(np.stack(mats), dtype=BF16), jnp.asarray(np.stack(invs), dtype=F32)


def _mix_kernel(x_ref, u_ref, zp_ref, yf_ref, yb_ref, g1_ref, sh2_ref, sc2_ref, d_ref, wglu_ref,
                pm_ref, pinv_ref, wpool_ref, pscale_ref, wout_ref, n2_ref, x1_ref, h2t_ref):
    x = x_ref[0]
    y = d_ref[...] * u_ref[...] + yf_ref[...] + yb_ref[...]
    g = _gelu(y)
    ys = g * jax.nn.sigmoid(_dot(g.astype(BF16), wglu_ref[...]))

    zp = zp_ref[0]
    parts = [ys]
    for gi in range(POOL_G):
        zg = zp[:, gi * POOL_C:(gi + 1) * POOL_C]
        hi, lo = _split_bf16(zg)
        pm = pm_ref[gi]
        win = _dot(pm, hi) + _dot(pm, lo)
        pooled = win * pinv_ref[gi] - zg
        og = _dot(pooled.astype(BF16), wpool_ref[gi])
        parts.append(og * pscale_ref[:, gi * POOL_C:(gi + 1) * POOL_C])
    mix = _dot(jnp.concatenate(parts, axis=-1).astype(BF16), wout_ref[...])

    x1 = x + g1_ref[0, 0] * mix
    x1_ref[...] = x1
    ms = jnp.mean(x1 * x1, axis=-1, keepdims=True)
    h2 = x1 * lax.rsqrt(ms + EPS) * n2_ref[...]
    h2 = h2 * (1.0 + sc2_ref[0, 0]) + sh2_ref[0, 0]
    h2t_ref[...] = h2.T.astype(BF16)


def _mix(x, zs2d, zp, yf2d, yb2d, mods, ssm_d, w_glu, w_pool, pool_scale, w_out, norm2_g, per_batch, grid):
    B, L, _ = x.shape
    tm = _seq_tile(L, grid)
    nt = L // tm
    pm, pinv = _pool_tables(grid, tm)
    tm_spec = pl.BlockSpec((tm, D_SSM), lambda b, i: (i, b))
    const2 = lambda b, i: (0, 0)
    const3 = lambda b, i: (0, 0, 0)
    return pl.pallas_call(
        _mix_kernel,
        grid=(B, nt),
        in_specs=[
            pl.BlockSpec((1, tm, D_MODEL), lambda b, i: (b, i, 0)),
            tm_spec,
            pl.BlockSpec((1, tm, D_POOL), lambda b, i: (b, i, 0)),
            tm_spec,
            tm_spec,
            _mod_spec(2, per_batch),
            _mod_spec(3, per_batch),
            _mod_spec(4, per_batch),
            pl.BlockSpec((1, D_SSM), const2),
            pl.BlockSpec((D_SSM, D_SSM), const2),
            pl.BlockSpec((POOL_G, tm, tm), const3),
            pl.BlockSpec((POOL_G, tm, LANES), const3),
            pl.BlockSpec((POOL_G, POOL_C, POOL_C), const3),
            pl.BlockSpec((1, D_POOL), const2),
            pl.BlockSpec((D_MODEL, D_MODEL), const2),
            pl.BlockSpec((1, D_MODEL), const2),
        ],
        out_specs=[
            pl.BlockSpec((tm, D_MODEL), lambda b, i: (b * nt + i, 0)),
            pl.BlockSpec((D_MODEL, tm), lambda b, i: (0, b * nt + i)),
        ],
        out_shape=[
            jax.ShapeDtypeStruct((B * L, D_MODEL), F32),
            jax.ShapeDtypeStruct((D_MODEL, B * L), BF16),
        ],
        compiler_params=_params(("parallel", "parallel")),
        name="mix",
    )(x, zs2d, zp, yf2d, yb2d, mods, mods, mods, ssm_d.reshape(1, D_SSM), w_glu.astype(BF16),
      pm, pinv, w_pool.astype(BF16), pool_scale.reshape(1, D_POOL), w_out.astype(BF16),
      norm2_g.reshape(1, D_MODEL))


def _sort_pairs(n):
    pairs = []

    def merge(lo, hi, r):
        step = r * 2
        if step < hi - lo:
            merge(lo, hi, step)
            merge(lo + r, hi, step)
            for i in range(lo + r, hi - r, step):
                pairs.append((i, i + r))
        else:
            pairs.append((lo, lo + r))

    def sort(lo, hi):
        if hi - lo >= 1:
            mid = lo + (hi - lo) // 2
            sort(lo, mid)
            sort(mid + 1, hi)
            merge(lo, hi, 1)

    sort(0, n - 1)
    return pairs


_SORT16 = _sort_pairs(PEER_TOPK)


def _vmax(a, b):
    if a is None:
        return b
    if b is None:
        return a
    return jnp.maximum(a, b)


def _vmin(a, b):
    if a is None or b is None:
        return None
    return jnp.minimum(a, b)


def _sort16_desc(w):
    w = list(w)
    for i, j in _SORT16:
        w[i], w[j] = _vmax(w[i], w[j]), _vmin(w[i], w[j])
    return w


def _merge_top16(a, b):
    k = PEER_TOPK
    w = [_vmax(a[i], b[k - 1 - i]) for i in range(k)]
    d = k // 2
    while d >= 1:
        for i in range(k):
            if i & d == 0:
                w[i], w[i + d] = _vmax(w[i], w[i + d]), _vmin(w[i], w[i + d])
        d //= 2
    return w


def _top16_of_128(load):
    cur = None
    for grp in range(PEER_NKEYS // PEER_TOPK):
        s = _sort16_desc([load(grp * PEER_TOPK + i) for i in range(PEER_TOPK)])
        cur = s if cur is None else _merge_top16(cur, s)
    return cur


def _top16_pair_sums(sv0, sv1):
    k = PEER_TOPK
    pad = lambda row: row + [None] * (k - len(row))
    cur = [sv0[0] + sv1[j] for j in range(k)]
    for i in range(1, k // 2):
        cur = _merge_top16(cur, pad([sv0[i] + sv1[j] for j in range(k // (i + 1))]))
    return _merge_top16(cur, pad([sv0[i] + sv1[0] for i in range(k // 2, k)]))


def _count_sorted(v, above):
    sel = jnp.where
    g8 = above(v[7])
    g4 = above(sel(g8, v[11], v[3]))
    g2 = above(sel(g8, sel(g4, v[13], v[9]), sel(g4, v[5], v[1])))
    g1 = above(sel(g8, sel(g4, sel(g2, v[14], v[12]), sel(g2, v[10], v[8])),
                   sel(g4, sel(g2, v[6], v[4]), sel(g2, v[2], v[0]))))
    low = sel(g8, 8.0, 0.0) + sel(g4, 4.0, 0.0) + sel(g2, 2.0, 0.0) + sel(g1, 1.0, 0.0)
    return sel(above(v[15]), 16.0, low)


def _as_words(x):
    return pltpu.bitcast(x.astype(BF16), jnp.uint32)


def _as_bf16(words):
    return pltpu.bitcast(words, BF16)


def _peer_prep_kernel(ht_ref, wq_ref, wka_ref, kb_ref, a_ref, m_ref, b_ref, r_ref, q_ref, sa_ref, st_ref):
    tm = ht_ref.shape[1]
    half = PEER_HEADS * PEER_DHALF
    q_ref[...] = _dot(wq_ref[...], ht_ref[...]).astype(BF16)
    for side in range(2):
        s = _dot(wka_ref[side], q_ref[side * half:(side + 1) * half, :])
        sa_ref[side] = s.reshape(PEER_NKEYS, PEER_HEADS, tm)

    for lc in range(tm // LANES):
        ls = slice(lc * LANES, (lc + 1) * LANES)
        sv0 = _top16_of_128(lambda n: sa_ref[0, n, :, ls])
        sv1 = _top16_of_128(lambda n: sa_ref[1, n, :, ls])
        top = _top16_pair_sums(sv0, sv1)
        tau = top[PEER_TOPK - 1]
        best = sv0[0] + sv1[0]
        z = jnp.exp(top[0] - best)
        for k in range(1, PEER_TOPK):
            z = z + jnp.exp(top[k] - best)
        inv_z = 1.0 / z
        best0 = sv0[0]
        for k in range(PEER_TOPK):
            st_ref[k, :, ls] = sv1[k]

        def first_key(n, carry, lc=lc, ls=ls, sv1=sv1, tau=tau, best0=best0, inv_z=inv_z):
            s0 = sa_ref[0, n, :, ls]
            a_ref[lc, n] = jnp.exp(s0 - best0) * inv_z
            m_ref[lc, n] = _count_sorted(sv1, lambda p: s0 + p >= tau)
            return carry

        lax.fori_loop(0, PEER_NKEYS, first_key, 0, unroll=2)

    for h in range(PEER_HEADS):
        s1 = _dot(kb_ref[h], q_ref[half + h * PEER_DHALF:half + (h + 1) * PEER_DHALF, :])
        for lc in range(tm // LANES):
            ls = slice(lc * LANES, (lc + 1) * LANES)
            sorted1 = [st_ref[k, h:h + 1, ls] for k in range(PEER_TOPK)]
            for part in range(PEER_NKEYS // RANK_ROWS):
                x = s1[part * RANK_ROWS:(part + 1) * RANK_ROWS, ls]
                words = slice(part * RANK_ROWS // 2, (part + 1) * RANK_ROWS // 2)
                b_ref[lc, h, words, :] = _as_words(jnp.exp(x - sorted1[0]))
                r_ref[lc, h, words, :] = _as_words(_count_sorted(sorted1, lambda p: p > x))


def _peer_tables(peer_wq, sub_keys):
    wq = peer_wq.reshape(D_MODEL, PEER_HEADS, 2, PEER_DHALF)
    wq_t = jnp.transpose(wq, (2, 1, 3, 0)).reshape(2 * PEER_HEADS * PEER_DHALF, D_MODEL).astype(BF16)
    eye = jnp.eye(PEER_HEADS, dtype=F32)
    wka = jnp.einsum("hsnk,hg->snhgk", sub_keys, eye)
    wka = wka.reshape(2, PEER_NKEYS * PEER_HEADS, PEER_HEADS * PEER_DHALF).astype(BF16)
    kb = sub_keys[:, 1].astype(BF16)
    return wq_t, wka, kb


def _peer_prep(h2t, wq_t, wka, kb):
    T = h2t.shape[1]
    tm = PREP_TILE
    half = PEER_HEADS * PEER_DHALF
    first = jax.ShapeDtypeStruct((T // LANES, PEER_NKEYS, PEER_HEADS, LANES), F32)
    second = jax.ShapeDtypeStruct((T // LANES, PEER_HEADS, PEER_NKEYS // 2, LANES), jnp.uint32)
    fspec = pl.BlockSpec((tm // LANES, PEER_NKEYS, PEER_HEADS, LANES), lambda i: (i, 0, 0, 0))
    sspec = pl.BlockSpec((tm // LANES, PEER_HEADS, PEER_NKEYS // 2, LANES), lambda i: (i, 0, 0, 0))
    return pl.pallas_call(
        _peer_prep_kernel,
        grid=(T // tm,),
        in_specs=[
            pl.BlockSpec((D_MODEL, tm), lambda i: (0, i)),
            pl.BlockSpec((2 * half, D_MODEL), lambda i: (0, 0)),
            pl.BlockSpec((2, PEER_NKEYS * PEER_HEADS, half), lambda i: (0, 0, 0)),
            pl.BlockSpec((PEER_HEADS, PEER_NKEYS, PEER_DHALF), lambda i: (0, 0, 0)),
        ],
        out_specs=[fspec, fspec, sspec, sspec],
        out_shape=[first, first, second, second],
        scratch_shapes=[
            pltpu.VMEM((2 * half, tm), BF16),
            pltpu.VMEM((2, PEER_NKEYS, PEER_HEADS, tm), F32),
            pltpu.VMEM((PEER_TOPK, PEER_HEADS, tm), F32),
        ],
        compiler_params=_params(("parallel",)),
        name="peer_prep",
    )(h2t, wq_t, wka, kb)


def _peer_dense_kernel(ht_ref, a_ref, m_ref, b_ref, r_ref, x1_ref, g2_ref, fg_ref, u_hbm, vt_hbm,
                       y_ref, ubuf, vbuf, usem, vsem, acc_ref, act0_ref, act1_ref, w0_ref, w1_ref):
    act_refs = (act0_ref, act1_ref)
    w_refs = (w0_ref, w1_ref)
    tm = ht_ref.shape[1]
    n_chunks = tm // LANES
    n_sub = PEER_N // PEER_SB
    n_pairs = n_sub // 2
    keys_per_sub = PEER_SB // PEER_NKEYS
    sub_words = PEER_SB // 2
    half_rows = D_MODEL // 2

    def u_copy(s):
        slot = s % U_SLOTS
        return pltpu.make_async_copy(u_hbm.at[pl.ds(s * PEER_SB, PEER_SB), :], ubuf.at[slot], usem.at[slot])

    def v_copy(pair):
        slot = pair % V_SLOTS
        return pltpu.make_async_copy(vt_hbm.at[pair], vbuf.at[slot], vsem.at[slot])

    def activations(s, aslot):
        act = _dot(ubuf[s % U_SLOTS], ht_ref[...])
        for tc in range(n_chunks):
            act_refs[aslot][tc] = act[:, tc * LANES:(tc + 1) * LANES]

    def gated(s, aslot, wslot, pos):
        for tc in range(n_chunks):
            a_rows = [a_ref[tc, s * keys_per_sub + k] for k in range(keys_per_sub)]
            m_rows = [m_ref[tc, s * keys_per_sub + k] for k in range(keys_per_sub)]
            for part in range(PEER_NKEYS // GATE_ROWS):
                pwords = slice(part * GATE_ROWS // 2, (part + 1) * GATE_ROWS // 2)
                gates = [None] * keys_per_sub
                for h in range(PEER_HEADS):
                    b = _as_bf16(b_ref[tc, h, pwords, :])
                    r = _as_bf16(r_ref[tc, h, pwords, :])
                    for k in range(keys_per_sub):
                        a_row = jnp.broadcast_to(a_rows[k][h:h + 1, :], (GATE_ROWS, LANES)).astype(BF16)
                        m_row = jnp.broadcast_to(m_rows[k][h:h + 1, :], (GATE_ROWS, LANES)).astype(BF16)
                        term = jnp.where(r < m_row, b, jnp.zeros_like(b)) * a_row
                        gates[k] = term if gates[k] is None else gates[k] + term
                for k in range(keys_per_sub):
                    row0 = k * PEER_NKEYS + part * GATE_ROWS
                    g = _gelu(act_refs[aslot][tc, row0:row0 + GATE_ROWS, :].astype(BF16))
                    word0 = pos * sub_words + row0 // 2
                    w_refs[wslot][tc, word0:word0 + GATE_ROWS // 2, :] = pltpu.bitcast(g * gates[k], jnp.uint32)

    def accumulate(pair, wslot, half):
        w = _as_bf16(jnp.concatenate([w_refs[wslot][tc] for tc in range(n_chunks)], axis=1))
        rows = slice(half * half_rows, (half + 1) * half_rows)
        acc_ref[rows, :] += _dot(vbuf[pair % V_SLOTS, rows, :], w)

    def stage_dma(s, jj):
        static = isinstance(s, int)

        def when(cond, fn):
            if static:
                if cond:
                    fn()
            else:
                pl.when(cond)(fn)

        when(s + U_AHEAD < n_sub, lambda: u_copy(s + U_AHEAD).start())
        if jj % 2 == 0:
            when((s + 1) // 2 < n_pairs, lambda: v_copy((s + 1) // 2).start())
        when(s < n_sub, lambda: u_copy(s).wait())
        if jj % 2 == 0:
            when(s >= 3, lambda: v_copy((s - 3) // 2).wait())

    def stage_compute(s, jj, with_activations=True, with_gate=True):
        if with_activations:
            activations(s, (jj + 1) % 2)
        if with_gate:
            gated(s - 1, jj % 2, (jj // 2) % 2, jj % 2)
        accumulate((s - 3) // 2, (jj // 2 + 1) % 2, jj % 2)

    acc_ref[...] = jnp.zeros_like(acc_ref)
    w_refs[1][...] = jnp.zeros_like(w_refs[1])

    @pl.when(pl.program_id(0) == 0)
    def _():
        vbuf[(-1) % V_SLOTS] = jnp.zeros(vbuf.shape[1:], vbuf.dtype)

    for s in range(U_AHEAD + 1):
        u_copy(s).start()
    v_copy(0).start()
    u_copy(0).wait()
    activations(0, 0)

    def stages(t, carry):
        for jj in range(STAGE_UNROLL):
            s = 1 + t * STAGE_UNROLL + jj
            stage_dma(s, jj)
            stage_compute(s, jj)
        return carry

    lax.fori_loop(0, n_sub // STAGE_UNROLL, stages, 0)
    for s in (n_sub + 1, n_sub + 2):
        jj = (s - 1) % STAGE_UNROLL
        stage_dma(s, jj)
        stage_compute(s, jj, with_activations=False, with_gate=False)

    out = x1_ref[...] + g2_ref[0, 0] * acc_ref[...].T
    ms = jnp.mean(out * out, axis=-1, keepdims=True)
    y_ref[...] = out * lax.rsqrt(ms + EPS) * fg_ref[...]


def _peer_dense(h2t, u_bf, vt_sub, a, m, b, r, x1, mods, final_g, tokens_per_row):
    T = h2t.shape[1]
    tm = PEER_TM
    if tokens_per_row is None:
        g2_map = lambda i: (CTX_ROW, 5, 0, 0)
    else:
        g2_map = lambda i: ((i * tm) // tokens_per_row, 5, 0, 0)
    fspec = pl.BlockSpec((tm // LANES, PEER_NKEYS, PEER_HEADS, LANES), lambda i: (i, 0, 0, 0))
    sspec = pl.BlockSpec((tm // LANES, PEER_HEADS, PEER_NKEYS // 2, LANES), lambda i: (i, 0, 0, 0))
    return pl.pallas_call(
        _peer_dense_kernel,
        grid=(T // tm,),
        in_specs=[
            pl.BlockSpec((D_MODEL, tm), lambda i: (0, i)),
            fspec, fspec, sspec, sspec,
            pl.BlockSpec((tm, D_MODEL), lambda i: (i, 0)),
            pl.BlockSpec((1, 1, 1, D_MODEL), g2_map),
            pl.BlockSpec((1, D_MODEL), lambda i: (0, 0)),
            pl.BlockSpec(memory_space=pl.ANY),
            pl.BlockSpec(memory_space=pl.ANY),
        ],
        out_specs=pl.BlockSpec((tm, D_MODEL), lambda i: (i, 0)),
        out_shape=jax.ShapeDtypeStruct((T, D_MODEL), F32),
        scratch_shapes=[
            pltpu.VMEM((U_SLOTS, PEER_SB, D_MODEL), BF16),
            pltpu.VMEM((V_SLOTS, D_MODEL, 2 * PEER_SB), BF16),
            pltpu.SemaphoreType.DMA((U_SLOTS,)),
            pltpu.SemaphoreType.DMA((V_SLOTS,)),
            pltpu.VMEM((D_MODEL, tm), F32),
            pltpu.VMEM((tm // LANES, PEER_SB, LANES), F32),
            pltpu.VMEM((tm // LANES, PEER_SB, LANES), F32),
            pltpu.VMEM((tm // LANES, PEER_SB, LANES), jnp.uint32),
            pltpu.VMEM((tm // LANES, PEER_SB, LANES), jnp.uint32),
        ],
        compiler_params=_params(("arbitrary",)),
        name="peer_dense",
    )(h2t, a, m, b, r, x1, mods, final_g.reshape(1, D_MODEL), u_bf, vt_sub)


def _trunk(x, mods, h0r, h0i, s5w, p, tables, final_g, per_batch, grid):
    B, L, _ = x.shape
    a_re, a_im, wb, wc_re, wc_im = s5w
    zs2d, zp = _inproj(x, mods, p["norm1_g"], p["w_in"], per_batch, grid)
    zs = zs2d.reshape(L, B, D_SSM)
    yf, sfr, sfi = _s5_direction(zs, a_re, a_im, wb, wc_re, wc_im, h0r[:, 0], h0i[:, 0], 0)
    yb, sbr, sbi = _s5_direction(zs, a_re, a_im, wb, wc_re, wc_im, h0r[:, 1], h0i[:, 1], 1)
    x1, h2t = _mix(x, zs2d, zp, yf.reshape(L, B * D_SSM), yb.reshape(L, B * D_SSM), mods, p["ssm_d"],
                   p["w_glu"], p["w_pool"], p["pool_scale"], p["w_out"], p["norm2_g"], per_batch, grid)
    wq_t, wka, kb, u_bf, vt_bf = tables
    a, m, b, r = _peer_prep(h2t, wq_t, wka, kb)
    y = _peer_dense(h2t, u_bf, vt_bf, a, m, b, r, x1, mods, final_g, L if per_batch else None)
    new_re = jnp.stack([sfr, sbr], axis=1).reshape(B, 2, SSM_G, SSM_P)
    new_im = jnp.stack([sfi, sbi], axis=1).reshape(B, 2, SSM_G, SSM_P)
    return y.reshape(B, L, D_MODEL), new_re, new_im


def kernel(x_prompt, x_sample, state_ssm_re, state_ssm_im, c, c_ctx, norm1_g, w_mod, b_mod, w_in,
           ssm_lambda_re, ssm_lambda_im, ssm_log_dt, ssm_b_re, ssm_b_im, ssm_c_re, ssm_c_im, ssm_d,
           w_glu, w_pool, pool_scale, w_out, norm2_g, peer_wq, peer_subkeys, peer_u, peer_v, final_g):
    depth = w_mod.shape[0]
    assert depth == 1, "single trunk layer"
    l = 0
    n_dec = c.shape[0]
    cond = jnp.zeros((COND_ROWS, D_MODEL), F32).at[:n_dec].set(c).at[CTX_ROW].set(c_ctx)
    mods = _mod_vectors(cond, w_mod[l], b_mod[l]).reshape(COND_ROWS, N_MOD, 1, D_MODEL)

    ar, ai, bbr, bbi = _discretise(ssm_lambda_re[l], ssm_lambda_im[l], ssm_log_dt[l], ssm_b_re[l], ssm_b_im[l])
    s5w = _s5_weights(ar, ai, bbr, bbi, ssm_c_re[l], ssm_c_im[l])

    wq_t, wka, kb = _peer_tables(peer_wq[l], peer_subkeys[l])
    vt_sub = jnp.transpose(peer_v[l].reshape(PEER_N // (2 * PEER_SB), 2 * PEER_SB, D_MODEL), (0, 2, 1)).astype(BF16)
    tables = (wq_t, wka, kb, peer_u[l].astype(BF16), vt_sub)

    p = {"norm1_g": norm1_g[l], "w_in": w_in[l].astype(BF16), "ssm_d": ssm_d[l], "w_glu": w_glu[l],
         "w_pool": w_pool[l], "pool_scale": pool_scale[l], "w_out": w_out[l], "norm2_g": norm2_g[l]}

    bp = x_prompt.shape[0]
    zeros = jnp.zeros((bp, 2, SSM_N), F32)
    y_prompt, new_re, new_im = _trunk(x_prompt, mods, zeros, zeros, s5w, p, tables, final_g, False, False)
    h0r = state_ssm_re[:, l].reshape(n_dec, 2, SSM_N)
    h0i = state_ssm_im[:, l].reshape(n_dec, 2, SSM_N)
    y_sample, _, _ = _trunk(x_sample, mods, h0r, h0i, s5w, p, tables, final_g, True, True)
    return (y_prompt, y_sample, new_re[:, None], new_im[:, None])
```

```python
import functools
import math

import numpy as np
import jax
import jax.numpy as jnp
from jax import lax
from jax.experimental import pallas as pl
from jax.experimental.pallas import tpu as pltpu

F32 = jnp.float32
BF16 = jnp.bfloat16

D_MODEL = 1024
D_SSM = 512
D_POOL = 512
SSM_H = 16
SSM_G = 32
SSM_P = 64
SSM_N = SSM_G * SSM_P
POOL_WINDOWS = (2, 4, 8, 16)
POOL_G = 4
POOL_C = 128
GRID_W = 64
PEER_HEADS = 8
PEER_NKEYS = 128
PEER_N = PEER_NKEYS * PEER_NKEYS
PEER_DHALF = 128
PEER_TOPK = 16
N_MOD = 6
EPS = 1e-6

SUBLANES = 8
LANES = 128
VMEM_LIMIT = 56 * 1024 * 1024

COND_ROWS = 16
CTX_ROW = 8
SEQ_TILE = 256
S5_CHUNK = 128
S5_LANES = 1024
PREP_TILE = 512
PEER_TM = 512
PEER_SB = 512
STAGE_UNROLL = 4
U_SLOTS = 6
U_AHEAD = 4
V_SLOTS = 3
RANK_ROWS = 64
GATE_ROWS = 64


def _gelu(x):
    return 0.5 * x * (1.0 + jnp.tanh(0.7978845608028654 * (x + 0.044715 * (x * x * x))))


def _split_bf16(x):
    hi = x.astype(BF16)
    lo = (x - hi.astype(F32)).astype(BF16)
    return hi, lo


def _dot(a, b):
    return jnp.dot(a, b, preferred_element_type=F32)


def _params(sem):
    return pltpu.CompilerParams(dimension_semantics=sem, vmem_limit_bytes=VMEM_LIMIT)


def _mod_kernel(cond_ref, w_ref, b_ref, o_ref):
    c = cond_ref[...]
    s = c * jax.nn.sigmoid(c)
    shi, slo = _split_bf16(s)
    whi, wlo = _split_bf16(w_ref[...])
    o_ref[...] = _dot(shi, whi) + _dot(slo, whi) + _dot(shi, wlo) + b_ref[...]


def _mod_vectors(cond, w_mod, b_mod):
    n = w_mod.shape[1]
    bn = D_MODEL
    return pl.pallas_call(
        _mod_kernel,
        grid=(n // bn,),
        in_specs=[
            pl.BlockSpec((COND_ROWS, D_MODEL), lambda k: (0, 0)),
            pl.BlockSpec((D_MODEL, bn), lambda k: (0, k)),
            pl.BlockSpec((1, bn), lambda k: (0, k)),
        ],
        out_specs=pl.BlockSpec((COND_ROWS, bn), lambda k: (0, k)),
        out_shape=jax.ShapeDtypeStruct((COND_ROWS, n), F32),
        compiler_params=_params(("arbitrary",)),
        name="mod",
    )(cond, w_mod, b_mod.reshape(1, n))


def _disc_kernel(lr_ref, li_ref, ldt_ref, br_ref, bi_ref, ar_ref, ai_ref, bbr_ref, bbi_ref):
    lr = lr_ref[...]
    li = li_ref[...]
    dt = jnp.exp(ldt_ref[...])
    mag = jnp.exp(lr * dt)
    ar = mag * jnp.cos(li * dt)
    ai = mag * jnp.sin(li * dt)
    den = lr * lr + li * li
    nr = ar - 1.0
    ni = ai
    fr = (nr * lr + ni * li) / den
    fi = (ni * lr - nr * li) / den
    ar_ref[...] = ar
    ai_ref[...] = ai
    br = br_ref[...]
    bi = bi_ref[...]
    frb = fr[:, None, :]
    fib = fi[:, None, :]
    bbr_ref[...] = frb * br - fib * bi
    bbi_ref[...] = frb * bi + fib * br


def _discretise(lam_re, lam_im, log_dt, b_re, b_im):
    rows = 2 * SSM_G
    lr = lam_re.reshape(rows, SSM_P)
    li = lam_im.reshape(rows, SSM_P)
    ldt = jnp.broadcast_to(log_dt.reshape(rows, 1), (rows, SSM_P))
    br = jnp.swapaxes(b_re, -1, -2).reshape(rows, SSM_H, SSM_P)
    bi = jnp.swapaxes(b_im, -1, -2).reshape(rows, SSM_H, SSM_P)
    small = jax.ShapeDtypeStruct((rows, SSM_P), F32)
    big = jax.ShapeDtypeStruct((rows, SSM_H, SSM_P), F32)
    return pl.pallas_call(_disc_kernel, out_shape=(small, small, big, big), name="disc")(lr, li, ldt, br, bi)


def _s5_weights(ar, ai, bbr, bbi, c_re, c_im):
    half_g = SSM_G // 2
    eye = jnp.eye(half_g, dtype=F32)
    a_re = ar.reshape(2, 1, SSM_N)
    a_im = ai.reshape(2, 1, SSM_N)

    def bmat(t):
        t = t.reshape(2, 2, half_g, SSM_H, SSM_P)
        m = jnp.einsum("djghp,gk->djghkp", t, eye)
        return m.reshape(2, 2, half_g * SSM_H, half_g * SSM_P)

    def cmat(t):
        t = t.reshape(2, 2, half_g, SSM_H, SSM_P)
        m = jnp.einsum("djghp,gk->djgpkh", t, eye)
        return m.reshape(2, 2, half_g * SSM_P, half_g * SSM_H)

    wb = jnp.concatenate([bmat(bbr), bmat(bbi)], axis=-1).astype(BF16)
    wc_re = cmat(c_re).astype(BF16)
    wc_im = cmat(c_im).astype(BF16)
    return a_re, a_im, wb, wc_re, wc_im


def _inproj_kernel(x_ref, sh_ref, sc_ref, g_ref, w_ref, zs_ref, zp_ref):
    x = x_ref[0]
    ms = jnp.mean(x * x, axis=-1, keepdims=True)
    y = x * lax.rsqrt(ms + EPS) * g_ref[...]
    h = y * (1.0 + sc_ref[0, 0]) + sh_ref[0, 0]
    z = _dot(h.astype(BF16), w_ref[...])
    zs_ref[...] = z[:, :D_SSM]
    zp_ref[0] = z[:, D_SSM:]


def _mod_spec(k, per_batch):
    if per_batch:
        return pl.BlockSpec((1, 1, 1, D_MODEL), lambda b, i: (b, k, 0, 0))
    return pl.BlockSpec((1, 1, 1, D_MODEL), lambda b, i: (CTX_ROW, k, 0, 0))


def _seq_tile(L, grid):
    if grid:
        tm = min(L, SEQ_TILE)
        assert L % tm == 0 and tm % GRID_W == 0
        return tm
    assert L <= SEQ_TILE, "sequence pooling needs the whole sequence in one tile"
    return L


def _inproj(x, mods, g, w_in, per_batch, grid):
    B, L, _ = x.shape
    tm = _seq_tile(L, grid)
    return pl.pallas_call(
        _inproj_kernel,
        grid=(B, L // tm),
        in_specs=[
            pl.BlockSpec((1, tm, D_MODEL), lambda b, i: (b, i, 0)),
            _mod_spec(0, per_batch),
            _mod_spec(1, per_batch),
            pl.BlockSpec((1, D_MODEL), lambda b, i: (0, 0)),
            pl.BlockSpec((D_MODEL, D_MODEL), lambda b, i: (0, 0)),
        ],
        out_specs=[
            pl.BlockSpec((tm, D_SSM), lambda b, i: (i, b)),
            pl.BlockSpec((1, tm, D_POOL), lambda b, i: (b, i, 0)),
        ],
        out_shape=[
            jax.ShapeDtypeStruct((L, B * D_SSM), F32),
            jax.ShapeDtypeStruct((B, L, D_POOL), F32),
        ],
        compiler_params=_params(("parallel", "parallel")),
        name="inproj",
    )(x, mods, mods, g.reshape(1, D_MODEL), w_in)


def _s5_kernel(u_ref, are_ref, aim_ref, wb_ref, wcr_ref, wci_ref, h0r_ref, h0i_ref,
               y_ref, sr_ref, si_ref, bur_ref, bui_ref, cr_ref, ci_ref, *, reverse, n_chunks):
    c = pl.program_id(1)
    rows = S5_CHUNK * SUBLANES
    half = SSM_N // 2
    half_in = D_SSM // 2

    @pl.when(c == 0)
    def _():
        cr_ref[...] = h0r_ref[...]
        ci_ref[...] = h0i_ref[...]

    u = u_ref[...].reshape(rows, D_SSM).astype(BF16)
    for j in range(2):
        r = _dot(u[:, j * half_in:(j + 1) * half_in], wb_ref[0, j])
        bur_ref[:, j * half:(j + 1) * half] = r[:, :half]
        bui_ref[:, j * half:(j + 1) * half] = r[:, half:]

    for lc in range(SSM_N // S5_LANES):
        ls = slice(lc * S5_LANES, (lc + 1) * S5_LANES)
        ar = jnp.broadcast_to(are_ref[0, :, ls], (SUBLANES, S5_LANES))
        ai = jnp.broadcast_to(aim_ref[0, :, ls], (SUBLANES, S5_LANES))

        def step(i, carry, ls=ls, ar=ar, ai=ai):
            hr, hi = carry
            t = (S5_CHUNK - 1 - i) if reverse else i
            row = pl.multiple_of(t * SUBLANES, SUBLANES)
            nhr = ar * hr - ai * hi + bur_ref[pl.ds(row, SUBLANES), ls]
            nhi = ar * hi + ai * hr + bui_ref[pl.ds(row, SUBLANES), ls]
            bur_ref[pl.ds(row, SUBLANES), ls] = nhr
            bui_ref[pl.ds(row, SUBLANES), ls] = nhi
            return nhr, nhi

        hr, hi = lax.fori_loop(0, S5_CHUNK, step, (cr_ref[:, ls], ci_ref[:, ls]), unroll=4)
        cr_ref[:, ls] = hr
        ci_ref[:, ls] = hi

    ys = []
    for j in range(2):
        hs = slice(j * half, (j + 1) * half)
        yr = _dot(bur_ref[:, hs].astype(BF16), wcr_ref[0, j])
        yi = _dot(bui_ref[:, hs].astype(BF16), wci_ref[0, j])
        ys.append(yr - yi)
    y_ref[...] = jnp.concatenate(ys, axis=-1).reshape(S5_CHUNK, SUBLANES, D_SSM)

    @pl.when(c == n_chunks - 1)
    def _():
        sr_ref[...] = cr_ref[...]
        si_ref[...] = ci_ref[...]


def _s5_direction(zs, a_re, a_im, wb, wc_re, wc_im, h0r, h0i, d):
    L, B, _ = zs.shape
    n_chunks = L // S5_CHUNK
    reverse = d == 1
    cidx = (lambda g, c: n_chunks - 1 - c) if reverse else (lambda g, c: c)
    rows = S5_CHUNK * SUBLANES
    kern = functools.partial(_s5_kernel, reverse=reverse, n_chunks=n_chunks)
    state = jax.ShapeDtypeStruct((B, SSM_N), F32)
    return pl.pallas_call(
        kern,
        grid=(B // SUBLANES, n_chunks),
        in_specs=[
            pl.BlockSpec((S5_CHUNK, SUBLANES, D_SSM), lambda g, c: (cidx(g, c), g, 0)),
            pl.BlockSpec((1, 1, SSM_N), lambda g, c: (d, 0, 0)),
            pl.BlockSpec((1, 1, SSM_N), lambda g, c: (d, 0, 0)),
            pl.BlockSpec((1, 2, D_SSM // 2, SSM_N), lambda g, c: (d, 0, 0, 0)),
            pl.BlockSpec((1, 2, SSM_N // 2, D_SSM // 2), lambda g, c: (d, 0, 0, 0)),
            pl.BlockSpec((1, 2, SSM_N // 2, D_SSM // 2), lambda g, c: (d, 0, 0, 0)),
            pl.BlockSpec((SUBLANES, SSM_N), lambda g, c: (g, 0)),
            pl.BlockSpec((SUBLANES, SSM_N), lambda g, c: (g, 0)),
        ],
        out_specs=[
            pl.BlockSpec((S5_CHUNK, SUBLANES, D_SSM), lambda g, c: (cidx(g, c), g, 0)),
            pl.BlockSpec((SUBLANES, SSM_N), lambda g, c: (g, 0)),
            pl.BlockSpec((SUBLANES, SSM_N), lambda g, c: (g, 0)),
        ],
        out_shape=[jax.ShapeDtypeStruct((L, B, D_SSM), F32), state, state],
        scratch_shapes=[
            pltpu.VMEM((rows, SSM_N), F32),
            pltpu.VMEM((rows, SSM_N), F32),
            pltpu.VMEM((SUBLANES, SSM_N), F32),
            pltpu.VMEM((SUBLANES, SSM_N), F32),
        ],
        compiler_params=_params(("parallel", "arbitrary")),
        name="s5_bwd" if reverse else "s5_fwd",
    )(zs, a_re, a_im, wb, wc_re, wc_im, h0r, h0i)


def _pool_tables(grid, tm):
    seg = GRID_W if grid else tm
    t = np.arange(tm)
    pos = t % seg
    base = t - pos
    mats, invs = [], []
    for w in POOL_WINDOWS:
        lo = np.clip(pos - w // 2, 0, seg)
        hi = np.clip(pos + (w - w // 2), 0, seg)
        s = t[None, :]
        m = (s >= (base + lo)[:, None]) & (s < (base + hi)[:, None])
        mats.append(m.astype(np.float32))
        invs.append(np.broadcast_to((1.0 / (hi - lo).astype(np.float32))[:, None], (tm, LANES)))
    return jnp.asarray(np.stack(mats), dtype=BF16), jnp.asarray(np.stack(invs), dtype=F32)


def _mix_kernel(x_ref, u_ref, zp_ref, yf_ref, yb_ref, g1_ref, sh2_ref, sc2_ref, d_ref, wglu_ref,
                pm_ref, pinv_ref, wpool_ref, pscale_ref, wout_ref, n2_ref, x1_ref, h2t_ref):
    x = x_ref[0]
    y = d_ref[...] * u_ref[...] + yf_ref[...] + yb_ref[...]
    g = _gelu(y)
    ys = g * jax.nn.sigmoid(_dot(g.astype(BF16), wglu_ref[...]))

    zp = zp_ref[0]
    parts = [ys]
    for gi in range(POOL_G):
        zg = zp[:, gi * POOL_C:(gi + 1) * POOL_C]
        hi, lo = _split_bf16(zg)
        pm = pm_ref[gi]
        win = _dot(pm, hi) + _dot(pm, lo)
        pooled = win * pinv_ref[gi] - zg
        og = _dot(pooled.astype(BF16), wpool_ref[gi])
        parts.append(og * pscale_ref[:, gi * POOL_C:(gi + 1) * POOL_C])
    mix = _dot(jnp.concatenate(parts, axis=-1).astype(BF16), wout_ref[...])

    x1 = x + g1_ref[0, 0] * mix
    x1_ref[...] = x1
    ms = jnp.mean(x1 * x1, axis=-1, keepdims=True)
    h2 = x1 * lax.rsqrt(ms + EPS) * n2_ref[...]
    h2 = h2 * (1.0 + sc2_ref[0, 0]) + sh2_ref[0, 0]
    h2t_ref[...] = h2.T.astype(BF16)


def _mix(x, zs2d, zp, yf2d, yb2d, mods, ssm_d, w_glu, w_pool, pool_scale, w_out, norm2_g, per_batch, grid):
    B, L, _ = x.shape
    tm = _seq_tile(L, grid)
    nt = L // tm
    pm, pinv = _pool_tables(grid, tm)
    tm_spec = pl.BlockSpec((tm, D_SSM), lambda b, i: (i, b))
    const2 = lambda b, i: (0, 0)
    const3 = lambda b, i: (0, 0, 0)
    return pl.pallas_call(
        _mix_kernel,
        grid=(B, nt),
        in_specs=[
            pl.BlockSpec((1, tm, D_MODEL), lambda b, i: (b, i, 0)),
            tm_spec,
            pl.BlockSpec((1, tm, D_POOL), lambda b, i: (b, i, 0)),
            tm_spec,
            tm_spec,
            _mod_spec(2, per_batch),
            _mod_spec(3, per_batch),
            _mod_spec(4, per_batch),
            pl.BlockSpec((1, D_SSM), const2),
            pl.BlockSpec((D_SSM, D_SSM), const2),
            pl.BlockSpec((POOL_G, tm, tm), const3),
            pl.BlockSpec((POOL_G, tm, LANES), const3),
            pl.BlockSpec((POOL_G, POOL_C, POOL_C), const3),
            pl.BlockSpec((1, D_POOL), const2),
            pl.BlockSpec((D_MODEL, D_MODEL), const2),
            pl.BlockSpec((1, D_MODEL), const2),
        ],
        out_specs=[
            pl.BlockSpec((tm, D_MODEL), lambda b, i: (b * nt + i, 0)),
            pl.BlockSpec((D_MODEL, tm), lambda b, i: (0, b * nt + i)),
        ],
        out_shape=[
            jax.ShapeDtypeStruct((B * L, D_MODEL), F32),
            jax.ShapeDtypeStruct((D_MODEL, B * L), BF16),
        ],
        compiler_params=_params(("parallel", "parallel")),
        name="mix",
    )(x, zs2d, zp, yf2d, yb2d, mods, mods, mods, ssm_d.reshape(1, D_SSM), w_glu.astype(BF16),
      pm, pinv, w_pool.astype(BF16), pool_scale.reshape(1, D_POOL), w_out.astype(BF16),
      norm2_g.reshape(1, D_MODEL))


def _sort_pairs(n):
    pairs = []

    def merge(lo, hi, r):
        step = r * 2
        if step < hi - lo:
            merge(lo, hi, step)
            merge(lo + r, hi, step)
            for i in range(lo + r, hi - r, step):
                pairs.append((i, i + r))
        else:
            pairs.append((lo, lo + r))

    def sort(lo, hi):
        if hi - lo >= 1:
            mid = lo + (hi - lo) // 2
            sort(lo, mid)
            sort(mid + 1, hi)
            merge(lo, hi, 1)

    sort(0, n - 1)
    return pairs


_SORT16 = _sort_pairs(PEER_TOPK)


def _vmax(a, b):
    if a is None:
        return b
    if b is None:
        return a
    return jnp.maximum(a, b)


def _vmin(a, b):
    if a is None or b is None:
        return None
    return jnp.minimum(a, b)


def _sort16_desc(w):
    w = list(w)
    for i, j in _SORT16:
        w[i], w[j] = _vmax(w[i], w[j]), _vmin(w[i], w[j])
    return w


def _merge_top16(a, b):
    k = PEER_TOPK
    w = [_vmax(a[i], b[k - 1 - i]) for i in range(k)]
    d = k // 2
    while d >= 1:
        for i in range(k):
            if i & d == 0:
                w[i], w[i + d] = _vmax(w[i], w[i + d]), _vmin(w[i], w[i + d])
        d //= 2
    return w


def _top16_of_128(load):
    cur = None
    for grp in range(PEER_NKEYS // PEER_TOPK):
        s = _sort16_desc([load(grp * PEER_TOPK + i) for i in range(PEER_TOPK)])
        cur = s if cur is None else _merge_top16(cur, s)
    return cur


def _top16_pair_sums(sv0, sv1):
    k = PEER_TOPK
    pad = lambda row: row + [None] * (k - len(row))
    cur = [sv0[0] + sv1[j] for j in range(k)]
    for i in range(1, k // 2):
        cur = _merge_top16(cur, pad([sv0[i] + sv1[j] for j in range(k // (i + 1))]))
    return _merge_top16(cur, pad([sv0[i] + sv1[0] for i in range(k // 2, k)]))


def _count_sorted(v, above):
    sel = jnp.where
    g8 = above(v[7])
    g4 = above(sel(g8, v[11], v[3]))
    g2 = above(sel(g8, sel(g4, v[13], v[9]), sel(g4, v[5], v[1])))
    g1 = above(sel(g8, sel(g4, sel(g2, v[14], v[12]), sel(g2, v[10], v[8])),
                   sel(g4, sel(g2, v[6], v[4]), sel(g2, v[2], v[0]))))
    low = sel(g8, 8.0, 0.0) + sel(g4, 4.0, 0.0) + sel(g2, 2.0, 0.0) + sel(g1, 1.0, 0.0)
    return sel(above(v[15]), 16.0, low)


def _as_words(x):
    return pltpu.bitcast(x.astype(BF16), jnp.uint32)


def _as_bf16(words):
    return pltpu.bitcast(words, BF16)


def _peer_prep_kernel(ht_ref, wq_ref, wka_ref, kb_ref, a_ref, m_ref, b_ref, r_ref, q_ref, sa_ref, st_ref):
    tm = ht_ref.shape[1]
    half = PEER_HEADS * PEER_DHALF
    q_ref[...] = _dot(wq_ref[...], ht_ref[...]).astype(BF16)
    for side in range(2):
        s = _dot(wka_ref[side], q_ref[side * half:(side + 1) * half, :])
        sa_ref[side] = s.reshape(PEER_NKEYS, PEER_HEADS, tm)

    for lc in range(tm // LANES):
        ls = slice(lc * LANES, (lc + 1) * LANES)
        sv0 = _top16_of_128(lambda n: sa_ref[0, n, :, ls])
        sv1 = _top16_of_128(lambda n: sa_ref[1, n, :, ls])
        top = _top16_pair_sums(sv0, sv1)
        tau = top[PEER_TOPK - 1]
        best = sv0[0] + sv1[0]
        z = jnp.exp(top[0] - best)
        for k in range(1, PEER_TOPK):
            z = z + jnp.exp(top[k] - best)
        inv_z = 1.0 / z
        best0 = sv0[0]
        for k in range(PEER_TOPK):
            st_ref[k, :, ls] = sv1[k]

        def first_key(n, carry, lc=lc, ls=ls, sv1=sv1, tau=tau, best0=best0, inv_z=inv_z):
            s0 = sa_ref[0, n, :, ls]
            a_ref[lc, n] = jnp.exp(s0 - best0) * inv_z
            m_ref[lc, n] = _count_sorted(sv1, lambda p: s0 + p >= tau)
            return carry

        lax.fori_loop(0, PEER_NKEYS, first_key, 0, unroll=8)

    for h in range(PEER_HEADS):
        s1 = _dot(kb_ref[h], q_ref[half + h * PEER_DHALF:half + (h + 1) * PEER_DHALF, :])
        for lc in range(tm // LANES):
            ls = slice(lc * LANES, (lc + 1) * LANES)
            sorted1 = [st_ref[k, h:h + 1, ls] for k in range(PEER_TOPK)]
            for part in range(PEER_NKEYS // RANK_ROWS):
                x = s1[part * RANK_ROWS:(part + 1) * RANK_ROWS, ls]
                words = slice(part * RANK_ROWS // 2, (part + 1) * RANK_ROWS // 2)
                b_ref[lc, h, words, :] = _as_words(jnp.exp(x - sorted1[0]))
                r_ref[lc, h, words, :] = _as_words(_count_sorted(sorted1, lambda p: p > x))


def _peer_tables(peer_wq, sub_keys):
    wq = peer_wq.reshape(D_MODEL, PEER_HEADS, 2, PEER_DHALF)
    wq_t = jnp.transpose(wq, (2, 1, 3, 0)).reshape(2 * PEER_HEADS * PEER_DHALF, D_MODEL).astype(BF16)
    eye = jnp.eye(PEER_HEADS, dtype=F32)
    wka = jnp.einsum("hsnk,hg->snhgk", sub_keys, eye)
    wka = wka.reshape(2, PEER_NKEYS * PEER_HEADS, PEER_HEADS * PEER_DHALF).astype(BF16)
    kb = sub_keys[:, 1].astype(BF16)
    return wq_t, wka, kb


def _peer_prep(h2t, wq_t, wka, kb):
    T = h2t.shape[1]
    tm = PREP_TILE
    half = PEER_HEADS * PEER_DHALF
    first = jax.ShapeDtypeStruct((T // LANES, PEER_NKEYS, PEER_HEADS, LANES), F32)
    second = jax.ShapeDtypeStruct((T // LANES, PEER_HEADS, PEER_NKEYS // 2, LANES), jnp.uint32)
    fspec = pl.BlockSpec((tm // LANES, PEER_NKEYS, PEER_HEADS, LANES), lambda i: (i, 0, 0, 0))
    sspec = pl.BlockSpec((tm // LANES, PEER_HEADS, PEER_NKEYS // 2, LANES), lambda i: (i, 0, 0, 0))
    return pl.pallas_call(
        _peer_prep_kernel,
        grid=(T // tm,),
        in_specs=[
            pl.BlockSpec((D_MODEL, tm), lambda i: (0, i)),
            pl.BlockSpec((2 * half, D_MODEL), lambda i: (0, 0)),
            pl.BlockSpec((2, PEER_NKEYS * PEER_HEADS, half), lambda i: (0, 0, 0)),
            pl.BlockSpec((PEER_HEADS, PEER_NKEYS, PEER_DHALF), lambda i: (0, 0, 0)),
        ],
        out_specs=[fspec, fspec, sspec, sspec],
        out_shape=[first, first, second, second],
        scratch_shapes=[
            pltpu.VMEM((2 * half, tm), BF16),
            pltpu.VMEM((2, PEER_NKEYS, PEER_HEADS, tm), F32),
            pltpu.VMEM((PEER_TOPK, PEER_HEADS, tm), F32),
        ],
        compiler_params=_params(("parallel",)),
        name="peer_prep",
    )(h2t, wq_t, wka, kb)


def _peer_dense_kernel(ht_ref, a_ref, m_ref, b_ref, r_ref, x1_ref, g2_ref, fg_ref, u_hbm, vt_hbm,
                       y_ref, ubuf, vbuf, usem, vsem, acc_ref, act0_ref, act1_ref, w0_ref, w1_ref):
    act_refs = (act0_ref, act1_ref)
    w_refs = (w0_ref, w1_ref)
    tm = ht_ref.shape[1]
    n_chunks = tm // LANES
    n_sub = PEER_N // PEER_SB
    n_pairs = n_sub // 2
    keys_per_sub = PEER_SB // PEER_NKEYS
    sub_words = PEER_SB // 2
    half_rows = D_MODEL // 2

    def u_copy(s):
        slot = s % U_SLOTS
        return pltpu.make_async_copy(u_hbm.at[pl.ds(s * PEER_SB, PEER_SB), :], ubuf.at[slot], usem.at[slot])

    def v_copy(pair):
        slot = pair % V_SLOTS
        return pltpu.make_async_copy(vt_hbm.at[pair], vbuf.at[slot], vsem.at[slot])

    def activations(s, aslot):
        act = _dot(ubuf[s % U_SLOTS], ht_ref[...])
        for tc in range(n_chunks):
            act_refs[aslot][tc] = act[:, tc * LANES:(tc + 1) * LANES]

    def gated(s, aslot, wslot, pos):
        for tc in range(n_chunks):
            a_rows = [a_ref[tc, s * keys_per_sub + k] for k in range(keys_per_sub)]
            m_rows = [m_ref[tc, s * keys_per_sub + k] for k in range(keys_per_sub)]
            for part in range(PEER_NKEYS // GATE_ROWS):
                pwords = slice(part * GATE_ROWS // 2, (part + 1) * GATE_ROWS // 2)
                gates = [None] * keys_per_sub
                for h in range(PEER_HEADS):
                    b = _as_bf16(b_ref[tc, h, pwords, :])
                    r = _as_bf16(r_ref[tc, h, pwords, :])
                    for k in range(keys_per_sub):
                        a_row = jnp.broadcast_to(a_rows[k][h:h + 1, :], (GATE_ROWS, LANES)).astype(BF16)
                        m_row = jnp.broadcast_to(m_rows[k][h:h + 1, :], (GATE_ROWS, LANES)).astype(BF16)
                        term = jnp.where(r < m_row, b, jnp.zeros_like(b)) * a_row
                        gates[k] = term if gates[k] is None else gates[k] + term
                for k in range(keys_per_sub):
                    row0 = k * PEER_NKEYS + part * GATE_ROWS
                    g = _gelu(act_refs[aslot][tc, row0:row0 + GATE_ROWS, :].astype(BF16))
                    word0 = pos * sub_words + row0 // 2
                    w_refs[wslot][tc, word0:word0 + GATE_ROWS // 2, :] = pltpu.bitcast(g * gates[k], jnp.uint32)

    def accumulate(pair, wslot, half):
        w = _as_bf16(jnp.concatenate([w_refs[wslot][tc] for tc in range(n_chunks)], axis=1))
        rows = slice(half * half_rows, (half + 1) * half_rows)
        acc_ref[rows, :] += _dot(vbuf[pair % V_SLOTS, rows, :], w)

    def stage_dma(s, jj):
        static = isinstance(s, int)

        def when(cond, fn):
            if static:
                if cond:
                    fn()
            else:
                pl.when(cond)(fn)

        when(s + U_AHEAD < n_sub, lambda: u_copy(s + U_AHEAD).start())
        if jj % 2 == 0:
            when((s + 1) // 2 < n_pairs, lambda: v_copy((s + 1) // 2).start())
        when(s < n_sub, lambda: u_copy(s).wait())
        if jj % 2 == 0:
            when(s >= 3, lambda: v_copy((s - 3) // 2).wait())

    def stage_compute(s, jj, with_activations=True, with_gate=True):
        if with_activations:
            activations(s, (jj + 1) % 2)
        if with_gate:
            gated(s - 1, jj % 2, (jj // 2) % 2, jj % 2)
        accumulate((s - 3) // 2, (jj // 2 + 1) % 2, jj % 2)

    acc_ref[...] = jnp.zeros_like(acc_ref)
    w_refs[1][...] = jnp.zeros_like(w_refs[1])

    @pl.when(pl.program_id(0) == 0)
    def _():
        vbuf[(-1) % V_SLOTS] = jnp.zeros(vbuf.shape[1:], vbuf.dtype)

    for s in range(U_AHEAD + 1):
        u_copy(s).start()
    v_copy(0).start()
    u_copy(0).wait()
    activations(0, 0)

    def stages(t, carry):
        for jj in range(STAGE_UNROLL):
            s = 1 + t * STAGE_UNROLL + jj
            stage_dma(s, jj)
            stage_compute(s, jj)
        return carry

    lax.fori_loop(0, n_sub // STAGE_UNROLL, stages, 0)
    for s in (n_sub + 1, n_sub + 2):
        jj = (s - 1) % STAGE_UNROLL
        stage_dma(s, jj)
        stage_compute(s, jj, with_activations=False, with_gate=False)

    out = x1_ref[...] + g2_ref[0, 0] * acc_ref[...].T
    ms = jnp.mean(out * out, axis=-1, keepdims=True)
    y_ref[...] = out * lax.rsqrt(ms + EPS) * fg_ref[...]


def _peer_dense(h2t, u_bf, vt_sub, a, m, b, r, x1, mods, final_g, tokens_per_row):
    T = h2t.shape[1]
    tm = PEER_TM
    if tokens_per_row is None:
        g2_map = lambda i: (CTX_ROW, 5, 0, 0)
    else:
        g2_map = lambda i: ((i * tm) // tokens_per_row, 5, 0, 0)
    fspec = pl.BlockSpec((tm // LANES, PEER_NKEYS, PEER_HEADS, LANES), lambda i: (i, 0, 0, 0))
    sspec = pl.BlockSpec((tm // LANES, PEER_HEADS, PEER_NKEYS // 2, LANES), lambda i: (i, 0, 0, 0))
    return pl.pallas_call(
        _peer_dense_kernel,
        grid=(T // tm,),
        in_specs=[
            pl.BlockSpec((D_MODEL, tm), lambda i: (0, i)),
            fspec, fspec, sspec, sspec,
            pl.BlockSpec((tm, D_MODEL), lambda i: (i, 0)),
            pl.BlockSpec((1, 1, 1, D_MODEL), g2_map),
            pl.BlockSpec((1, D_MODEL), lambda i: (0, 0)),
            pl.BlockSpec(memory_space=pl.ANY),
            pl.BlockSpec(memory_space=pl.ANY),
        ],
        out_specs=pl.BlockSpec((tm, D_MODEL), lambda i: (i, 0)),
        out_shape=jax.ShapeDtypeStruct((T, D_MODEL), F32),
        scratch_shapes=[
            pltpu.VMEM((U_SLOTS, PEER_SB, D_MODEL), BF16),
            pltpu.VMEM((V_SLOTS, D_MODEL, 2 * PEER_SB), BF16),
            pltpu.SemaphoreType.DMA((U_SLOTS,)),
            pltpu.SemaphoreType.DMA((V_SLOTS,)),
            pltpu.VMEM((D_MODEL, tm), F32),
            pltpu.VMEM((tm // LANES, PEER_SB, LANES), F32),
            pltpu.VMEM((tm // LANES, PEER_SB, LANES), F32),
            pltpu.VMEM((tm // LANES, PEER_SB, LANES), jnp.uint32),
            pltpu.VMEM((tm // LANES, PEER_SB, LANES), jnp.uint32),
        ],
        compiler_params=_params(("arbitrary",)),
        name="peer_dense",
    )(h2t, a, m, b, r, x1, mods, final_g.reshape(1, D_MODEL), u_bf, vt_sub)


def _trunk(x, mods, h0r, h0i, s5w, p, tables, final_g, per_batch, grid):
    B, L, _ = x.shape
    a_re, a_im, wb, wc_re, wc_im = s5w
    zs2d, zp = _inproj(x, mods, p["norm1_g"], p["w_in"], per_batch, grid)
    zs = zs2d.reshape(L, B, D_SSM)
    yf, sfr, sfi = _s5_direction(zs, a_re, a_im, wb, wc_re, wc_im, h0r[:, 0], h0i[:, 0], 0)
    yb, sbr, sbi = _s5_direction(zs, a_re, a_im, wb, wc_re, wc_im, h0r[:, 1], h0i[:, 1], 1)
    x1, h2t = _mix(x, zs2d, zp, yf.reshape(L, B * D_SSM), yb.reshape(L, B * D_SSM), mods, p["ssm_d"],
                   p["w_glu"], p["w_pool"], p["pool_scale"], p["w_out"], p["norm2_g"], per_batch, grid)
    wq_t, wka, kb, u_bf, vt_bf = tables
    a, m, b, r = _peer_prep(h2t, wq_t, wka, kb)
    y = _peer_dense(h2t, u_bf, vt_bf, a, m, b, r, x1, mods, final_g, L if per_batch else None)
    new_re = jnp.stack([sfr, sbr], axis=1).reshape(B, 2, SSM_G, SSM_P)
    new_im = jnp.stack([sfi, sbi], axis=1).reshape(B, 2, SSM_G, SSM_P)
    return y.reshape(B, L, D_MODEL), new_re, new_im


def kernel(x_prompt, x_sample, state_ssm_re, state_ssm_im, c, c_ctx, norm1_g, w_mod, b_mod, w_in,
           ssm_lambda_re, ssm_lambda_im, ssm_log_dt, ssm_b_re, ssm_b_im, ssm_c_re, ssm_c_im, ssm_d,
           w_glu, w_pool, pool_scale, w_out, norm2_g, peer_wq, peer_subkeys, peer_u, peer_v, final_g):
    depth = w_mod.shape[0]
    assert depth == 1, "single trunk layer"
    l = 0
    n_dec = c.shape[0]
    cond = jnp.zeros((COND_ROWS, D_MODEL), F32).at[:n_dec].set(c).at[CTX_ROW].set(c_ctx)
    mods = _mod_vectors(cond, w_mod[l], b_mod[l]).reshape(COND_ROWS, N_MOD, 1, D_MODEL)

    ar, ai, bbr, bbi = _discretise(ssm_lambda_re[l], ssm_lambda_im[l], ssm_log_dt[l], ssm_b_re[l], ssm_b_im[l])
    s5w = _s5_weights(ar, ai, bbr, bbi, ssm_c_re[l], ssm_c_im[l])

    wq_t, wka, kb = _peer_tables(peer_wq[l], peer_subkeys[l])
    vt_sub = jnp.transpose(peer_v[l].reshape(PEER_N // (2 * PEER_SB), 2 * PEER_SB, D_MODEL), (0, 2, 1)).astype(BF16)
    tables = (wq_t, wka, kb, peer_u[l].astype(BF16), vt_sub)

    p = {"norm1_g": norm1_g[l], "w_in": w_in[l].astype(BF16), "ssm_d": ssm_d[l], "w_glu": w_glu[l],
         "w_pool": w_pool[l], "pool_scale": pool_scale[l], "w_out": w_out[l], "norm2_g": norm2_g[l]}

    bp = x_prompt.shape[0]
    zeros = jnp.zeros((bp, 2, SSM_N), F32)
    y_prompt, new_re, new_im = _trunk(x_prompt, mods, zeros, zeros, s5w, p, tables, final_g, False, False)
    h0r = state_ssm_re[:, l].reshape(n_dec, 2, SSM_N)
    h0i = state_ssm_im[:, l].reshape(n_dec, 2, SSM_N)
    y_sample, _, _ = _trunk(x_sample, mods, h0r, h0i, s5w, p, tables, final_g, True, True)
    return (y_prompt, y_sample, new_re[:, None], new_im[:, None])
```

```python
import functools

import numpy as np
import jax
import jax.numpy as jnp
from jax import lax
from jax.experimental import pallas as pl
from jax.experimental.pallas import tpu as pltpu

F32 = jnp.float32
BF16 = jnp.bfloat16

D_MODEL = 1024
D_SSM = 512
D_POOL = 512
SSM_H = 16
SSM_G = 32
SSM_P = 64
SSM_N = SSM_G * SSM_P
POOL_WINDOWS = (2, 4, 8, 16)
POOL_G = 4
POOL_C = 128
GRID_W = 64
PEER_HEADS = 8
PEER_NKEYS = 128
PEER_N = PEER_NKEYS * PEER_NKEYS
PEER_DHALF = 128
PEER_TOPK = 16
N_MOD = 6
EPS = 1e-6

SUBLANES = 8
LANES = 128
VMEM_LIMIT = 56 * 1024 * 1024

COND_ROWS = 16
CTX_ROW = 8
SEQ_TILE = 256
INPROJ_TILE = 512
S5_CHUNK = 128
S5_LANES = 1024
PREP_TILE = 512
PEER_TM = 512
PEER_SB = 512
STAGE_UNROLL = 4
U_SLOTS = 6
U_AHEAD = 4
V_SLOTS = 3
RANK_ROWS = 64
GATE_ROWS = 64


def _gelu(x):
    return 0.5 * x * (1.0 + jnp.tanh(0.7978845608028654 * (x + 0.044715 * (x * x * x))))


def _split_bf16(x):
    hi = x.astype(BF16)
    lo = (x - hi.astype(F32)).astype(BF16)
    return hi, lo


def _dot(a, b):
    return jnp.dot(a, b, preferred_element_type=F32)


def _params(sem):
    return pltpu.CompilerParams(dimension_semantics=sem, vmem_limit_bytes=VMEM_LIMIT)


def _mod_kernel(cond_ref, w_ref, b_ref, o_ref):
    c = cond_ref[...]
    s = c * jax.nn.sigmoid(c)
    shi, slo = _split_bf16(s)
    whi, wlo = _split_bf16(w_ref[...])
    o_ref[...] = _dot(shi, whi) + _dot(slo, whi) + _dot(shi, wlo) + b_ref[...]


def _mod_vectors(cond, w_mod, b_mod):
    n = w_mod.shape[1]
    bn = D_MODEL
    return pl.pallas_call(
        _mod_kernel,
        grid=(n // bn,),
        in_specs=[
            pl.BlockSpec((COND_ROWS, D_MODEL), lambda k: (0, 0)),
            pl.BlockSpec((D_MODEL, bn), lambda k: (0, k)),
            pl.BlockSpec((1, bn), lambda k: (0, k)),
        ],
        out_specs=pl.BlockSpec((COND_ROWS, bn), lambda k: (0, k)),
        out_shape=jax.ShapeDtypeStruct((COND_ROWS, n), F32),
        compiler_params=_params(("arbitrary",)),
        name="mod",
    )(cond, w_mod, b_mod.reshape(1, n))


def _disc_kernel(lr_ref, li_ref, ldt_ref, br_ref, bi_ref, ar_ref, ai_ref, bbr_ref, bbi_ref):
    lr = lr_ref[...]
    li = li_ref[...]
    dt = jnp.exp(ldt_ref[...])
    mag = jnp.exp(lr * dt)
    ar = mag * jnp.cos(li * dt)
    ai = mag * jnp.sin(li * dt)
    den = lr * lr + li * li
    nr = ar - 1.0
    ni = ai
    fr = (nr * lr + ni * li) / den
    fi = (ni * lr - nr * li) / den
    ar_ref[...] = ar
    ai_ref[...] = ai
    br = br_ref[...]
    bi = bi_ref[...]
    frb = fr[:, None, :]
    fib = fi[:, None, :]
    bbr_ref[...] = frb * br - fib * bi
    bbi_ref[...] = frb * bi + fib * br


def _discretise(lam_re, lam_im, log_dt, b_re, b_im):
    rows = 2 * SSM_G
    lr = lam_re.reshape(rows, SSM_P)
    li = lam_im.reshape(rows, SSM_P)
    ldt = jnp.broadcast_to(log_dt.reshape(rows, 1), (rows, SSM_P))
    br = jnp.swapaxes(b_re, -1, -2).reshape(rows, SSM_H, SSM_P)
    bi = jnp.swapaxes(b_im, -1, -2).reshape(rows, SSM_H, SSM_P)
    small = jax.ShapeDtypeStruct((rows, SSM_P), F32)
    big = jax.ShapeDtypeStruct((rows, SSM_H, SSM_P), F32)
    return pl.pallas_call(_disc_kernel, out_shape=(small, small, big, big), name="disc")(lr, li, ldt, br, bi)


def _s5_weights(ar, ai, bbr, bbi, c_re, c_im):
    half_g = SSM_G // 2
    eye = jnp.eye(half_g, dtype=F32)
    a_re = ar.reshape(2, 1, SSM_N)
    a_im = ai.reshape(2, 1, SSM_N)

    def bmat(t):
        t = t.reshape(2, 2, half_g, SSM_H, SSM_P)
        m = jnp.einsum("djghp,gk->djghkp", t, eye)
        return m.reshape(2, 2, half_g * SSM_H, half_g * SSM_P)

    def cmat(t):
        t = t.reshape(2, 2, half_g, SSM_H, SSM_P)
        m = jnp.einsum("djghp,gk->djgpkh", t, eye)
        return m.reshape(2, 2, half_g * SSM_P, half_g * SSM_H)

    wb = jnp.concatenate([bmat(bbr), bmat(bbi)], axis=-1).astype(BF16)
    wc_re = cmat(c_re).astype(BF16)
    wc_im = cmat(c_im).astype(BF16)
    return a_re, a_im, wb, wc_re, wc_im


def _inproj_kernel(x_ref, sh_ref, sc_ref, g_ref, w_ref, zs_ref, zp_ref):
    x = x_ref[0]
    ms = jnp.mean(x * x, axis=-1, keepdims=True)
    y = x * lax.rsqrt(ms + EPS) * g_ref[...]
    h = y * (1.0 + sc_ref[0, 0]) + sh_ref[0, 0]
    z = _dot(h.astype(BF16), w_ref[...])
    zs_ref[...] = z[:, :D_SSM]
    zp_ref[0] = z[:, D_SSM:]


def _mod_spec(k, per_batch):
    if per_batch:
        return pl.BlockSpec((1, 1, 1, D_MODEL), lambda b, i: (b, k, 0, 0))
    return pl.BlockSpec((1, 1, 1, D_MODEL), lambda b, i: (CTX_ROW, k, 0, 0))


def _seq_tile(L, grid):
    if grid:
        tm = min(L, SEQ_TILE)
        assert L % tm == 0 and tm % GRID_W == 0
        return tm
    assert L <= SEQ_TILE, "sequence pooling needs the whole sequence in one tile"
    return L


def _inproj(x, mods, g, w_in, per_batch):
    B, L, _ = x.shape
    tm = min(L, INPROJ_TILE)
    assert L % tm == 0
    return pl.pallas_call(
        _inproj_kernel,
        grid=(B, L // tm),
        in_specs=[
            pl.BlockSpec((1, tm, D_MODEL), lambda b, i: (b, i, 0)),
            _mod_spec(0, per_batch),
            _mod_spec(1, per_batch),
            pl.BlockSpec((1, D_MODEL), lambda b, i: (0, 0)),
            pl.BlockSpec((D_MODEL, D_MODEL), lambda b, i: (0, 0)),
        ],
        out_specs=[
            pl.BlockSpec((tm, D_SSM), lambda b, i: (i, b)),
            pl.BlockSpec((1, tm, D_POOL), lambda b, i: (b, i, 0)),
        ],
        out_shape=[
            jax.ShapeDtypeStruct((L, B * D_SSM), F32),
            jax.ShapeDtypeStruct((B, L, D_POOL), F32),
        ],
        compiler_params=_params(("parallel", "parallel")),
        name="inproj",
    )(x, mods, mods, g.reshape(1, D_MODEL), w_in)


def _s5_kernel(u_ref, are_ref, aim_ref, wb_ref, wcr_ref, wci_ref, h0r_ref, h0i_ref,
               y_ref, sr_ref, si_ref, bur_ref, bui_ref, cr_ref, ci_ref, *, reverse, n_chunks):
    c = pl.program_id(1)
    rows = S5_CHUNK * SUBLANES
    half = SSM_N // 2
    half_in = D_SSM // 2

    @pl.when(c == 0)
    def _():
        cr_ref[...] = h0r_ref[...]
        ci_ref[...] = h0i_ref[...]

    u = u_ref[...].reshape(rows, D_SSM).astype(BF16)
    for j in range(2):
        r = _dot(u[:, j * half_in:(j + 1) * half_in], wb_ref[0, j])
        bur_ref[:, j * half:(j + 1) * half] = r[:, :half]
        bui_ref[:, j * half:(j + 1) * half] = r[:, half:]

    for lc in range(SSM_N // S5_LANES):
        ls = slice(lc * S5_LANES, (lc + 1) * S5_LANES)
        ar = jnp.broadcast_to(are_ref[0, :, ls], (SUBLANES, S5_LANES))
        ai = jnp.broadcast_to(aim_ref[0, :, ls], (SUBLANES, S5_LANES))

        def step(i, carry, ls=ls, ar=ar, ai=ai):
            hr, hi = carry
            t = (S5_CHUNK - 1 - i) if reverse else i
            row = pl.multiple_of(t * SUBLANES, SUBLANES)
            nhr = ar * hr - ai * hi + bur_ref[pl.ds(row, SUBLANES), ls]
            nhi = ar * hi + ai * hr + bui_ref[pl.ds(row, SUBLANES), ls]
            bur_ref[pl.ds(row, SUBLANES), ls] = nhr
            bui_ref[pl.ds(row, SUBLANES), ls] = nhi
            return nhr, nhi

        hr, hi = lax.fori_loop(0, S5_CHUNK, step, (cr_ref[:, ls], ci_ref[:, ls]), unroll=4)
        cr_ref[:, ls] = hr
        ci_ref[:, ls] = hi

    ys = []
    for j in range(2):
        hs = slice(j * half, (j + 1) * half)
        yr = _dot(bur_ref[:, hs].astype(BF16), wcr_ref[0, j])
        yi = _dot(bui_ref[:, hs].astype(BF16), wci_ref[0, j])
        ys.append(yr - yi)
    y_ref[...] = jnp.concatenate(ys, axis=-1).reshape(S5_CHUNK, SUBLANES, D_SSM)

    @pl.when(c == n_chunks - 1)
    def _():
        sr_ref[...] = cr_ref[...]
        si_ref[...] = ci_ref[...]


def _s5_direction(zs, a_re, a_im, wb, wc_re, wc_im, h0r, h0i, d):
    L, B, _ = zs.shape
    assert L % S5_CHUNK == 0 and B % SUBLANES == 0
    n_chunks = L // S5_CHUNK
    reverse = d == 1
    cidx = (lambda g, c: n_chunks - 1 - c) if reverse else (lambda g, c: c)
    rows = S5_CHUNK * SUBLANES
    kern = functools.partial(_s5_kernel, reverse=reverse, n_chunks=n_chunks)
    state = jax.ShapeDtypeStruct((B, SSM_N), F32)
    return pl.pallas_call(
        kern,
        grid=(B // SUBLANES, n_chunks),
        in_specs=[
            pl.BlockSpec((S5_CHUNK, SUBLANES, D_SSM), lambda g, c: (cidx(g, c), g, 0)),
            pl.BlockSpec((1, 1, SSM_N), lambda g, c: (d, 0, 0)),
            pl.BlockSpec((1, 1, SSM_N), lambda g, c: (d, 0, 0)),
            pl.BlockSpec((1, 2, D_SSM // 2, SSM_N), lambda g, c: (d, 0, 0, 0)),
            pl.BlockSpec((1, 2, SSM_N // 2, D_SSM // 2), lambda g, c: (d, 0, 0, 0)),
            pl.BlockSpec((1, 2, SSM_N // 2, D_SSM // 2), lambda g, c: (d, 0, 0, 0)),
            pl.BlockSpec((SUBLANES, SSM_N), lambda g, c: (g, 0)),
            pl.BlockSpec((SUBLANES, SSM_N), lambda g, c: (g, 0)),
        ],
        out_specs=[
            pl.BlockSpec((S5_CHUNK, SUBLANES, D_SSM), lambda g, c: (cidx(g, c), g, 0)),
            pl.BlockSpec((SUBLANES, SSM_N), lambda g, c: (g, 0)),
            pl.BlockSpec((SUBLANES, SSM_N), lambda g, c: (g, 0)),
        ],
        out_shape=[jax.ShapeDtypeStruct((L, B, D_SSM), F32), state, state],
        scratch_shapes=[
            pltpu.VMEM((rows, SSM_N), F32),
            pltpu.VMEM((rows, SSM_N), F32),
            pltpu.VMEM((SUBLANES, SSM_N), F32),
            pltpu.VMEM((SUBLANES, SSM_N), F32),
        ],
        compiler_params=_params(("parallel", "arbitrary")),
        name="s5_bwd" if reverse else "s5_fwd",
    )(zs, a_re, a_im, wb, wc_re, wc_im, h0r, h0i)


def _pool_tables(grid, tm):
    seg = GRID_W if grid else tm
    t = np.arange(tm)
    pos = t % seg
    base = t - pos
    mats, invs = [], []
    for w in POOL_WINDOWS:
        lo = np.clip(pos - w // 2, 0, seg)
        hi = np.clip(pos + (w - w // 2), 0, seg)
        s = t[None, :]
        m = (s >= (base + lo)[:, None]) & (s < (base + hi)[:, None])
        mats.append(m.astype(np.float32))
        invs.append(np.broadcast_to((1.0 / (hi - lo).astype(np.float32))[:, None], (tm, LANES)))
    return jnp.asarray(np.stack(mats), dtype=BF16), jnp.asarray(np.stack(invs), dtype=F32)


def _mix_kernel(x_ref, u_ref, zp_ref, yf_ref, yb_ref, g1_ref, sh2_ref, sc2_ref, d_ref, wglu_ref,
                pm_ref, pinv_ref, wpool_ref, pscale_ref, wout_ref, n2_ref, x1_ref, h2t_ref):
    x = x_ref[0]
    y = d_ref[...] * u_ref[...] + yf_ref[...] + yb_ref[...]
    g = _gelu(y)
    ys = g * jax.nn.sigmoid(_dot(g.astype(BF16), wglu_ref[...]))

    zp = zp_ref[0]
    parts = [ys]
    for gi in range(POOL_G):
        zg = zp[:, gi * POOL_C:(gi + 1) * POOL_C]
        hi, lo = _split_bf16(zg)
        pm = pm_ref[gi]
        win = _dot(pm, hi) + _dot(pm, lo)
        pooled = win * pinv_ref[gi] - zg
        og = _dot(pooled.astype(BF16), wpool_ref[gi])
        parts.append(og * pscale_ref[:, gi * POOL_C:(gi + 1) * POOL_C])
    mix = _dot(jnp.concatenate(parts, axis=-1).astype(BF16), wout_ref[...])

    x1 = x + g1_ref[0, 0] * mix
    x1_ref[...] = x1
    ms = jnp.mean(x1 * x1, axis=-1, keepdims=True)
    h2 = x1 * lax.rsqrt(ms + EPS) * n2_ref[...]
    h2 = h2 * (1.0 + sc2_ref[0, 0]) + sh2_ref[0, 0]
    h2t_ref[...] = h2.T.astype(BF16)


def _mix(x, zs2d, zp, yf2d, yb2d, mods, ssm_d, w_glu, w_pool, pool_scale, w_out, norm2_g, per_batch, grid):
    B, L, _ = x.shape
    tm = _seq_tile(L, grid)
    nt = L // tm
    pm, pinv = _pool_tables(grid, tm)
    tm_spec = pl.BlockSpec((tm, D_SSM), lambda b, i: (i, b))
    const2 = lambda b, i: (0, 0)
    const3 = lambda b, i: (0, 0, 0)
    return pl.pallas_call(
        _mix_kernel,
        grid=(B, nt),
        in_specs=[
            pl.BlockSpec((1, tm, D_MODEL), lambda b, i: (b, i, 0)),
            tm_spec,
            pl.BlockSpec((1, tm, D_POOL), lambda b, i: (b, i, 0)),
            tm_spec,
            tm_spec,
            _mod_spec(2, per_batch),
            _mod_spec(3, per_batch),
            _mod_spec(4, per_batch),
            pl.BlockSpec((1, D_SSM), const2),
            pl.BlockSpec((D_SSM, D_SSM), const2),
            pl.BlockSpec((POOL_G, tm, tm), const3),
            pl.BlockSpec((POOL_G, tm, LANES), const3),
            pl.BlockSpec((POOL_G, POOL_C, POOL_C), const3),
            pl.BlockSpec((1, D_POOL), const2),
            pl.BlockSpec((D_MODEL, D_MODEL), const2),
            pl.BlockSpec((1, D_MODEL), const2),
        ],
        out_specs=[
            pl.BlockSpec((tm, D_MODEL), lambda b, i: (b * nt + i, 0)),
            pl.BlockSpec((D_MODEL, tm), lambda b, i: (0, b * nt + i)),
        ],
        out_shape=[
            jax.ShapeDtypeStruct((B * L, D_MODEL), F32),
            jax.ShapeDtypeStruct((D_MODEL, B * L), BF16),
        ],
        compiler_params=_params(("parallel", "parallel")),
        name="mix",
    )(x, zs2d, zp, yf2d, yb2d, mods, mods, mods, ssm_d.reshape(1, D_SSM), w_glu.astype(BF16),
      pm, pinv, w_pool.astype(BF16), pool_scale.reshape(1, D_POOL), w_out.astype(BF16),
      norm2_g.reshape(1, D_MODEL))


def _sort_pairs(n):
    pairs = []

    def merge(lo, hi, r):
        step = r * 2
        if step < hi - lo:
            merge(lo, hi, step)
            merge(lo + r, hi, step)
            for i in range(lo + r, hi - r, step):
                pairs.append((i, i + r))
        else:
            pairs.append((lo, lo + r))

    def sort(lo, hi):
        if hi - lo >= 1:
            mid = lo + (hi - lo) // 2
            sort(lo, mid)
            sort(mid + 1, hi)
            merge(lo, hi, 1)

    sort(0, n - 1)
    return pairs


_SORT16 = _sort_pairs(PEER_TOPK)


def _vmax(a, b):
    if a is None:
        return b
    if b is None:
        return a
    return jnp.maximum(a, b)


def _vmin(a, b):
    if a is None or b is None:
        return None
    return jnp.minimum(a, b)


def _sort16_desc(w):
    w = list(w)
    for i, j in _SORT16:
        w[i], w[j] = _vmax(w[i], w[j]), _vmin(w[i], w[j])
    return w


def _merge_top16(a, b):
    k = PEER_TOPK
    w = [_vmax(a[i], b[k - 1 - i]) for i in range(k)]
    d = k // 2
    while d >= 1:
        for i in range(k):
            if i & d == 0:
                w[i], w[i + d] = _vmax(w[i], w[i + d]), _vmin(w[i], w[i + d])
        d //= 2
    return w


def _top16_of_128(load):
    cur = None
    for grp in range(PEER_NKEYS // PEER_TOPK):
        s = _sort16_desc([load(grp * PEER_TOPK + i) for i in range(PEER_TOPK)])
        cur = s if cur is None else _merge_top16(cur, s)
    return cur


def _top16_pair_sums(sv0, sv1):
    k = PEER_TOPK
    pad = lambda row: row + [None] * (k - len(row))
    cur = [sv0[0] + sv1[j] for j in range(k)]
    for i in range(1, k // 2):
        cur = _merge_top16(cur, pad([sv0[i] + sv1[j] for j in range(k // (i + 1))]))
    return _merge_top16(cur, pad([sv0[i] + sv1[0] for i in range(k // 2, k)]))


def _count_sorted(v, above):
    sel = jnp.where
    g8 = above(v[7])
    g4 = above(sel(g8, v[11], v[3]))
    g2 = above(sel(g8, sel(g4, v[13], v[9]), sel(g4, v[5], v[1])))
    g1 = above(sel(g8, sel(g4, sel(g2, v[14], v[12]), sel(g2, v[10], v[8])),
                   sel(g4, sel(g2, v[6], v[4]), sel(g2, v[2], v[0]))))
    low = sel(g8, 8.0, 0.0) + sel(g4, 4.0, 0.0) + sel(g2, 2.0, 0.0) + sel(g1, 1.0, 0.0)
    return sel(above(v[15]), 16.0, low)


def _as_words(x):
    return pltpu.bitcast(x.astype(BF16), jnp.uint32)


def _as_bf16(words):
    return pltpu.bitcast(words, BF16)


def _peer_prep_kernel(ht_ref, wq_ref, wka_ref, kb_ref, a_ref, m_ref, b_ref, r_ref, q_ref, sa_ref, st_ref):
    tm = ht_ref.shape[1]
    half = PEER_HEADS * PEER_DHALF
    q_ref[...] = _dot(wq_ref[...], ht_ref[...]).astype(BF16)
    for side in range(2):
        s = _dot(wka_ref[side], q_ref[side * half:(side + 1) * half, :])
        sa_ref[side] = s.reshape(PEER_NKEYS, PEER_HEADS, tm)

    for lc in range(tm // LANES):
        ls = slice(lc * LANES, (lc + 1) * LANES)
        sv0 = _top16_of_128(lambda n: sa_ref[0, n, :, ls])
        sv1 = _top16_of_128(lambda n: sa_ref[1, n, :, ls])
        top = _top16_pair_sums(sv0, sv1)
        tau = top[PEER_TOPK - 1]
        best = sv0[0] + sv1[0]
        z = jnp.exp(top[0] - best)
        for k in range(1, PEER_TOPK):
            z = z + jnp.exp(top[k] - best)
        inv_z = 1.0 / z
        best0 = sv0[0]
        for k in range(PEER_TOPK):
            st_ref[k, :, ls] = sv1[k]

        def first_key(n, carry, lc=lc, ls=ls, sv1=sv1, tau=tau, best0=best0, inv_z=inv_z):
            s0 = sa_ref[0, n, :, ls]
            a_ref[lc, n] = jnp.exp(s0 - best0) * inv_z
            m_ref[lc, n] = _count_sorted(sv1, lambda p: s0 + p >= tau)
            return carry

        lax.fori_loop(0, PEER_NKEYS, first_key, 0, unroll=8)

    for h in range(PEER_HEADS):
        s1 = _dot(kb_ref[h], q_ref[half + h * PEER_DHALF:half + (h + 1) * PEER_DHALF, :])
        for lc in range(tm // LANES):
            ls = slice(lc * LANES, (lc + 1) * LANES)
            sorted1 = [st_ref[k, h:h + 1, ls] for k in range(PEER_TOPK)]
            for part in range(PEER_NKEYS // RANK_ROWS):
                x = s1[part * RANK_ROWS:(part + 1) * RANK_ROWS, ls]
                words = slice(part * RANK_ROWS // 2, (part + 1) * RANK_ROWS // 2)
                b_ref[lc, h, words, :] = _as_words(jnp.exp(x - sorted1[0]))
                r_ref[lc, h, words, :] = _as_words(_count_sorted(sorted1, lambda p: p > x))


def _peer_tables(peer_wq, sub_keys):
    wq = peer_wq.reshape(D_MODEL, PEER_HEADS, 2, PEER_DHALF)
    wq_t = jnp.transpose(wq, (2, 1, 3, 0)).reshape(2 * PEER_HEADS * PEER_DHALF, D_MODEL).astype(BF16)
    eye = jnp.eye(PEER_HEADS, dtype=F32)
    wka = jnp.einsum("hsnk,hg->snhgk", sub_keys, eye)
    wka = wka.reshape(2, PEER_NKEYS * PEER_HEADS, PEER_HEADS * PEER_DHALF).astype(BF16)
    kb = sub_keys[:, 1].astype(BF16)
    return wq_t, wka, kb


def _peer_prep(h2t, wq_t, wka, kb):
    T = h2t.shape[1]
    tm = PREP_TILE
    assert T % tm == 0
    half = PEER_HEADS * PEER_DHALF
    first = jax.ShapeDtypeStruct((T // LANES, PEER_NKEYS, PEER_HEADS, LANES), F32)
    second = jax.ShapeDtypeStruct((T // LANES, PEER_HEADS, PEER_NKEYS // 2, LANES), jnp.uint32)
    fspec = pl.BlockSpec((tm // LANES, PEER_NKEYS, PEER_HEADS, LANES), lambda i: (i, 0, 0, 0))
    sspec = pl.BlockSpec((tm // LANES, PEER_HEADS, PEER_NKEYS // 2, LANES), lambda i: (i, 0, 0, 0))
    return pl.pallas_call(
        _peer_prep_kernel,
        grid=(T // tm,),
        in_specs=[
            pl.BlockSpec((D_MODEL, tm), lambda i: (0, i)),
            pl.BlockSpec((2 * half, D_MODEL), lambda i: (0, 0)),
            pl.BlockSpec((2, PEER_NKEYS * PEER_HEADS, half), lambda i: (0, 0, 0)),
            pl.BlockSpec((PEER_HEADS, PEER_NKEYS, PEER_DHALF), lambda i: (0, 0, 0)),
        ],
        out_specs=[fspec, fspec, sspec, sspec],
        out_shape=[first, first, second, second],
        scratch_shapes=[
            pltpu.VMEM((2 * half, tm), BF16),
            pltpu.VMEM((2, PEER_NKEYS, PEER_HEADS, tm), F32),
            pltpu.VMEM((PEER_TOPK, PEER_HEADS, tm), F32),
        ],
        compiler_params=_params(("parallel",)),
        name="peer_prep",
    )(h2t, wq_t, wka, kb)


def _peer_dense_kernel(ht_ref, a_ref, m_ref, b_ref, r_ref, x1_ref, g2_ref, fg_ref, u_hbm, vt_hbm,
                       y_ref, ubuf, vbuf, usem, vsem, acc_ref, act0_ref, act1_ref, w0_ref, w1_ref):
    act_refs = (act0_ref, act1_ref)
    w_refs = (w0_ref, w1_ref)
    tm = ht_ref.shape[1]
    n_chunks = tm // LANES
    n_sub = PEER_N // PEER_SB
    n_pairs = n_sub // 2
    keys_per_sub = PEER_SB // PEER_NKEYS
    sub_words = PEER_SB // 2
    half_rows = D_MODEL // 2

    def u_copy(s):
        slot = s % U_SLOTS
        return pltpu.make_async_copy(u_hbm.at[pl.ds(s * PEER_SB, PEER_SB), :], ubuf.at[slot], usem.at[slot])

    def v_copy(pair):
        slot = pair % V_SLOTS
        return pltpu.make_async_copy(vt_hbm.at[pair], vbuf.at[slot], vsem.at[slot])

    def activations(s, aslot):
        act = _dot(ubuf[s % U_SLOTS], ht_ref[...])
        for tc in range(n_chunks):
            act_refs[aslot][tc] = act[:, tc * LANES:(tc + 1) * LANES]

    def gated(s, aslot, wslot, pos):
        for tc in range(n_chunks):
            a_rows = [a_ref[tc, s * keys_per_sub + k] for k in range(keys_per_sub)]
            m_rows = [m_ref[tc, s * keys_per_sub + k] for k in range(keys_per_sub)]
            for part in range(PEER_NKEYS // GATE_ROWS):
                pwords = slice(part * GATE_ROWS // 2, (part + 1) * GATE_ROWS // 2)
                gates = [None] * keys_per_sub
                for h in range(PEER_HEADS):
                    b = _as_bf16(b_ref[tc, h, pwords, :])
                    r = _as_bf16(r_ref[tc, h, pwords, :])
                    for k in range(keys_per_sub):
                        a_row = jnp.broadcast_to(a_rows[k][h:h + 1, :], (GATE_ROWS, LANES)).astype(BF16)
                        m_row = jnp.broadcast_to(m_rows[k][h:h + 1, :], (GATE_ROWS, LANES)).astype(BF16)
                        term = jnp.where(r < m_row, b, jnp.zeros_like(b)) * a_row
                        gates[k] = term if gates[k] is None else gates[k] + term
                for k in range(keys_per_sub):
                    row0 = k * PEER_NKEYS + part * GATE_ROWS
                    g = _gelu(act_refs[aslot][tc, row0:row0 + GATE_ROWS, :].astype(BF16))
                    word0 = pos * sub_words + row0 // 2
                    w_refs[wslot][tc, word0:word0 + GATE_ROWS // 2, :] = pltpu.bitcast(g * gates[k], jnp.uint32)

    def accumulate(pair, wslot, half):
        w = _as_bf16(jnp.concatenate([w_refs[wslot][tc] for tc in range(n_chunks)], axis=1))
        rows = slice(half * half_rows, (half + 1) * half_rows)
        acc_ref[rows, :] += _dot(vbuf[pair % V_SLOTS, rows, :], w)

    def stage_dma(s, jj):
        static = isinstance(s, int)

        def when(cond, fn):
            if static:
                if cond:
                    fn()
            else:
                pl.when(cond)(fn)

        when(s + U_AHEAD < n_sub, lambda: u_copy(s + U_AHEAD).start())
        if jj % 2 == 0:
            when((s + 1) // 2 < n_pairs, lambda: v_copy((s + 1) // 2).start())
        when(s < n_sub, lambda: u_copy(s).wait())
        if jj % 2 == 0:
            when(s >= 3, lambda: v_copy((s - 3) // 2).wait())

    def stage_compute(s, jj, with_activations=True, with_gate=True):
        if with_activations:
            activations(s, (jj + 1) % 2)
        if with_gate:
            gated(s - 1, jj % 2, (jj // 2) % 2, jj % 2)
        accumulate((s - 3) // 2, (jj // 2 + 1) % 2, jj % 2)

    acc_ref[...] = jnp.zeros_like(acc_ref)
    w_refs[1][...] = jnp.zeros_like(w_refs[1])

    @pl.when(pl.program_id(0) == 0)
    def _():
        vbuf[(-1) % V_SLOTS] = jnp.zeros(vbuf.shape[1:], vbuf.dtype)

    for s in range(U_AHEAD + 1):
        u_copy(s).start()
    v_copy(0).start()
    u_copy(0).wait()
    activations(0, 0)

    def stages(t, carry):
        for jj in range(STAGE_UNROLL):
            s = 1 + t * STAGE_UNROLL + jj
            stage_dma(s, jj)
            stage_compute(s, jj)
        return carry

    lax.fori_loop(0, n_sub // STAGE_UNROLL, stages, 0)
    for s in (n_sub + 1, n_sub + 2):
        jj = (s - 1) % STAGE_UNROLL
        stage_dma(s, jj)
        stage_compute(s, jj, with_activations=False, with_gate=False)

    out = x1_ref[...] + g2_ref[0, 0] * acc_ref[...].T
    ms = jnp.mean(out * out, axis=-1, keepdims=True)
    y_ref[...] = out * lax.rsqrt(ms + EPS) * fg_ref[...]


def _peer_dense(h2t, u_bf, vt_sub, a, m, b, r, x1, mods, final_g, tokens_per_row):
    T = h2t.shape[1]
    tm = PEER_TM
    assert T % tm == 0
    if tokens_per_row is None:
        g2_map = lambda i: (CTX_ROW, 5, 0, 0)
    else:
        assert tokens_per_row % tm == 0, "a token tile must not straddle two requests"
        g2_map = lambda i: ((i * tm) // tokens_per_row, 5, 0, 0)
    fspec = pl.BlockSpec((tm // LANES, PEER_NKEYS, PEER_HEADS, LANES), lambda i: (i, 0, 0, 0))
    sspec = pl.BlockSpec((tm // LANES, PEER_HEADS, PEER_NKEYS // 2, LANES), lambda i: (i, 0, 0, 0))
    return pl.pallas_call(
        _peer_dense_kernel,
        grid=(T // tm,),
        in_specs=[
            pl.BlockSpec((D_MODEL, tm), lambda i: (0, i)),
            fspec, fspec, sspec, sspec,
            pl.BlockSpec((tm, D_MODEL), lambda i: (i, 0)),
            pl.BlockSpec((1, 1, 1, D_MODEL), g2_map),
            pl.BlockSpec((1, D_MODEL), lambda i: (0, 0)),
            pl.BlockSpec(memory_space=pl.ANY),
            pl.BlockSpec(memory_space=pl.ANY),
        ],
        out_specs=pl.BlockSpec((tm, D_MODEL), lambda i: (i, 0)),
        out_shape=jax.ShapeDtypeStruct((T, D_MODEL), F32),
        scratch_shapes=[
            pltpu.VMEM((U_SLOTS, PEER_SB, D_MODEL), BF16),
            pltpu.VMEM((V_SLOTS, D_MODEL, 2 * PEER_SB), BF16),
            pltpu.SemaphoreType.DMA((U_SLOTS,)),
            pltpu.SemaphoreType.DMA((V_SLOTS,)),
            pltpu.VMEM((D_MODEL, tm), F32),
            pltpu.VMEM((tm // LANES, PEER_SB, LANES), F32),
            pltpu.VMEM((tm // LANES, PEER_SB, LANES), F32),
            pltpu.VMEM((tm // LANES, PEER_SB, LANES), jnp.uint32),
            pltpu.VMEM((tm // LANES, PEER_SB, LANES), jnp.uint32),
        ],
        compiler_params=_params(("arbitrary",)),
        name="peer_dense",
    )(h2t, a, m, b, r, x1, mods, final_g.reshape(1, D_MODEL), u_bf, vt_sub)


def _trunk(x, mods, h0r, h0i, s5w, p, tables, final_g, per_batch, grid):
    B, L, _ = x.shape
    a_re, a_im, wb, wc_re, wc_im = s5w
    zs2d, zp = _inproj(x, mods, p["norm1_g"], p["w_in"], per_batch)
    zs = zs2d.reshape(L, B, D_SSM)
    yf, sfr, sfi = _s5_direction(zs, a_re, a_im, wb, wc_re, wc_im, h0r[:, 0], h0i[:, 0], 0)
    yb, sbr, sbi = _s5_direction(zs, a_re, a_im, wb, wc_re, wc_im, h0r[:, 1], h0i[:, 1], 1)
    x1, h2t = _mix(x, zs2d, zp, yf.reshape(L, B * D_SSM), yb.reshape(L, B * D_SSM), mods, p["ssm_d"],
                   p["w_glu"], p["w_pool"], p["pool_scale"], p["w_out"], p["norm2_g"], per_batch, grid)
    wq_t, wka, kb, u_bf, vt_bf = tables
    a, m, b, r = _peer_prep(h2t, wq_t, wka, kb)
    y = _peer_dense(h2t, u_bf, vt_bf, a, m, b, r, x1, mods, final_g, L if per_batch else None)
    new_re = jnp.stack([sfr, sbr], axis=1).reshape(B, 2, SSM_G, SSM_P)
    new_im = jnp.stack([sfi, sbi], axis=1).reshape(B, 2, SSM_G, SSM_P)
    return y.reshape(B, L, D_MODEL), new_re, new_im


def kernel(x_prompt, x_sample, state_ssm_re, state_ssm_im, c, c_ctx, norm1_g, w_mod, b_mod, w_in,
           ssm_lambda_re, ssm_lambda_im, ssm_log_dt, ssm_b_re, ssm_b_im, ssm_c_re, ssm_c_im, ssm_d,
           w_glu, w_pool, pool_scale, w_out, norm2_g, peer_wq, peer_subkeys, peer_u, peer_v, final_g):
    depth = w_mod.shape[0]
    assert depth == 1, "single trunk layer"
    l = 0
    n_dec = c.shape[0]
    cond = jnp.zeros((COND_ROWS, D_MODEL), F32).at[:n_dec].set(c).at[CTX_ROW].set(c_ctx)
    mods = _mod_vectors(cond, w_mod[l], b_mod[l]).reshape(COND_ROWS, N_MOD, 1, D_MODEL)

    ar, ai, bbr, bbi = _discretise(ssm_lambda_re[l], ssm_lambda_im[l], ssm_log_dt[l], ssm_b_re[l], ssm_b_im[l])
    s5w = _s5_weights(ar, ai, bbr, bbi, ssm_c_re[l], ssm_c_im[l])

    wq_t, wka, kb = _peer_tables(peer_wq[l], peer_subkeys[l])
    vt_sub = jnp.transpose(peer_v[l].reshape(PEER_N // (2 * PEER_SB), 2 * PEER_SB, D_MODEL), (0, 2, 1)).astype(BF16)
    tables = (wq_t, wka, kb, peer_u[l].astype(BF16), vt_sub)

    p = {"norm1_g": norm1_g[l], "w_in": w_in[l].astype(BF16), "ssm_d": ssm_d[l], "w_glu": w_glu[l],
         "w_pool": w_pool[l], "pool_scale": pool_scale[l], "w_out": w_out[l], "norm2_g": norm2_g[l]}

    bp = x_prompt.shape[0]
    zeros = jnp.zeros((bp, 2, SSM_N), F32)
    y_prompt, new_re, new_im = _trunk(x_prompt, mods, zeros, zeros, s5w, p, tables, final_g, False, False)
    h0r = state_ssm_re[:, l].reshape(n_dec, 2, SSM_N)
    h0i = state_ssm_im[:, l].reshape(n_dec, 2, SSM_N)
    y_sample, _, _ = _trunk(x_sample, mods, h0r, h0i, s5w, p, tables, final_g, True, True)
    return (y_prompt, y_sample, new_re[:, None], new_im[:, None])
```

```python
import functools

import numpy as np
import jax
import jax.numpy as jnp
from jax import lax
from jax.experimental import pallas as pl
from jax.experimental.pallas import tpu as pltpu

F32 = jnp.float32
BF16 = jnp.bfloat16

D_MODEL = 1024
D_SSM = 512
D_POOL = 512
SSM_H = 16
SSM_G = 32
SSM_P = 64
SSM_N = SSM_G * SSM_P
POOL_WINDOWS = (2, 4, 8, 16)
POOL_G = 4
POOL_C = 128
GRID_W = 64
PEER_HEADS = 8
PEER_NKEYS = 128
PEER_N = PEER_NKEYS * PEER_NKEYS
PEER_DHALF = 128
PEER_TOPK = 16
N_MOD = 6
EPS = 1e-6

SUBLANES = 8
LANES = 128
VMEM_LIMIT = 56 * 1024 * 1024

COND_ROWS = 16
CTX_ROW = 8
SEQ_TILE = 256
INPROJ_TILE = 512
S5_CHUNK = 128
S5_LANES = 1024
PREP_TILE = 512
PEER_TM = 512
PEER_SB = 512
STAGE_UNROLL = 4
U_SLOTS = 6
U_AHEAD = 4
V_SLOTS = 3
RANK_ROWS = 64
GATE_ROWS = 64


def _gelu(x):
    u = x * (0.7978845608028654 + 0.035677408136300125 * (x * x))
    return 0.5 * x * (1.0 + jnp.tanh(u))


def _split_bf16(x):
    hi = x.astype(BF16)
    lo = (x - hi.astype(F32)).astype(BF16)
    return hi, lo


def _dot(a, b):
    return jnp.dot(a, b, preferred_element_type=F32)


def _params(sem):
    return pltpu.CompilerParams(dimension_semantics=sem, vmem_limit_bytes=VMEM_LIMIT)


def _mod_kernel(cond_ref, w_ref, b_ref, o_ref):
    c = cond_ref[...]
    s = c * jax.nn.sigmoid(c)
    shi, slo = _split_bf16(s)
    whi, wlo = _split_bf16(w_ref[...])
    o_ref[...] = _dot(shi, whi) + _dot(slo, whi) + _dot(shi, wlo) + b_ref[...]


def _mod_vectors(cond, w_mod, b_mod):
    n = w_mod.shape[1]
    bn = D_MODEL
    return pl.pallas_call(
        _mod_kernel,
        grid=(n // bn,),
        in_specs=[
            pl.BlockSpec((COND_ROWS, D_MODEL), lambda k: (0, 0)),
            pl.BlockSpec((D_MODEL, bn), lambda k: (0, k)),
            pl.BlockSpec((1, bn), lambda k: (0, k)),
        ],
        out_specs=pl.BlockSpec((COND_ROWS, bn), lambda k: (0, k)),
        out_shape=jax.ShapeDtypeStruct((COND_ROWS, n), F32),
        compiler_params=_params(("arbitrary",)),
        name="mod",
    )(cond, w_mod, b_mod.reshape(1, n))


def _disc_kernel(lr_ref, li_ref, ldt_ref, br_ref, bi_ref, ar_ref, ai_ref, bbr_ref, bbi_ref):
    lr = lr_ref[...]
    li = li_ref[...]
    dt = jnp.exp(ldt_ref[...])
    mag = jnp.exp(lr * dt)
    ar = mag * jnp.cos(li * dt)
    ai = mag * jnp.sin(li * dt)
    den = lr * lr + li * li
    nr = ar - 1.0
    ni = ai
    fr = (nr * lr + ni * li) / den
    fi = (ni * lr - nr * li) / den
    ar_ref[...] = ar
    ai_ref[...] = ai
    br = br_ref[...]
    bi = bi_ref[...]
    frb = fr[:, None, :]
    fib = fi[:, None, :]
    bbr_ref[...] = frb * br - fib * bi
    bbi_ref[...] = frb * bi + fib * br


def _discretise(lam_re, lam_im, log_dt, b_re, b_im):
    rows = 2 * SSM_G
    lr = lam_re.reshape(rows, SSM_P)
    li = lam_im.reshape(rows, SSM_P)
    ldt = jnp.broadcast_to(log_dt.reshape(rows, 1), (rows, SSM_P))
    br = jnp.swapaxes(b_re, -1, -2).reshape(rows, SSM_H, SSM_P)
    bi = jnp.swapaxes(b_im, -1, -2).reshape(rows, SSM_H, SSM_P)
    small = jax.ShapeDtypeStruct((rows, SSM_P), F32)
    big = jax.ShapeDtypeStruct((rows, SSM_H, SSM_P), F32)
    return pl.pallas_call(_disc_kernel, out_shape=(small, small, big, big), name="disc")(lr, li, ldt, br, bi)


def _s5_weights(ar, ai, bbr, bbi, c_re, c_im):
    half_g = SSM_G // 2
    eye = jnp.eye(half_g, dtype=F32)
    a_re = ar.reshape(2, 1, SSM_N)
    a_im = ai.reshape(2, 1, SSM_N)

    def bmat(t):
        t = t.reshape(2, 2, half_g, SSM_H, SSM_P)
        m = jnp.einsum("djghp,gk->djghkp", t, eye)
        return m.reshape(2, 2, half_g * SSM_H, half_g * SSM_P)

    def cmat(t):
        t = t.reshape(2, 2, half_g, SSM_H, SSM_P)
        m = jnp.einsum("djghp,gk->djgpkh", t, eye)
        return m.reshape(2, 2, half_g * SSM_P, half_g * SSM_H)

    wb = jnp.concatenate([bmat(bbr), bmat(bbi)], axis=-1).astype(BF16)
    wc_re = cmat(c_re).astype(BF16)
    wc_im = cmat(c_im).astype(BF16)
    return a_re, a_im, wb, wc_re, wc_im


def _inproj_kernel(x_ref, sh_ref, sc_ref, g_ref, w_ref, zs_ref, zp_ref):
    x = x_ref[0]
    ms = jnp.mean(x * x, axis=-1, keepdims=True)
    y = x * lax.rsqrt(ms + EPS) * g_ref[...]
    h = y * (1.0 + sc_ref[0, 0]) + sh_ref[0, 0]
    z = _dot(h.astype(BF16), w_ref[...])
    zs_ref[...] = z[:, :D_SSM]
    zp_ref[0] = z[:, D_SSM:]


def _mod_spec(k, per_batch):
    if per_batch:
        return pl.BlockSpec((1, 1, 1, D_MODEL), lambda b, i: (b, k, 0, 0))
    return pl.BlockSpec((1, 1, 1, D_MODEL), lambda b, i: (CTX_ROW, k, 0, 0))


def _seq_tile(L, grid):
    if grid:
        tm = min(L, SEQ_TILE)
        assert L % tm == 0 and tm % GRID_W == 0
        return tm
    assert L <= SEQ_TILE, "sequence pooling needs the whole sequence in one tile"
    return L


def _inproj(x, mods, g, w_in, per_batch):
    B, L, _ = x.shape
    tm = min(L, INPROJ_TILE)
    assert L % tm == 0
    return pl.pallas_call(
        _inproj_kernel,
        grid=(B, L // tm),
        in_specs=[
            pl.BlockSpec((1, tm, D_MODEL), lambda b, i: (b, i, 0)),
            _mod_spec(0, per_batch),
            _mod_spec(1, per_batch),
            pl.BlockSpec((1, D_MODEL), lambda b, i: (0, 0)),
            pl.BlockSpec((D_MODEL, D_MODEL), lambda b, i: (0, 0)),
        ],
        out_specs=[
            pl.BlockSpec((tm, D_SSM), lambda b, i: (i, b)),
            pl.BlockSpec((1, tm, D_POOL), lambda b, i: (b, i, 0)),
        ],
        out_shape=[
            jax.ShapeDtypeStruct((L, B * D_SSM), F32),
            jax.ShapeDtypeStruct((B, L, D_POOL), F32),
        ],
        compiler_params=_params(("parallel", "parallel")),
        name="inproj",
    )(x, mods, mods, g.reshape(1, D_MODEL), w_in)


def _s5_kernel(u_ref, are_ref, aim_ref, wb_ref, wcr_ref, wci_ref, h0r_ref, h0i_ref,
               y_ref, sr_ref, si_ref, bur_ref, bui_ref, cr_ref, ci_ref, *, reverse, n_chunks):
    c = pl.program_id(1)
    rows = S5_CHUNK * SUBLANES
    half = SSM_N // 2
    half_in = D_SSM // 2

    @pl.when(c == 0)
    def _():
        cr_ref[...] = h0r_ref[...]
        ci_ref[...] = h0i_ref[...]

    u = u_ref[...].reshape(rows, D_SSM).astype(BF16)
    for j in range(2):
        r = _dot(u[:, j * half_in:(j + 1) * half_in], wb_ref[0, j])
        bur_ref[:, j * half:(j + 1) * half] = r[:, :half]
        bui_ref[:, j * half:(j + 1) * half] = r[:, half:]

    for lc in range(SSM_N // S5_LANES):
        ls = slice(lc * S5_LANES, (lc + 1) * S5_LANES)
        ar = jnp.broadcast_to(are_ref[0, :, ls], (SUBLANES, S5_LANES))
        ai = jnp.broadcast_to(aim_ref[0, :, ls], (SUBLANES, S5_LANES))

        def step(i, carry, ls=ls, ar=ar, ai=ai):
            hr, hi = carry
            t = (S5_CHUNK - 1 - i) if reverse else i
            row = pl.multiple_of(t * SUBLANES, SUBLANES)
            nhr = ar * hr - ai * hi + bur_ref[pl.ds(row, SUBLANES), ls]
            nhi = ar * hi + ai * hr + bui_ref[pl.ds(row, SUBLANES), ls]
            bur_ref[pl.ds(row, SUBLANES), ls] = nhr
            bui_ref[pl.ds(row, SUBLANES), ls] = nhi
            return nhr, nhi

        hr, hi = lax.fori_loop(0, S5_CHUNK, step, (cr_ref[:, ls], ci_ref[:, ls]), unroll=4)
        cr_ref[:, ls] = hr
        ci_ref[:, ls] = hi

    ys = []
    for j in range(2):
        hs = slice(j * half, (j + 1) * half)
        yr = _dot(bur_ref[:, hs].astype(BF16), wcr_ref[0, j])
        yi = _dot(bui_ref[:, hs].astype(BF16), wci_ref[0, j])
        ys.append(yr - yi)
    y_ref[...] = jnp.concatenate(ys, axis=-1).reshape(S5_CHUNK, SUBLANES, D_SSM)

    @pl.when(c == n_chunks - 1)
    def _():
        sr_ref[...] = cr_ref[...]
        si_ref[...] = ci_ref[...]


def _s5_direction(zs, a_re, a_im, wb, wc_re, wc_im, h0r, h0i, d):
    L, B, _ = zs.shape
    assert L % S5_CHUNK == 0 and B % SUBLANES == 0
    n_chunks = L // S5_CHUNK
    reverse = d == 1
    cidx = (lambda g, c: n_chunks - 1 - c) if reverse else (lambda g, c: c)
    rows = S5_CHUNK * SUBLANES
    kern = functools.partial(_s5_kernel, reverse=reverse, n_chunks=n_chunks)
    state = jax.ShapeDtypeStruct((B, SSM_N), F32)
    return pl.pallas_call(
        kern,
        grid=(B // SUBLANES, n_chunks),
        in_specs=[
            pl.BlockSpec((S5_CHUNK, SUBLANES, D_SSM), lambda g, c: (cidx(g, c), g, 0)),
            pl.BlockSpec((1, 1, SSM_N), lambda g, c: (d, 0, 0)),
            pl.BlockSpec((1, 1, SSM_N), lambda g, c: (d, 0, 0)),
            pl.BlockSpec((1, 2, D_SSM // 2, SSM_N), lambda g, c: (d, 0, 0, 0)),
            pl.BlockSpec((1, 2, SSM_N // 2, D_SSM // 2), lambda g, c: (d, 0, 0, 0)),
            pl.BlockSpec((1, 2, SSM_N // 2, D_SSM // 2), lambda g, c: (d, 0, 0, 0)),
            pl.BlockSpec((SUBLANES, SSM_N), lambda g, c: (g, 0)),
            pl.BlockSpec((SUBLANES, SSM_N), lambda g, c: (g, 0)),
        ],
        out_specs=[
            pl.BlockSpec((S5_CHUNK, SUBLANES, D_SSM), lambda g, c: (cidx(g, c), g, 0)),
            pl.BlockSpec((SUBLANES, SSM_N), lambda g, c: (g, 0)),
            pl.BlockSpec((SUBLANES, SSM_N), lambda g, c: (g, 0)),
        ],
        out_shape=[jax.ShapeDtypeStruct((L, B, D_SSM), F32), state, state],
        scratch_shapes=[
            pltpu.VMEM((rows, SSM_N), F32),
            pltpu.VMEM((rows, SSM_N), F32),
            pltpu.VMEM((SUBLANES, SSM_N), F32),
            pltpu.VMEM((SUBLANES, SSM_N), F32),
        ],
        compiler_params=_params(("parallel", "arbitrary")),
        name="s5_bwd" if reverse else "s5_fwd",
    )(zs, a_re, a_im, wb, wc_re, wc_im, h0r, h0i)


def _pool_tables(grid, tm):
    seg = GRID_W if grid else tm
    t = np.arange(tm)
    pos = t % seg
    base = t - pos
    mats, invs = [], []
    for w in POOL_WINDOWS:
        lo = np.clip(pos - w // 2, 0, seg)
        hi = np.clip(pos + (w - w // 2), 0, seg)
        s = t[None, :]
        m = (s >= (base + lo)[:, None]) & (s < (base + hi)[:, None])
        mats.append(m.astype(np.float32))
        invs.append(np.broadcast_to((1.0 / (hi - lo).astype(np.float32))[:, None], (tm, LANES)))
    return jnp.asarray(np.stack(mats), dtype=BF16), jnp.asarray(np.stack(invs), dtype=F32)


def _mix_kernel(x_ref, u_ref, zp_ref, yf_ref, yb_ref, g1_ref, sh2_ref, sc2_ref, d_ref, wglu_ref,
                pm_ref, pinv_ref, wpool_ref, pscale_ref, wout_ref, n2_ref, x1_ref, h2t_ref):
    x = x_ref[0]
    y = d_ref[...] * u_ref[...] + yf_ref[...] + yb_ref[...]
    g = _gelu(y)
    ys = g * jax.nn.sigmoid(_dot(g.astype(BF16), wglu_ref[...]))

    zp = zp_ref[0]
    parts = [ys]
    for gi in range(POOL_G):
        zg = zp[:, gi * POOL_C:(gi + 1) * POOL_C]
        hi, lo = _split_bf16(zg)
        pm = pm_ref[gi]
        win = _dot(pm, hi) + _dot(pm, lo)
        pooled = win * pinv_ref[gi] - zg
        og = _dot(pooled.astype(BF16), wpool_ref[gi])
        parts.append(og * pscale_ref[:, gi * POOL_C:(gi + 1) * POOL_C])
    mix = _dot(jnp.concatenate(parts, axis=-1).astype(BF16), wout_ref[...])

    x1 = x + g1_ref[0, 0] * mix
    x1_ref[...] = x1
    ms = jnp.mean(x1 * x1, axis=-1, keepdims=True)
    h2 = x1 * lax.rsqrt(ms + EPS) * n2_ref[...]
    h2 = h2 * (1.0 + sc2_ref[0, 0]) + sh2_ref[0, 0]
    h2t_ref[...] = h2.T.astype(BF16)


def _mix(x, zs2d, zp, yf2d, yb2d, mods, ssm_d, w_glu, w_pool, pool_scale, w_out, norm2_g, per_batch, grid):
    B, L, _ = x.shape
    tm = _seq_tile(L, grid)
    nt = L // tm
    pm, pinv = _pool_tables(grid, tm)
    tm_spec = pl.BlockSpec((tm, D_SSM), lambda b, i: (i, b))
    const2 = lambda b, i: (0, 0)
    const3 = lambda b, i: (0, 0, 0)
    return pl.pallas_call(
        _mix_kernel,
        grid=(B, nt),
        in_specs=[
            pl.BlockSpec((1, tm, D_MODEL), lambda b, i: (b, i, 0)),
            tm_spec,
            pl.BlockSpec((1, tm, D_POOL), lambda b, i: (b, i, 0)),
            tm_spec,
            tm_spec,
            _mod_spec(2, per_batch),
            _mod_spec(3, per_batch),
            _mod_spec(4, per_batch),
            pl.BlockSpec((1, D_SSM), const2),
            pl.BlockSpec((D_SSM, D_SSM), const2),
            pl.BlockSpec((POOL_G, tm, tm), const3),
            pl.BlockSpec((POOL_G, tm, LANES), const3),
            pl.BlockSpec((POOL_G, POOL_C, POOL_C), const3),
            pl.BlockSpec((1, D_POOL), const2),
            pl.BlockSpec((D_MODEL, D_MODEL), const2),
            pl.BlockSpec((1, D_MODEL), const2),
        ],
        out_specs=[
            pl.BlockSpec((tm, D_MODEL), lambda b, i: (b * nt + i, 0)),
            pl.BlockSpec((D_MODEL, tm), lambda b, i: (0, b * nt + i)),
        ],
        out_shape=[
            jax.ShapeDtypeStruct((B * L, D_MODEL), F32),
            jax.ShapeDtypeStruct((D_MODEL, B * L), BF16),
        ],
        compiler_params=_params(("parallel", "parallel")),
        name="mix",
    )(x, zs2d, zp, yf2d, yb2d, mods, mods, mods, ssm_d.reshape(1, D_SSM), w_glu.astype(BF16),
      pm, pinv, w_pool.astype(BF16), pool_scale.reshape(1, D_POOL), w_out.astype(BF16),
      norm2_g.reshape(1, D_MODEL))


def _sort_pairs(n):
    pairs = []

    def merge(lo, hi, r):
        step = r * 2
        if step < hi - lo:
            merge(lo, hi, step)
            merge(lo + r, hi, step)
            for i in range(lo + r, hi - r, step):
                pairs.append((i, i + r))
        else:
            pairs.append((lo, lo + r))

    def sort(lo, hi):
        if hi - lo >= 1:
            mid = lo + (hi - lo) // 2
            sort(lo, mid)
            sort(mid + 1, hi)
            merge(lo, hi, 1)

    sort(0, n - 1)
    return pairs


_SORT16 = _sort_pairs(PEER_TOPK)


def _vmax(a, b):
    if a is None:
        return b
    if b is None:
        return a
    return jnp.maximum(a, b)


def _vmin(a, b):
    if a is None or b is None:
        return None
    return jnp.minimum(a, b)


def _sort16_desc(w):
    w = list(w)
    for i, j in _SORT16:
        w[i], w[j] = _vmax(w[i], w[j]), _vmin(w[i], w[j])
    return w


def _merge_top16(a, b):
    k = PEER_TOPK
    w = [_vmax(a[i], b[k - 1 - i]) for i in range(k)]
    d = k // 2
    while d >= 1:
        for i in range(k):
            if i & d == 0:
                w[i], w[i + d] = _vmax(w[i], w[i + d]), _vmin(w[i], w[i + d])
        d //= 2
    return w


def _top16_of_128(load):
    cur = None
    for grp in range(PEER_NKEYS // PEER_TOPK):
        s = _sort16_desc([load(grp * PEER_TOPK + i) for i in range(PEER_TOPK)])
        cur = s if cur is None else _merge_top16(cur, s)
    return cur


def _top16_pair_sums(sv0, sv1):
    k = PEER_TOPK
    pad = lambda row: row + [None] * (k - len(row))
    cur = [sv0[0] + sv1[j] for j in range(k)]
    for i in range(1, k // 2):
        cur = _merge_top16(cur, pad([sv0[i] + sv1[j] for j in range(k // (i + 1))]))
    return _merge_top16(cur, pad([sv0[i] + sv1[0] for i in range(k // 2, k)]))


def _count_sorted(v, above):
    sel = jnp.where
    g8 = above(v[7])
    g4 = above(sel(g8, v[11], v[3]))
    g2 = above(sel(g8, sel(g4, v[13], v[9]), sel(g4, v[5], v[1])))
    g1 = above(sel(g8, sel(g4, sel(g2, v[14], v[12]), sel(g2, v[10], v[8])),
                   sel(g4, sel(g2, v[6], v[4]), sel(g2, v[2], v[0]))))
    low = sel(g8, 8.0, 0.0) + sel(g4, 4.0, 0.0) + sel(g2, 2.0, 0.0) + sel(g1, 1.0, 0.0)
    return sel(above(v[15]), 16.0, low)


def _as_words(x):
    return pltpu.bitcast(x.astype(BF16), jnp.uint32)


def _as_bf16(words):
    return pltpu.bitcast(words, BF16)


def _peer_prep_kernel(ht_ref, wq_ref, wka_ref, kb_ref, a_ref, m_ref, b_ref, r_ref, q_ref, sa_ref, st_ref):
    tm = ht_ref.shape[1]
    half = PEER_HEADS * PEER_DHALF
    q_ref[...] = _dot(wq_ref[...], ht_ref[...]).astype(BF16)
    for side in range(2):
        s = _dot(wka_ref[side], q_ref[side * half:(side + 1) * half, :])
        sa_ref[side] = s.reshape(PEER_NKEYS, PEER_HEADS, tm)

    for lc in range(tm // LANES):
        ls = slice(lc * LANES, (lc + 1) * LANES)
        sv0 = _top16_of_128(lambda n: sa_ref[0, n, :, ls])
        sv1 = _top16_of_128(lambda n: sa_ref[1, n, :, ls])
        top = _top16_pair_sums(sv0, sv1)
        tau = top[PEER_TOPK - 1]
        best = sv0[0] + sv1[0]
        z = jnp.exp(top[0] - best)
        for k in range(1, PEER_TOPK):
            z = z + jnp.exp(top[k] - best)
        inv_z = 1.0 / z
        best0 = sv0[0]
        for k in range(PEER_TOPK):
            st_ref[k, :, ls] = sv1[k]

        def first_key(n, carry, lc=lc, ls=ls, sv1=sv1, tau=tau, best0=best0, inv_z=inv_z):
            s0 = sa_ref[0, n, :, ls]
            a_ref[lc, n] = jnp.exp(s0 - best0) * inv_z
            m_ref[lc, n] = _count_sorted(sv1, lambda p: s0 + p >= tau)
            return carry

        lax.fori_loop(0, PEER_NKEYS, first_key, 0, unroll=8)

    for h in range(PEER_HEADS):
        s1 = _dot(kb_ref[h], q_ref[half + h * PEER_DHALF:half + (h + 1) * PEER_DHALF, :])
        for lc in range(tm // LANES):
            ls = slice(lc * LANES, (lc + 1) * LANES)
            sorted1 = [st_ref[k, h:h + 1, ls] for k in range(PEER_TOPK)]
            for part in range(PEER_NKEYS // RANK_ROWS):
                x = s1[part * RANK_ROWS:(part + 1) * RANK_ROWS, ls]
                words = slice(part * RANK_ROWS // 2, (part + 1) * RANK_ROWS // 2)
                b_ref[lc, h, words, :] = _as_words(jnp.exp(x - sorted1[0]))
                r_ref[lc, h, words, :] = _as_words(_count_sorted(sorted1, lambda p: p > x))


def _peer_tables(peer_wq, sub_keys):
    wq = peer_wq.reshape(D_MODEL, PEER_HEADS, 2, PEER_DHALF)
    wq_t = jnp.transpose(wq, (2, 1, 3, 0)).reshape(2 * PEER_HEADS * PEER_DHALF, D_MODEL).astype(BF16)
    eye = jnp.eye(PEER_HEADS, dtype=F32)
    wka = jnp.einsum("hsnk,hg->snhgk", sub_keys, eye)
    wka = wka.reshape(2, PEER_NKEYS * PEER_HEADS, PEER_HEADS * PEER_DHALF).astype(BF16)
    kb = sub_keys[:, 1].astype(BF16)
    return wq_t, wka, kb


def _peer_prep(h2t, wq_t, wka, kb):
    T = h2t.shape[1]
    tm = PREP_TILE
    assert T % tm == 0
    half = PEER_HEADS * PEER_DHALF
    first = jax.ShapeDtypeStruct((T // LANES, PEER_NKEYS, PEER_HEADS, LANES), F32)
    second = jax.ShapeDtypeStruct((T // LANES, PEER_HEADS, PEER_NKEYS // 2, LANES), jnp.uint32)
    fspec = pl.BlockSpec((tm // LANES, PEER_NKEYS, PEER_HEADS, LANES), lambda i: (i, 0, 0, 0))
    sspec = pl.BlockSpec((tm // LANES, PEER_HEADS, PEER_NKEYS // 2, LANES), lambda i: (i, 0, 0, 0))
    return pl.pallas_call(
        _peer_prep_kernel,
        grid=(T // tm,),
        in_specs=[
            pl.BlockSpec((D_MODEL, tm), lambda i: (0, i)),
            pl.BlockSpec((2 * half, D_MODEL), lambda i: (0, 0)),
            pl.BlockSpec((2, PEER_NKEYS * PEER_HEADS, half), lambda i: (0, 0, 0)),
            pl.BlockSpec((PEER_HEADS, PEER_NKEYS, PEER_DHALF), lambda i: (0, 0, 0)),
        ],
        out_specs=[fspec, fspec, sspec, sspec],
        out_shape=[first, first, second, second],
        scratch_shapes=[
            pltpu.VMEM((2 * half, tm), BF16),
            pltpu.VMEM((2, PEER_NKEYS, PEER_HEADS, tm), F32),
            pltpu.VMEM((PEER_TOPK, PEER_HEADS, tm), F32),
        ],
        compiler_params=_params(("parallel",)),
        name="peer_prep",
    )(h2t, wq_t, wka, kb)


def _peer_dense_kernel(ht_ref, a_ref, m_ref, b_ref, r_ref, x1_ref, g2_ref, fg_ref, u_hbm, vt_hbm,
                       y_ref, ubuf, vbuf, usem, vsem, acc_ref, act0_ref, act1_ref, w0_ref, w1_ref):
    act_refs = (act0_ref, act1_ref)
    w_refs = (w0_ref, w1_ref)
    tm = ht_ref.shape[1]
    n_chunks = tm // LANES
    n_sub = PEER_N // PEER_SB
    n_pairs = n_sub // 2
    keys_per_sub = PEER_SB // PEER_NKEYS
    sub_words = PEER_SB // 2
    half_rows = D_MODEL // 2

    def u_copy(s):
        slot = s % U_SLOTS
        return pltpu.make_async_copy(u_hbm.at[pl.ds(s * PEER_SB, PEER_SB), :], ubuf.at[slot], usem.at[slot])

    def v_copy(pair):
        slot = pair % V_SLOTS
        return pltpu.make_async_copy(vt_hbm.at[pair], vbuf.at[slot], vsem.at[slot])

    def activations(s, aslot):
        act = _dot(ubuf[s % U_SLOTS], ht_ref[...])
        for tc in range(n_chunks):
            act_refs[aslot][tc] = act[:, tc * LANES:(tc + 1) * LANES]

    def gated(s, aslot, wslot, pos):
        for tc in range(n_chunks):
            a_rows = [a_ref[tc, s * keys_per_sub + k] for k in range(keys_per_sub)]
            m_rows = [m_ref[tc, s * keys_per_sub + k] for k in range(keys_per_sub)]
            for part in range(PEER_NKEYS // GATE_ROWS):
                pwords = slice(part * GATE_ROWS // 2, (part + 1) * GATE_ROWS // 2)
                gates = [None] * keys_per_sub
                for h in range(PEER_HEADS):
                    b = _as_bf16(b_ref[tc, h, pwords, :])
                    r = _as_bf16(r_ref[tc, h, pwords, :])
                    for k in range(keys_per_sub):
                        a_row = jnp.broadcast_to(a_rows[k][h:h + 1, :], (GATE_ROWS, LANES)).astype(BF16)
                        m_row = jnp.broadcast_to(m_rows[k][h:h + 1, :], (GATE_ROWS, LANES)).astype(BF16)
                        term = jnp.where(r < m_row, b, jnp.zeros_like(b)) * a_row
                        gates[k] = term if gates[k] is None else gates[k] + term
                for k in range(keys_per_sub):
                    row0 = k * PEER_NKEYS + part * GATE_ROWS
                    g = _gelu(act_refs[aslot][tc, row0:row0 + GATE_ROWS, :].astype(BF16))
                    word0 = pos * sub_words + row0 // 2
                    w_refs[wslot][tc, word0:word0 + GATE_ROWS // 2, :] = pltpu.bitcast(g * gates[k], jnp.uint32)

    def accumulate(pair, wslot, half):
        w = _as_bf16(jnp.concatenate([w_refs[wslot][tc] for tc in range(n_chunks)], axis=1))
        rows = slice(half * half_rows, (half + 1) * half_rows)
        acc_ref[rows, :] += _dot(vbuf[pair % V_SLOTS, rows, :], w)

    def stage_dma(s, jj):
        static = isinstance(s, int)

        def when(cond, fn):
            if static:
                if cond:
                    fn()
            else:
                pl.when(cond)(fn)

        when(s + U_AHEAD < n_sub, lambda: u_copy(s + U_AHEAD).start())
        if jj % 2 == 0:
            when((s + 1) // 2 < n_pairs, lambda: v_copy((s + 1) // 2).start())
        when(s < n_sub, lambda: u_copy(s).wait())
        if jj % 2 == 0:
            when(s >= 3, lambda: v_copy((s - 3) // 2).wait())

    def stage_compute(s, jj, with_activations=True, with_gate=True):
        if with_activations:
            activations(s, (jj + 1) % 2)
        if with_gate:
            gated(s - 1, jj % 2, (jj // 2) % 2, jj % 2)
        accumulate((s - 3) // 2, (jj // 2 + 1) % 2, jj % 2)

    acc_ref[...] = jnp.zeros_like(acc_ref)
    w_refs[1][...] = jnp.zeros_like(w_refs[1])

    @pl.when(pl.program_id(0) == 0)
    def _():
        vbuf[(-1) % V_SLOTS] = jnp.zeros(vbuf.shape[1:], vbuf.dtype)

    for s in range(U_AHEAD + 1):
        u_copy(s).start()
    v_copy(0).start()
    u_copy(0).wait()
    activations(0, 0)

    def stages(t, carry):
        for jj in range(STAGE_UNROLL):
            s = 1 + t * STAGE_UNROLL + jj
            stage_dma(s, jj)
            stage_compute(s, jj)
        return carry

    lax.fori_loop(0, n_sub // STAGE_UNROLL, stages, 0)
    for s in (n_sub + 1, n_sub + 2):
        jj = (s - 1) % STAGE_UNROLL
        stage_dma(s, jj)
        stage_compute(s, jj, with_activations=False, with_gate=False)

    out = x1_ref[...] + g2_ref[0, 0] * acc_ref[...].T
    ms = jnp.mean(out * out, axis=-1, keepdims=True)
    y_ref[...] = out * lax.rsqrt(ms + EPS) * fg_ref[...]


def _peer_dense(h2t, u_bf, vt_sub, a, m, b, r, x1, mods, final_g, tokens_per_row):
    T = h2t.shape[1]
    tm = PEER_TM
    assert T % tm == 0
    if tokens_per_row is None:
        g2_map = lambda i: (CTX_ROW, 5, 0, 0)
    else:
        assert tokens_per_row % tm == 0, "a token tile must not straddle two requests"
        g2_map = lambda i: ((i * tm) // tokens_per_row, 5, 0, 0)
    fspec = pl.BlockSpec((tm // LANES, PEER_NKEYS, PEER_HEADS, LANES), lambda i: (i, 0, 0, 0))
    sspec = pl.BlockSpec((tm // LANES, PEER_HEADS, PEER_NKEYS // 2, LANES), lambda i: (i, 0, 0, 0))
    return pl.pallas_call(
        _peer_dense_kernel,
        grid=(T // tm,),
        in_specs=[
            pl.BlockSpec((D_MODEL, tm), lambda i: (0, i)),
            fspec, fspec, sspec, sspec,
            pl.BlockSpec((tm, D_MODEL), lambda i: (i, 0)),
            pl.BlockSpec((1, 1, 1, D_MODEL), g2_map),
            pl.BlockSpec((1, D_MODEL), lambda i: (0, 0)),
            pl.BlockSpec(memory_space=pl.ANY),
            pl.BlockSpec(memory_space=pl.ANY),
        ],
        out_specs=pl.BlockSpec((tm, D_MODEL), lambda i: (i, 0)),
        out_shape=jax.ShapeDtypeStruct((T, D_MODEL), F32),
        scratch_shapes=[
            pltpu.VMEM((U_SLOTS, PEER_SB, D_MODEL), BF16),
            pltpu.VMEM((V_SLOTS, D_MODEL, 2 * PEER_SB), BF16),
            pltpu.SemaphoreType.DMA((U_SLOTS,)),
            pltpu.SemaphoreType.DMA((V_SLOTS,)),
            pltpu.VMEM((D_MODEL, tm), F32),
            pltpu.VMEM((tm // LANES, PEER_SB, LANES), F32),
            pltpu.VMEM((tm // LANES, PEER_SB, LANES), F32),
            pltpu.VMEM((tm // LANES, PEER_SB, LANES), jnp.uint32),
            pltpu.VMEM((tm // LANES, PEER_SB, LANES), jnp.uint32),
        ],
        compiler_params=_params(("arbitrary",)),
        name="peer_dense",
    )(h2t, a, m, b, r, x1, mods, final_g.reshape(1, D_MODEL), u_bf, vt_sub)


def _trunk(x, mods, h0r, h0i, s5w, p, tables, final_g, per_batch, grid):
    B, L, _ = x.shape
    a_re, a_im, wb, wc_re, wc_im = s5w
    zs2d, zp = _inproj(x, mods, p["norm1_g"], p["w_in"], per_batch)
    zs = zs2d.reshape(L, B, D_SSM)
    yf, sfr, sfi = _s5_direction(zs, a_re, a_im, wb, wc_re, wc_im, h0r[:, 0], h0i[:, 0], 0)
    yb, sbr, sbi = _s5_direction(zs, a_re, a_im, wb, wc_re, wc_im, h0r[:, 1], h0i[:, 1], 1)
    x1, h2t = _mix(x, zs2d, zp, yf.reshape(L, B * D_SSM), yb.reshape(L, B * D_SSM), mods, p["ssm_d"],
                   p["w_glu"], p["w_pool"], p["pool_scale"], p["w_out"], p["norm2_g"], per_batch, grid)
    wq_t, wka, kb, u_bf, vt_bf = tables
    a, m, b, r = _peer_prep(h2t, wq_t, wka, kb)
    y = _peer_dense(h2t, u_bf, vt_bf, a, m, b, r, x1, mods, final_g, L if per_batch else None)
    new_re = jnp.stack([sfr, sbr], axis=1).reshape(B, 2, SSM_G, SSM_P)
    new_im = jnp.stack([sfi, sbi], axis=1).reshape(B, 2, SSM_G, SSM_P)
    return y.reshape(B, L, D_MODEL), new_re, new_im


def kernel(x_prompt, x_sample, state_ssm_re, state_ssm_im, c, c_ctx, norm1_g, w_mod, b_mod, w_in,
           ssm_lambda_re, ssm_lambda_im, ssm_log_dt, ssm_b_re, ssm_b_im, ssm_c_re, ssm_c_im, ssm_d,
           w_glu, w_pool, pool_scale, w_out, norm2_g, peer_wq, peer_subkeys, peer_u, peer_v, final_g):
    depth = w_mod.shape[0]
    assert depth == 1, "single trunk layer"
    l = 0
    n_dec = c.shape[0]
    cond = jnp.zeros((COND_ROWS, D_MODEL), F32).at[:n_dec].set(c).at[CTX_ROW].set(c_ctx)
    mods = _mod_vectors(cond, w_mod[l], b_mod[l]).reshape(COND_ROWS, N_MOD, 1, D_MODEL)

    ar, ai, bbr, bbi = _discretise(ssm_lambda_re[l], ssm_lambda_im[l], ssm_log_dt[l], ssm_b_re[l], ssm_b_im[l])
    s5w = _s5_weights(ar, ai, bbr, bbi, ssm_c_re[l], ssm_c_im[l])

    wq_t, wka, kb = _peer_tables(peer_wq[l], peer_subkeys[l])
    vt_sub = jnp.transpose(peer_v[l].reshape(PEER_N // (2 * PEER_SB), 2 * PEER_SB, D_MODEL), (0, 2, 1)).astype(BF16)
    tables = (wq_t, wka, kb, peer_u[l].astype(BF16), vt_sub)

    p = {"norm1_g": norm1_g[l], "w_in": w_in[l].astype(BF16), "ssm_d": ssm_d[l], "w_glu": w_glu[l],
         "w_pool": w_pool[l], "pool_scale": pool_scale[l], "w_out": w_out[l], "norm2_g": norm2_g[l]}

    bp = x_prompt.shape[0]
    zeros = jnp.zeros((bp, 2, SSM_N), F32)
    y_prompt, new_re, new_im = _trunk(x_prompt, mods, zeros, zeros, s5w, p, tables, final_g, False, False)
    h0r = state_ssm_re[:, l].reshape(n_dec, 2, SSM_N)
    h0i = state_ssm_im[:, l].reshape(n_dec, 2, SSM_N)
    y_sample, _, _ = _trunk(x_sample, mods, h0r, h0i, s5w, p, tables, final_g, True, True)
    return (y_prompt, y_sample, new_re[:, None], new_im[:, None])
```

```python
import functools

import numpy as np
import jax
import jax.numpy as jnp
from jax import lax
from jax.experimental import pallas as pl
from jax.experimental.pallas import tpu as pltpu

F32 = jnp.float32
BF16 = jnp.bfloat16

D_MODEL = 1024
D_SSM = 512
D_POOL = 512
SSM_H = 16
SSM_G = 32
SSM_P = 64
SSM_N = SSM_G * SSM_P
POOL_WINDOWS = (2, 4, 8, 16)
POOL_G = 4
POOL_C = 128
GRID_W = 64
PEER_HEADS = 8
PEER_NKEYS = 128
PEER_N = PEER_NKEYS * PEER_NKEYS
PEER_DHALF = 128
PEER_TOPK = 16
N_MOD = 6
EPS = 1e-6

SUBLANES = 8
LANES = 128
VMEM_LIMIT = 56 * 1024 * 1024

COND_ROWS = 16
CTX_ROW = 8
SEQ_TILE = 256
INPROJ_TILE = 512
S5_CHUNK = 128
S5_LANES = 1024
PREP_TILE = 512
PEER_TM = 512
PEER_SB = 512
STAGE_UNROLL = 4
U_SLOTS = 6
U_AHEAD = 4
V_SLOTS = 3
V_DMA_PRIORITY = 1
RANK_ROWS = 64
GATE_ROWS = 64


def _gelu(x):
    u = x * (0.7978845608028654 + 0.035677408136300125 * (x * x))
    return 0.5 * x * (1.0 + jnp.tanh(u))


def _split_bf16(x):
    hi = x.astype(BF16)
    lo = (x - hi.astype(F32)).astype(BF16)
    return hi, lo


def _dot(a, b):
    return jnp.dot(a, b, preferred_element_type=F32)


def _params(sem):
    return pltpu.CompilerParams(dimension_semantics=sem, vmem_limit_bytes=VMEM_LIMIT)


def _mod_kernel(cond_ref, w_ref, b_ref, o_ref):
    c = cond_ref[...]
    s = c * jax.nn.sigmoid(c)
    shi, slo = _split_bf16(s)
    whi, wlo = _split_bf16(w_ref[...])
    o_ref[...] = _dot(shi, whi) + _dot(slo, whi) + _dot(shi, wlo) + b_ref[...]


def _mod_vectors(cond, w_mod, b_mod):
    n = w_mod.shape[1]
    bn = D_MODEL
    return pl.pallas_call(
        _mod_kernel,
        grid=(n // bn,),
        in_specs=[
            pl.BlockSpec((COND_ROWS, D_MODEL), lambda k: (0, 0)),
            pl.BlockSpec((D_MODEL, bn), lambda k: (0, k)),
            pl.BlockSpec((1, bn), lambda k: (0, k)),
        ],
        out_specs=pl.BlockSpec((COND_ROWS, bn), lambda k: (0, k)),
        out_shape=jax.ShapeDtypeStruct((COND_ROWS, n), F32),
        compiler_params=_params(("arbitrary",)),
        name="mod",
    )(cond, w_mod, b_mod.reshape(1, n))


def _disc_kernel(lr_ref, li_ref, ldt_ref, br_ref, bi_ref, ar_ref, ai_ref, bbr_ref, bbi_ref):
    lr = lr_ref[...]
    li = li_ref[...]
    dt = jnp.exp(ldt_ref[...])
    mag = jnp.exp(lr * dt)
    ar = mag * jnp.cos(li * dt)
    ai = mag * jnp.sin(li * dt)
    den = lr * lr + li * li
    nr = ar - 1.0
    ni = ai
    fr = (nr * lr + ni * li) / den
    fi = (ni * lr - nr * li) / den
    ar_ref[...] = ar
    ai_ref[...] = ai
    br = br_ref[...]
    bi = bi_ref[...]
    frb = fr[:, None, :]
    fib = fi[:, None, :]
    bbr_ref[...] = frb * br - fib * bi
    bbi_ref[...] = frb * bi + fib * br


def _discretise(lam_re, lam_im, log_dt, b_re, b_im):
    rows = 2 * SSM_G
    lr = lam_re.reshape(rows, SSM_P)
    li = lam_im.reshape(rows, SSM_P)
    ldt = jnp.broadcast_to(log_dt.reshape(rows, 1), (rows, SSM_P))
    br = jnp.swapaxes(b_re, -1, -2).reshape(rows, SSM_H, SSM_P)
    bi = jnp.swapaxes(b_im, -1, -2).reshape(rows, SSM_H, SSM_P)
    small = jax.ShapeDtypeStruct((rows, SSM_P), F32)
    big = jax.ShapeDtypeStruct((rows, SSM_H, SSM_P), F32)
    return pl.pallas_call(_disc_kernel, out_shape=(small, small, big, big), name="disc")(lr, li, ldt, br, bi)


def _s5_weights(ar, ai, bbr, bbi, c_re, c_im):
    half_g = SSM_G // 2
    eye = jnp.eye(half_g, dtype=F32)
    a_re = ar.reshape(2, 1, SSM_N)
    a_im = ai.reshape(2, 1, SSM_N)

    def bmat(t):
        t = t.reshape(2, 2, half_g, SSM_H, SSM_P)
        m = jnp.einsum("djghp,gk->djghkp", t, eye)
        return m.reshape(2, 2, half_g * SSM_H, half_g * SSM_P)

    def cmat(t):
        t = t.reshape(2, 2, half_g, SSM_H, SSM_P)
        m = jnp.einsum("djghp,gk->djgpkh", t, eye)
        return m.reshape(2, 2, half_g * SSM_P, half_g * SSM_H)

    wb = jnp.concatenate([bmat(bbr), bmat(bbi)], axis=-1).astype(BF16)
    wc_re = cmat(c_re).astype(BF16)
    wc_im = cmat(c_im).astype(BF16)
    return a_re, a_im, wb, wc_re, wc_im


def _inproj_kernel(x_ref, sh_ref, sc_ref, g_ref, w_ref, zs_ref, zp_ref):
    x = x_ref[0]
    ms = jnp.mean(x * x, axis=-1, keepdims=True)
    y = x * lax.rsqrt(ms + EPS) * g_ref[...]
    h = y * (1.0 + sc_ref[0, 0]) + sh_ref[0, 0]
    z = _dot(h.astype(BF16), w_ref[...])
    zs_ref[...] = z[:, :D_SSM]
    zp_ref[0] = z[:, D_SSM:]


def _mod_spec(k, per_batch):
    if per_batch:
        return pl.BlockSpec((1, 1, 1, D_MODEL), lambda b, i: (b, k, 0, 0))
    return pl.BlockSpec((1, 1, 1, D_MODEL), lambda b, i: (CTX_ROW, k, 0, 0))


def _seq_tile(L, grid):
    if grid:
        tm = min(L, SEQ_TILE)
        assert L % tm == 0 and tm % GRID_W == 0
        return tm
    assert L <= SEQ_TILE, "sequence pooling needs the whole sequence in one tile"
    return L


def _inproj(x, mods, g, w_in, per_batch):
    B, L, _ = x.shape
    tm = min(L, INPROJ_TILE)
    assert L % tm == 0
    return pl.pallas_call(
        _inproj_kernel,
        grid=(B, L // tm),
        in_specs=[
            pl.BlockSpec((1, tm, D_MODEL), lambda b, i: (b, i, 0)),
            _mod_spec(0, per_batch),
            _mod_spec(1, per_batch),
            pl.BlockSpec((1, D_MODEL), lambda b, i: (0, 0)),
            pl.BlockSpec((D_MODEL, D_MODEL), lambda b, i: (0, 0)),
        ],
        out_specs=[
            pl.BlockSpec((tm, D_SSM), lambda b, i: (i, b)),
            pl.BlockSpec((1, tm, D_POOL), lambda b, i: (b, i, 0)),
        ],
        out_shape=[
            jax.ShapeDtypeStruct((L, B * D_SSM), F32),
            jax.ShapeDtypeStruct((B, L, D_POOL), F32),
        ],
        compiler_params=_params(("parallel", "parallel")),
        name="inproj",
    )(x, mods, mods, g.reshape(1, D_MODEL), w_in)


def _s5_kernel(u_ref, are_ref, aim_ref, wb_ref, wcr_ref, wci_ref, h0r_ref, h0i_ref,
               y_ref, sr_ref, si_ref, bur_ref, bui_ref, cr_ref, ci_ref, *, reverse, n_chunks):
    c = pl.program_id(1)
    rows = S5_CHUNK * SUBLANES
    half = SSM_N // 2
    half_in = D_SSM // 2

    @pl.when(c == 0)
    def _():
        cr_ref[...] = h0r_ref[...]
        ci_ref[...] = h0i_ref[...]

    u = u_ref[...].reshape(rows, D_SSM).astype(BF16)
    for j in range(2):
        r = _dot(u[:, j * half_in:(j + 1) * half_in], wb_ref[0, j])
        bur_ref[:, j * half:(j + 1) * half] = r[:, :half]
        bui_ref[:, j * half:(j + 1) * half] = r[:, half:]

    for lc in range(SSM_N // S5_LANES):
        ls = slice(lc * S5_LANES, (lc + 1) * S5_LANES)
        ar = jnp.broadcast_to(are_ref[0, :, ls], (SUBLANES, S5_LANES))
        ai = jnp.broadcast_to(aim_ref[0, :, ls], (SUBLANES, S5_LANES))

        def step(i, carry, ls=ls, ar=ar, ai=ai):
            hr, hi = carry
            t = (S5_CHUNK - 1 - i) if reverse else i
            row = pl.multiple_of(t * SUBLANES, SUBLANES)
            nhr = ar * hr - ai * hi + bur_ref[pl.ds(row, SUBLANES), ls]
            nhi = ar * hi + ai * hr + bui_ref[pl.ds(row, SUBLANES), ls]
            bur_ref[pl.ds(row, SUBLANES), ls] = nhr
            bui_ref[pl.ds(row, SUBLANES), ls] = nhi
            return nhr, nhi

        hr, hi = lax.fori_loop(0, S5_CHUNK, step, (cr_ref[:, ls], ci_ref[:, ls]), unroll=4)
        cr_ref[:, ls] = hr
        ci_ref[:, ls] = hi

    ys = []
    for j in range(2):
        hs = slice(j * half, (j + 1) * half)
        yr = _dot(bur_ref[:, hs].astype(BF16), wcr_ref[0, j])
        yi = _dot(bui_ref[:, hs].astype(BF16), wci_ref[0, j])
        ys.append(yr - yi)
    y_ref[...] = jnp.concatenate(ys, axis=-1).reshape(S5_CHUNK, SUBLANES, D_SSM)

    @pl.when(c == n_chunks - 1)
    def _():
        sr_ref[...] = cr_ref[...]
        si_ref[...] = ci_ref[...]


def _s5_direction(zs, a_re, a_im, wb, wc_re, wc_im, h0r, h0i, d):
    L, B, _ = zs.shape
    assert L % S5_CHUNK == 0 and B % SUBLANES == 0
    n_chunks = L // S5_CHUNK
    reverse = d == 1
    cidx = (lambda g, c: n_chunks - 1 - c) if reverse else (lambda g, c: c)
    rows = S5_CHUNK * SUBLANES
    kern = functools.partial(_s5_kernel, reverse=reverse, n_chunks=n_chunks)
    state = jax.ShapeDtypeStruct((B, SSM_N), F32)
    return pl.pallas_call(
        kern,
        grid=(B // SUBLANES, n_chunks),
        in_specs=[
            pl.BlockSpec((S5_CHUNK, SUBLANES, D_SSM), lambda g, c: (cidx(g, c), g, 0)),
            pl.BlockSpec((1, 1, SSM_N), lambda g, c: (d, 0, 0)),
            pl.BlockSpec((1, 1, SSM_N), lambda g, c: (d, 0, 0)),
            pl.BlockSpec((1, 2, D_SSM // 2, SSM_N), lambda g, c: (d, 0, 0, 0)),
            pl.BlockSpec((1, 2, SSM_N // 2, D_SSM // 2), lambda g, c: (d, 0, 0, 0)),
            pl.BlockSpec((1, 2, SSM_N // 2, D_SSM // 2), lambda g, c: (d, 0, 0, 0)),
            pl.BlockSpec((SUBLANES, SSM_N), lambda g, c: (g, 0)),
            pl.BlockSpec((SUBLANES, SSM_N), lambda g, c: (g, 0)),
        ],
        out_specs=[
            pl.BlockSpec((S5_CHUNK, SUBLANES, D_SSM), lambda g, c: (cidx(g, c), g, 0)),
            pl.BlockSpec((SUBLANES, SSM_N), lambda g, c: (g, 0)),
            pl.BlockSpec((SUBLANES, SSM_N), lambda g, c: (g, 0)),
        ],
        out_shape=[jax.ShapeDtypeStruct((L, B, D_SSM), F32), state, state],
        scratch_shapes=[
            pltpu.VMEM((rows, SSM_N), F32),
            pltpu.VMEM((rows, SSM_N), F32),
            pltpu.VMEM((SUBLANES, SSM_N), F32),
            pltpu.VMEM((SUBLANES, SSM_N), F32),
        ],
        compiler_params=_params(("parallel", "arbitrary")),
        name="s5_bwd" if reverse else "s5_fwd",
    )(zs, a_re, a_im, wb, wc_re, wc_im, h0r, h0i)


def _pool_tables(grid, tm):
    seg = GRID_W if grid else tm
    t = np.arange(tm)
    pos = t % seg
    base = t - pos
    mats, invs = [], []
    for w in POOL_WINDOWS:
        lo = np.clip(pos - w // 2, 0, seg)
        hi = np.clip(pos + (w - w // 2), 0, seg)
        s = t[None, :]
        m = (s >= (base + lo)[:, None]) & (s < (base + hi)[:, None])
        mats.append(m.astype(np.float32))
        invs.append(np.broadcast_to((1.0 / (hi - lo).astype(np.float32))[:, None], (tm, LANES)))
    return jnp.asarray(np.stack(mats), dtype=BF16), jnp.asarray(np.stack(invs), dtype=F32)


def _mix_kernel(x_ref, u_ref, zp_ref, yf_ref, yb_ref, g1_ref, sh2_ref, sc2_ref, d_ref, wglu_ref,
                pm_ref, pinv_ref, wpool_ref, pscale_ref, wout_ref, n2_ref, x1_ref, h2t_ref):
    x = x_ref[0]
    y = d_ref[...] * u_ref[...] + yf_ref[...] + yb_ref[...]
    g = _gelu(y)
    ys = g * jax.nn.sigmoid(_dot(g.astype(BF16), wglu_ref[...]))

    zp = zp_ref[0]
    parts = [ys]
    for gi in range(POOL_G):
        zg = zp[:, gi * POOL_C:(gi + 1) * POOL_C]
        hi, lo = _split_bf16(zg)
        pm = pm_ref[gi]
        win = _dot(pm, hi) + _dot(pm, lo)
        pooled = win * pinv_ref[gi] - zg
        og = _dot(pooled.astype(BF16), wpool_ref[gi])
        parts.append(og * pscale_ref[:, gi * POOL_C:(gi + 1) * POOL_C])
    mix = _dot(jnp.concatenate(parts, axis=-1).astype(BF16), wout_ref[...])

    x1 = x + g1_ref[0, 0] * mix
    x1_ref[...] = x1
    ms = jnp.mean(x1 * x1, axis=-1, keepdims=True)
    h2 = x1 * lax.rsqrt(ms + EPS) * n2_ref[...]
    h2 = h2 * (1.0 + sc2_ref[0, 0]) + sh2_ref[0, 0]
    h2t_ref[...] = h2.T.astype(BF16)


def _mix(x, zs2d, zp, yf2d, yb2d, mods, ssm_d, w_glu, w_pool, pool_scale, w_out, norm2_g, per_batch, grid):
    B, L, _ = x.shape
    tm = _seq_tile(L, grid)
    nt = L // tm
    pm, pinv = _pool_tables(grid, tm)
    tm_spec = pl.BlockSpec((tm, D_SSM), lambda b, i: (i, b))
    const2 = lambda b, i: (0, 0)
    const3 = lambda b, i: (0, 0, 0)
    return pl.pallas_call(
        _mix_kernel,
        grid=(B, nt),
        in_specs=[
            pl.BlockSpec((1, tm, D_MODEL), lambda b, i: (b, i, 0)),
            tm_spec,
            pl.BlockSpec((1, tm, D_POOL), lambda b, i: (b, i, 0)),
            tm_spec,
            tm_spec,
            _mod_spec(2, per_batch),
            _mod_spec(3, per_batch),
            _mod_spec(4, per_batch),
            pl.BlockSpec((1, D_SSM), const2),
            pl.BlockSpec((D_SSM, D_SSM), const2),
            pl.BlockSpec((POOL_G, tm, tm), const3),
            pl.BlockSpec((POOL_G, tm, LANES), const3),
            pl.BlockSpec((POOL_G, POOL_C, POOL_C), const3),
            pl.BlockSpec((1, D_POOL), const2),
            pl.BlockSpec((D_MODEL, D_MODEL), const2),
            pl.BlockSpec((1, D_MODEL), const2),
        ],
        out_specs=[
            pl.BlockSpec((tm, D_MODEL), lambda b, i: (b * nt + i, 0)),
            pl.BlockSpec((D_MODEL, tm), lambda b, i: (0, b * nt + i)),
        ],
        out_shape=[
            jax.ShapeDtypeStruct((B * L, D_MODEL), F32),
            jax.ShapeDtypeStruct((D_MODEL, B * L), BF16),
        ],
        compiler_params=_params(("parallel", "parallel")),
        name="mix",
    )(x, zs2d, zp, yf2d, yb2d, mods, mods, mods, ssm_d.reshape(1, D_SSM), w_glu.astype(BF16),
      pm, pinv, w_pool.astype(BF16), pool_scale.reshape(1, D_POOL), w_out.astype(BF16),
      norm2_g.reshape(1, D_MODEL))


def _sort_pairs(n):
    pairs = []

    def merge(lo, hi, r):
        step = r * 2
        if step < hi - lo:
            merge(lo, hi, step)
            merge(lo + r, hi, step)
            for i in range(lo + r, hi - r, step):
                pairs.append((i, i + r))
        else:
            pairs.append((lo, lo + r))

    def sort(lo, hi):
        if hi - lo >= 1:
            mid = lo + (hi - lo) // 2
            sort(lo, mid)
            sort(mid + 1, hi)
            merge(lo, hi, 1)

    sort(0, n - 1)
    return pairs


_SORT16 = _sort_pairs(PEER_TOPK)


def _vmax(a, b):
    if a is None:
        return b
    if b is None:
        return a
    return jnp.maximum(a, b)


def _vmin(a, b):
    if a is None or b is None:
        return None
    return jnp.minimum(a, b)


def _sort16_desc(w):
    w = list(w)
    for i, j in _SORT16:
        w[i], w[j] = _vmax(w[i], w[j]), _vmin(w[i], w[j])
    return w


def _merge_top16(a, b):
    k = PEER_TOPK
    w = [_vmax(a[i], b[k - 1 - i]) for i in range(k)]
    d = k // 2
    while d >= 1:
        for i in range(k):
            if i & d == 0:
                w[i], w[i + d] = _vmax(w[i], w[i + d]), _vmin(w[i], w[i + d])
        d //= 2
    return w


def _top16_of_128(load):
    cur = None
    for grp in range(PEER_NKEYS // PEER_TOPK):
        s = _sort16_desc([load(grp * PEER_TOPK + i) for i in range(PEER_TOPK)])
        cur = s if cur is None else _merge_top16(cur, s)
    return cur


def _top16_pair_sums(sv0, sv1):
    k = PEER_TOPK
    pad = lambda row: row + [None] * (k - len(row))
    cur = [sv0[0] + sv1[j] for j in range(k)]
    for i in range(1, k // 2):
        cur = _merge_top16(cur, pad([sv0[i] + sv1[j] for j in range(k // (i + 1))]))
    return _merge_top16(cur, pad([sv0[i] + sv1[0] for i in range(k // 2, k)]))


def _count_sorted(v, above):
    sel = jnp.where
    g8 = above(v[7])
    g4 = above(sel(g8, v[11], v[3]))
    g2 = above(sel(g8, sel(g4, v[13], v[9]), sel(g4, v[5], v[1])))
    g1 = above(sel(g8, sel(g4, sel(g2, v[14], v[12]), sel(g2, v[10], v[8])),
                   sel(g4, sel(g2, v[6], v[4]), sel(g2, v[2], v[0]))))
    low = sel(g8, 8.0, 0.0) + sel(g4, 4.0, 0.0) + sel(g2, 2.0, 0.0) + sel(g1, 1.0, 0.0)
    return sel(above(v[15]), 16.0, low)


def _as_words(x):
    return pltpu.bitcast(x.astype(BF16), jnp.uint32)


def _as_bf16(words):
    return pltpu.bitcast(words, BF16)


def _peer_prep_kernel(ht_ref, wq_ref, wka_ref, kb_ref, a_ref, m_ref, b_ref, r_ref, q_ref, sa_ref, st_ref):
    tm = ht_ref.shape[1]
    half = PEER_HEADS * PEER_DHALF
    q_ref[...] = _dot(wq_ref[...], ht_ref[...]).astype(BF16)
    for side in range(2):
        s = _dot(wka_ref[side], q_ref[side * half:(side + 1) * half, :])
        sa_ref[side] = s.reshape(PEER_NKEYS, PEER_HEADS, tm)

    for lc in range(tm // LANES):
        ls = slice(lc * LANES, (lc + 1) * LANES)
        sv0 = _top16_of_128(lambda n: sa_ref[0, n, :, ls])
        sv1 = _top16_of_128(lambda n: sa_ref[1, n, :, ls])
        top = _top16_pair_sums(sv0, sv1)
        tau = top[PEER_TOPK - 1]
        best = sv0[0] + sv1[0]
        z = jnp.exp(top[0] - best)
        for k in range(1, PEER_TOPK):
            z = z + jnp.exp(top[k] - best)
        inv_z = 1.0 / z
        best0 = sv0[0]
        for k in range(PEER_TOPK):
            st_ref[k, :, ls] = sv1[k]

        def first_key(n, carry, lc=lc, ls=ls, sv1=sv1, tau=tau, best0=best0, inv_z=inv_z):
            s0 = sa_ref[0, n, :, ls]
            a_ref[lc, n] = jnp.exp(s0 - best0) * inv_z
            m_ref[lc, n] = _count_sorted(sv1, lambda p: s0 + p >= tau)
            return carry

        lax.fori_loop(0, PEER_NKEYS, first_key, 0, unroll=8)

    for h in range(PEER_HEADS):
        s1 = _dot(kb_ref[h], q_ref[half + h * PEER_DHALF:half + (h + 1) * PEER_DHALF, :])
        for lc in range(tm // LANES):
            ls = slice(lc * LANES, (lc + 1) * LANES)
            sorted1 = [st_ref[k, h:h + 1, ls] for k in range(PEER_TOPK)]
            for part in range(PEER_NKEYS // RANK_ROWS):
                x = s1[part * RANK_ROWS:(part + 1) * RANK_ROWS, ls]
                words = slice(part * RANK_ROWS // 2, (part + 1) * RANK_ROWS // 2)
                b_ref[lc, h, words, :] = _as_words(jnp.exp(x - sorted1[0]))
                r_ref[lc, h, words, :] = _as_words(_count_sorted(sorted1, lambda p: p > x))


def _peer_tables(peer_wq, sub_keys):
    wq = peer_wq.reshape(D_MODEL, PEER_HEADS, 2, PEER_DHALF)
    wq_t = jnp.transpose(wq, (2, 1, 3, 0)).reshape(2 * PEER_HEADS * PEER_DHALF, D_MODEL).astype(BF16)
    eye = jnp.eye(PEER_HEADS, dtype=F32)
    wka = jnp.einsum("hsnk,hg->snhgk", sub_keys, eye)
    wka = wka.reshape(2, PEER_NKEYS * PEER_HEADS, PEER_HEADS * PEER_DHALF).astype(BF16)
    kb = sub_keys[:, 1].astype(BF16)
    return wq_t, wka, kb


def _peer_prep(h2t, wq_t, wka, kb):
    T = h2t.shape[1]
    tm = PREP_TILE
    assert T % tm == 0
    half = PEER_HEADS * PEER_DHALF
    first = jax.ShapeDtypeStruct((T // LANES, PEER_NKEYS, PEER_HEADS, LANES), F32)
    second = jax.ShapeDtypeStruct((T // LANES, PEER_HEADS, PEER_NKEYS // 2, LANES), jnp.uint32)
    fspec = pl.BlockSpec((tm // LANES, PEER_NKEYS, PEER_HEADS, LANES), lambda i: (i, 0, 0, 0))
    sspec = pl.BlockSpec((tm // LANES, PEER_HEADS, PEER_NKEYS // 2, LANES), lambda i: (i, 0, 0, 0))
    return pl.pallas_call(
        _peer_prep_kernel,
        grid=(T // tm,),
        in_specs=[
            pl.BlockSpec((D_MODEL, tm), lambda i: (0, i)),
            pl.BlockSpec((2 * half, D_MODEL), lambda i: (0, 0)),
            pl.BlockSpec((2, PEER_NKEYS * PEER_HEADS, half), lambda i: (0, 0, 0)),
            pl.BlockSpec((PEER_HEADS, PEER_NKEYS, PEER_DHALF), lambda i: (0, 0, 0)),
        ],
        out_specs=[fspec, fspec, sspec, sspec],
        out_shape=[first, first, second, second],
        scratch_shapes=[
            pltpu.VMEM((2 * half, tm), BF16),
            pltpu.VMEM((2, PEER_NKEYS, PEER_HEADS, tm), F32),
            pltpu.VMEM((PEER_TOPK, PEER_HEADS, tm), F32),
        ],
        compiler_params=_params(("parallel",)),
        name="peer_prep",
    )(h2t, wq_t, wka, kb)


def _peer_dense_kernel(ht_ref, a_ref, m_ref, b_ref, r_ref, x1_ref, g2_ref, fg_ref, u_hbm, vt_hbm,
                       y_ref, ubuf, vbuf, usem, vsem, acc_ref, act0_ref, act1_ref, w0_ref, w1_ref):
    act_refs = (act0_ref, act1_ref)
    w_refs = (w0_ref, w1_ref)
    tm = ht_ref.shape[1]
    n_chunks = tm // LANES
    n_sub = PEER_N // PEER_SB
    n_pairs = n_sub // 2
    keys_per_sub = PEER_SB // PEER_NKEYS
    sub_words = PEER_SB // 2
    half_rows = D_MODEL // 2

    def u_copy(s):
        slot = s % U_SLOTS
        return pltpu.make_async_copy(u_hbm.at[pl.ds(s * PEER_SB, PEER_SB), :], ubuf.at[slot], usem.at[slot])

    def v_copy(pair):
        slot = pair % V_SLOTS
        return pltpu.make_async_copy(vt_hbm.at[pair], vbuf.at[slot], vsem.at[slot])

    def activations(s, aslot):
        act = _dot(ubuf[s % U_SLOTS], ht_ref[...])
        for tc in range(n_chunks):
            act_refs[aslot][tc] = act[:, tc * LANES:(tc + 1) * LANES]

    def gated(s, aslot, wslot, pos):
        for tc in range(n_chunks):
            a_rows = [a_ref[tc, s * keys_per_sub + k] for k in range(keys_per_sub)]
            m_rows = [m_ref[tc, s * keys_per_sub + k] for k in range(keys_per_sub)]
            for part in range(PEER_NKEYS // GATE_ROWS):
                pwords = slice(part * GATE_ROWS // 2, (part + 1) * GATE_ROWS // 2)
                gates = [None] * keys_per_sub
                for h in range(PEER_HEADS):
                    b = _as_bf16(b_ref[tc, h, pwords, :])
                    r = _as_bf16(r_ref[tc, h, pwords, :])
                    for k in range(keys_per_sub):
                        a_row = jnp.broadcast_to(a_rows[k][h:h + 1, :], (GATE_ROWS, LANES)).astype(BF16)
                        m_row = jnp.broadcast_to(m_rows[k][h:h + 1, :], (GATE_ROWS, LANES)).astype(BF16)
                        term = jnp.where(r < m_row, b, jnp.zeros_like(b)) * a_row
                        gates[k] = term if gates[k] is None else gates[k] + term
                for k in range(keys_per_sub):
                    row0 = k * PEER_NKEYS + part * GATE_ROWS
                    g = _gelu(act_refs[aslot][tc, row0:row0 + GATE_ROWS, :].astype(BF16))
                    word0 = pos * sub_words + row0 // 2
                    w_refs[wslot][tc, word0:word0 + GATE_ROWS // 2, :] = pltpu.bitcast(g * gates[k], jnp.uint32)

    def accumulate(pair, wslot, half):
        w = _as_bf16(jnp.concatenate([w_refs[wslot][tc] for tc in range(n_chunks)], axis=1))
        rows = slice(half * half_rows, (half + 1) * half_rows)
        acc_ref[rows, :] += _dot(vbuf[pair % V_SLOTS, rows, :], w)

    def stage_dma(s, jj):
        static = isinstance(s, int)

        def when(cond, fn):
            if static:
                if cond:
                    fn()
            else:
                pl.when(cond)(fn)

        when(s + U_AHEAD < n_sub, lambda: u_copy(s + U_AHEAD).start())
        if jj % 2 == 0:
            when((s + 1) // 2 < n_pairs, lambda: v_copy((s + 1) // 2).start(priority=V_DMA_PRIORITY))
        when(s < n_sub, lambda: u_copy(s).wait())
        if jj % 2 == 0:
            when(s >= 3, lambda: v_copy((s - 3) // 2).wait())

    def stage_compute(s, jj, with_activations=True, with_gate=True):
        if with_activations:
            activations(s, (jj + 1) % 2)
        if with_gate:
            gated(s - 1, jj % 2, (jj // 2) % 2, jj % 2)
        accumulate((s - 3) // 2, (jj // 2 + 1) % 2, jj % 2)

    acc_ref[...] = jnp.zeros_like(acc_ref)
    w_refs[1][...] = jnp.zeros_like(w_refs[1])

    @pl.when(pl.program_id(0) == 0)
    def _():
        vbuf[(-1) % V_SLOTS] = jnp.zeros(vbuf.shape[1:], vbuf.dtype)

    for s in range(U_AHEAD + 1):
        u_copy(s).start()
    v_copy(0).start(priority=V_DMA_PRIORITY)
    u_copy(0).wait()
    activations(0, 0)

    def stages(t, carry):
        for jj in range(STAGE_UNROLL):
            s = 1 + t * STAGE_UNROLL + jj
            stage_dma(s, jj)
            stage_compute(s, jj)
        return carry

    lax.fori_loop(0, n_sub // STAGE_UNROLL, stages, 0)
    for s in (n_sub + 1, n_sub + 2):
        jj = (s - 1) % STAGE_UNROLL
        stage_dma(s, jj)
        stage_compute(s, jj, with_activations=False, with_gate=False)

    out = x1_ref[...] + g2_ref[0, 0] * acc_ref[...].T
    ms = jnp.mean(out * out, axis=-1, keepdims=True)
    y_ref[...] = out * lax.rsqrt(ms + EPS) * fg_ref[...]


def _peer_dense(h2t, u_bf, vt_sub, a, m, b, r, x1, mods, final_g, tokens_per_row):
    T = h2t.shape[1]
    tm = PEER_TM
    assert T % tm == 0
    if tokens_per_row is None:
        g2_map = lambda i: (CTX_ROW, 5, 0, 0)
    else:
        assert tokens_per_row % tm == 0, "a token tile must not straddle two requests"
        g2_map = lambda i: ((i * tm) // tokens_per_row, 5, 0, 0)
    fspec = pl.BlockSpec((tm // LANES, PEER_NKEYS, PEER_HEADS, LANES), lambda i: (i, 0, 0, 0))
    sspec = pl.BlockSpec((tm // LANES, PEER_HEADS, PEER_NKEYS // 2, LANES), lambda i: (i, 0, 0, 0))
    return pl.pallas_call(
        _peer_dense_kernel,
        grid=(T // tm,),
        in_specs=[
            pl.BlockSpec((D_MODEL, tm), lambda i: (0, i)),
            fspec, fspec, sspec, sspec,
            pl.BlockSpec((tm, D_MODEL), lambda i: (i, 0)),
            pl.BlockSpec((1, 1, 1, D_MODEL), g2_map),
            pl.BlockSpec((1, D_MODEL), lambda i: (0, 0)),
            pl.BlockSpec(memory_space=pl.ANY),
            pl.BlockSpec(memory_space=pl.ANY),
        ],
        out_specs=pl.BlockSpec((tm, D_MODEL), lambda i: (i, 0)),
        out_shape=jax.ShapeDtypeStruct((T, D_MODEL), F32),
        scratch_shapes=[
            pltpu.VMEM((U_SLOTS, PEER_SB, D_MODEL), BF16),
            pltpu.VMEM((V_SLOTS, D_MODEL, 2 * PEER_SB), BF16),
            pltpu.SemaphoreType.DMA((U_SLOTS,)),
            pltpu.SemaphoreType.DMA((V_SLOTS,)),
            pltpu.VMEM((D_MODEL, tm), F32),
            pltpu.VMEM((tm // LANES, PEER_SB, LANES), F32),
            pltpu.VMEM((tm // LANES, PEER_SB, LANES), F32),
            pltpu.VMEM((tm // LANES, PEER_SB, LANES), jnp.uint32),
            pltpu.VMEM((tm // LANES, PEER_SB, LANES), jnp.uint32),
        ],
        compiler_params=_params(("arbitrary",)),
        name="peer_dense",
    )(h2t, a, m, b, r, x1, mods, final_g.reshape(1, D_MODEL), u_bf, vt_sub)


def _trunk(x, mods, h0r, h0i, s5w, p, tables, final_g, per_batch, grid):
    B, L, _ = x.shape
    a_re, a_im, wb, wc_re, wc_im = s5w
    zs2d, zp = _inproj(x, mods, p["norm1_g"], p["w_in"], per_batch)
    zs = zs2d.reshape(L, B, D_SSM)
    yf, sfr, sfi = _s5_direction(zs, a_re, a_im, wb, wc_re, wc_im, h0r[:, 0], h0i[:, 0], 0)
    yb, sbr, sbi = _s5_direction(zs, a_re, a_im, wb, wc_re, wc_im, h0r[:, 1], h0i[:, 1], 1)
    x1, h2t = _mix(x, zs2d, zp, yf.reshape(L, B * D_SSM), yb.reshape(L, B * D_SSM), mods, p["ssm_d"],
                   p["w_glu"], p["w_pool"], p["pool_scale"], p["w_out"], p["norm2_g"], per_batch, grid)
    wq_t, wka, kb, u_bf, vt_bf = tables
    a, m, b, r = _peer_prep(h2t, wq_t, wka, kb)
    y = _peer_dense(h2t, u_bf, vt_bf, a, m, b, r, x1, mods, final_g, L if per_batch else None)
    new_re = jnp.stack([sfr, sbr], axis=1).reshape(B, 2, SSM_G, SSM_P)
    new_im = jnp.stack([sfi, sbi], axis=1).reshape(B, 2, SSM_G, SSM_P)
    return y.reshape(B, L, D_MODEL), new_re, new_im


def kernel(x_prompt, x_sample, state_ssm_re, state_ssm_im, c, c_ctx, norm1_g, w_mod, b_mod, w_in,
           ssm_lambda_re, ssm_lambda_im, ssm_log_dt, ssm_b_re, ssm_b_im, ssm_c_re, ssm_c_im, ssm_d,
           w_glu, w_pool, pool_scale, w_out, norm2_g, peer_wq, peer_subkeys, peer_u, peer_v, final_g):
    depth = w_mod.shape[0]
    assert depth == 1, "single trunk layer"
    l = 0
    n_dec = c.shape[0]
    cond = jnp.zeros((COND_ROWS, D_MODEL), F32).at[:n_dec].set(c).at[CTX_ROW].set(c_ctx)
    mods = _mod_vectors(cond, w_mod[l], b_mod[l]).reshape(COND_ROWS, N_MOD, 1, D_MODEL)

    ar, ai, bbr, bbi = _discretise(ssm_lambda_re[l], ssm_lambda_im[l], ssm_log_dt[l], ssm_b_re[l], ssm_b_im[l])
    s5w = _s5_weights(ar, ai, bbr, bbi, ssm_c_re[l], ssm_c_im[l])

    wq_t, wka, kb = _peer_tables(peer_wq[l], peer_subkeys[l])
    vt_sub = jnp.transpose(peer_v[l].reshape(PEER_N // (2 * PEER_SB), 2 * PEER_SB, D_MODEL), (0, 2, 1)).astype(BF16)
    tables = (wq_t, wka, kb, peer_u[l].astype(BF16), vt_sub)

    p = {"norm1_g": norm1_g[l], "w_in": w_in[l].astype(BF16), "ssm_d": ssm_d[l], "w_glu": w_glu[l],
         "w_pool": w_pool[l], "pool_scale": pool_scale[l], "w_out": w_out[l], "norm2_g": norm2_g[l]}

    bp = x_prompt.shape[0]
    zeros = jnp.zeros((bp, 2, SSM_N), F32)
    y_prompt, new_re, new_im = _trunk(x_prompt, mods, zeros, zeros, s5w, p, tables, final_g, False, False)
    h0r = state_ssm_re[:, l].reshape(n_dec, 2, SSM_N)
    h0i = state_ssm_im[:, l].reshape(n_dec, 2, SSM_N)
    y_sample, _, _ = _trunk(x_sample, mods, h0r, h0i, s5w, p, tables, final_g, True, True)
    return (y_prompt, y_sample, new_re[:, None], new_im[:, None])
```
